```python
import jax, jax.numpy as jnp
from jax import lax
import numpy as np

D_MODEL = 1024
BATCH = 1
SEQ = 16384
DEPTH = 2

CHUNK = 64
HEAD_DIM = 64
N_HEADS = D_MODEL // HEAD_DIM
DECAY_LORA = 64
AAA_LORA = 64
GATE_LORA = 128
GN_EPS = 64e-5
RMS_EPS = 1e-6
Q_BLOCK = 128
D_FF = 3584
N_EXPERTS = 8
TOP_K = 2
MOE_ROW_BLOCK = 256
N_A = (DEPTH + 1) // 2
N_B = DEPTH // 2

kernel_name = 'hybrid_rwkv7_fox_moe_trunk'


def rmsnorm(x, g):
    xf = x.astype(jnp.float32)
    y = xf * lax.rsqrt(jnp.mean(xf * xf, axis=-1, keepdims=True) + RMS_EPS)
    return (y * g).astype(x.dtype)


def swiglu(x, w_gu, w_down):
    gate, up = jnp.split(x @ w_gu, 2, axis=-1)
    return (jax.nn.silu(gate) * up) @ w_down


def rwkv7_time_mix(x, mu, w_rkv, w0, w1, w2, a0, a1, a2, g1, g2, k_k, k_a, r_k, gn_g, gn_b, w_o):
    B, T, D = x.shape
    H, N = N_HEADS, HEAD_DIM
    f32 = jnp.float32
    x_prev = jnp.pad(x, ((0, 0), (1, 0), (0, 0)))[:, :T]
    xs = x[None] + (x_prev - x)[None] * mu[:, None, None, :]
    r, k, v = jnp.einsum('nbtd,nde->nbte', xs[:3], w_rkv)
    xw, xa, xg = xs[3], xs[4], xs[5]
    w = -jax.nn.softplus(-(w0 + jnp.tanh(xw @ w1) @ w2).astype(f32)) - 0.5
    decay = jnp.exp(-jnp.exp(w))
    a = jax.nn.sigmoid((a0 + (xa @ a1) @ a2).astype(f32))
    g = jax.nn.sigmoid(xg @ g1) @ g2
    heads = lambda z: z.astype(f32).reshape(B, T, H, N)
    kk = heads(k * k_k)
    kk = kk / jnp.maximum(jnp.sqrt(jnp.sum(kk * kk, axis=-1, keepdims=True)), 1e-12)
    k_mod = k.astype(f32) * (1.0 + (a - 1.0) * k_a)
    r_h, k_h, v_h, w_h, a_h = heads(r), heads(k_mod), heads(v), heads(decay), heads(a)
    tm = lambda z: z.transpose(1, 0, 2, 3)

    def step(S, inp):
        r_t, w_t, k_t, v_t, kk_t, a_t = inp
        sa = jnp.einsum('bhij,bhj->bhi', S, -kk_t)
        S = (S * w_t[:, :, None, :] + sa[..., None] * (kk_t * a_t)[:, :, None, :]
             + v_t[..., None] * k_t[:, :, None, :])
        return S, jnp.einsum('bhij,bhj->bhi', S, r_t)

    S0 = jnp.zeros((B, H, N, N), f32)
    _, y = lax.scan(step, S0, (tm(r_h), tm(w_h), tm(k_h), tm(v_h), tm(kk), tm(a_h)))
    y = y.transpose(1, 0, 2, 3)
    mean = jnp.mean(y, axis=-1, keepdims=True)
    var = jnp.mean(jnp.square(y - mean), axis=-1, keepdims=True)
    y = ((y - mean) * lax.rsqrt(var + GN_EPS)).reshape(B, T, D) * gn_g + gn_b
    bonus = jnp.sum(r_h * k_h * r_k, axis=-1, keepdims=True) * v_h
    y = (y + bonus.reshape(B, T, D)) * g
    return y.astype(x.dtype) @ w_o


def forgetting_attention(x, w_in, b_f, q_gain, k_gain, w_o):
    B, T, D = x.shape
    H, N, QB = N_HEADS, HEAD_DIM, Q_BLOCK
    f32 = jnp.float32
    proj = x @ w_in
    q, k, v, og = jnp.split(proj[..., :4 * D], 4, axis=-1)
    log_f = jax.nn.log_sigmoid((proj[..., 4 * D:] + b_f).astype(f32))
    c = jnp.cumsum(log_f, axis=1).transpose(0, 2, 1)
    q = rmsnorm(q.reshape(B, T, H, N), q_gain).transpose(0, 2, 1, 3)
    k = rmsnorm(k.reshape(B, T, H, N), k_gain).transpose(0, 2, 1, 3)
    v = v.reshape(B, T, H, N).transpose(0, 2, 1, 3)
    nblk = T // QB
    q_blocks = q.reshape(B, H, nblk, QB, N).transpose(2, 0, 1, 3, 4)
    c_blocks = c.reshape(B, H, nblk, QB).transpose(2, 0, 1, 3)
    kpos = jnp.arange(T)
    scale = N ** -0.5

    def attend(args):
        i, q_i, c_i = args
        qpos = i * QB + jnp.arange(QB)
        s = (jnp.einsum('bhqd,bhkd->bhqk', q_i, k).astype(f32) * scale
             + (c_i[..., :, None] - c[..., None, :]))
        s = jnp.where(kpos[None, :] <= qpos[:, None], s, -jnp.inf)
        p = jax.nn.softmax(s, axis=-1)
        return jnp.einsum('bhqk,bhkd->bhqd', p.astype(v.dtype), v)

    o = lax.map(attend, (jnp.arange(nblk), q_blocks, c_blocks))
    o = o.transpose(1, 0, 3, 2, 4).reshape(B, T, D)
    return (o * jax.nn.sigmoid(og)) @ w_o


def moe_swiglu(x, w_router, w_gu, w_down):
    B, T, D = x.shape
    E, K, RB = N_EXPERTS, TOP_K, MOE_ROW_BLOCK
    xf = x.reshape(B * T, D)
    logits = (xf @ w_router).astype(jnp.float32)
    top_v, top_i = lax.top_k(logits, K)
    gates = jax.nn.softmax(top_v, axis=-1)
    A = B * T * K
    NB = -(-A // RB) + E
    flat_e = top_i.reshape(A)
    flat_tok = jnp.repeat(jnp.arange(B * T, dtype=jnp.int32), K)
    flat_w = gates.reshape(A)
    order = jnp.argsort(flat_e, stable=True)
    sorted_e = flat_e[order]
    counts = jnp.bincount(flat_e, length=E)
    padded = (counts + RB - 1) // RB * RB
    ends_pad = jnp.cumsum(padded)
    start_pad = ends_pad - padded
    start = jnp.cumsum(counts) - counts
    dest = start_pad[sorted_e] + jnp.arange(A, dtype=jnp.int32) - start[sorted_e]
    buf_tok = jnp.zeros((NB * RB,), jnp.int32).at[dest].set(flat_tok[order])
    buf_w = jnp.zeros((NB * RB,), jnp.float32).at[dest].set(flat_w[order])
    block_e = jnp.minimum(jnp.sum(jnp.arange(NB)[:, None] * RB >= ends_pad[None, :], axis=1), E - 1)

    def expert_block(args):
        e, tok, wt = args
        return swiglu(xf[tok], w_gu[e], w_down[e]).astype(jnp.float32) * wt[:, None]

    ys = lax.map(expert_block, (block_e, buf_tok.reshape(NB, RB), buf_w.reshape(NB, RB)))
    out = jnp.zeros((B * T, D), jnp.float32).at[buf_tok].add(ys.reshape(NB * RB, D))
    return out.astype(x.dtype).reshape(B, T, D)


def setup_inputs(seed: int = 0) -> dict:
    key = jax.random.key(seed)
    ks = iter(jax.random.split(key, 40))
    f32 = jnp.float32
    D, H, N, F, E = D_MODEL, N_HEADS, HEAD_DIM, D_FF, N_EXPERTS
    nrm = lambda shape, s: jax.random.normal(next(ks), shape, f32) * s
    uni = lambda shape, lo, hi: jax.random.uniform(next(ks), shape, f32, lo, hi)
    return {
        'x': nrm((BATCH, SEQ, D), 1.0),
        'norm_g': 1.0 + nrm((DEPTH, 2, D), 0.02),
        'final_g': 1.0 + nrm((D,), 0.02),
        'rwkv_mu': uni((N_A, 6, D), 0.0, 1.0),
        'rwkv_w_rkv': nrm((N_A, 3, D, D), D ** -0.5),
        'rwkv_w0': uni((N_A, D), -5.0, 1.0),
        'rwkv_w1': nrm((N_A, D, DECAY_LORA), D ** -0.5),
        'rwkv_w2': nrm((N_A, DECAY_LORA, D), 0.5 * DECAY_LORA ** -0.5),
        'rwkv_a0': nrm((N_A, D), 0.1),
        'rwkv_a1': nrm((N_A, D, AAA_LORA), D ** -0.5),
        'rwkv_a2': nrm((N_A, AAA_LORA, D), 0.5 * AAA_LORA ** -0.5),
        'rwkv_g1': nrm((N_A, D, GATE_LORA), D ** -0.5),
        'rwkv_g2': nrm((N_A, GATE_LORA, D), GATE_LORA ** -0.5),
        'rwkv_k_k': 0.85 + nrm((N_A, D), 0.02),
        'rwkv_k_a': 1.0 + nrm((N_A, D), 0.02),
        'rwkv_r_k': nrm((N_A, H, N), 0.1),
        'rwkv_gn_g': 1.0 + nrm((N_A, D), 0.02),
        'rwkv_gn_b': nrm((N_A, D), 0.02),
        'rwkv_w_o': nrm((N_A, D, D), D ** -0.5),
        'fox_w_in': nrm((N_B, D, 4 * D + H), D ** -0.5),
        'fox_b_f': uni((N_B, H), 1.0, 4.0),
        'fox_q_gain': 1.0 + nrm((N_B, N), 0.02),
        'fox_k_gain': 1.0 + nrm((N_B, N), 0.02),
        'fox_w_o': nrm((N_B, D, D), D ** -0.5),
        'ffn_w_gu': nrm((N_A, D, 2 * F), D ** -0.5),
        'ffn_w_down': nrm((N_A, F, D), F ** -0.5),
        'moe_w_router': nrm((N_B, D, E), D ** -0.5),
        'moe_w_gu': nrm((N_B, E, D, 2 * F), D ** -0.5),
        'moe_w_down': nrm((N_B, E, F, D), F ** -0.5),
    }


def reference(x, norm_g, final_g, rwkv_mu, rwkv_w_rkv, rwkv_w0, rwkv_w1, rwkv_w2, rwkv_a0, rwkv_a1,
              rwkv_a2, rwkv_g1, rwkv_g2, rwkv_k_k, rwkv_k_a, rwkv_r_k, rwkv_gn_g, rwkv_gn_b, rwkv_w_o,
              fox_w_in, fox_b_f, fox_q_gain, fox_k_gain, fox_w_o, ffn_w_gu, ffn_w_down,
              moe_w_router, moe_w_gu, moe_w_down):
    h = x
    for i in range(DEPTH):
        j = i // 2
        u = rmsnorm(h, norm_g[i, 0])
        if i % 2 == 0:
            h = h + rwkv7_time_mix(u, rwkv_mu[j], rwkv_w_rkv[j], rwkv_w0[j], rwkv_w1[j], rwkv_w2[j],
                                   rwkv_a0[j], rwkv_a1[j], rwkv_a2[j], rwkv_g1[j], rwkv_g2[j],
                                   rwkv_k_k[j], rwkv_k_a[j], rwkv_r_k[j], rwkv_gn_g[j], rwkv_gn_b[j],
                                   rwkv_w_o[j])
            u = rmsnorm(h, norm_g[i, 1])
            h = h + swiglu(u, ffn_w_gu[j], ffn_w_down[j])
        else:
            h = h + forgetting_attention(u, fox_w_in[j], fox_b_f[j], fox_q_gain[j], fox_k_gain[j],
                                         fox_w_o[j])
            u = rmsnorm(h, norm_g[i, 1])
            h = h + moe_swiglu(u, moe_w_router[j], moe_w_gu[j], moe_w_down[j])
    return rmsnorm(h, final_g)
```

```python
import functools

import jax
import jax.numpy as jnp
from jax import lax
from jax.experimental import pallas as pl
from jax.experimental.pallas import tpu as pltpu

F32 = jnp.float32
BF16 = jnp.bfloat16

HEAD_DIM = 64
PAIR = 2 * HEAD_DIM
GN_EPS = 64e-5
RMS_EPS = 1e-6
N_EXPERTS = 8
SCAN_CHUNK = 64
MOE_ROWS = 512
VMEM_LIMIT = 56 * 1024 * 1024


def _cparams(*sem):
    return pltpu.CompilerParams(dimension_semantics=sem, vmem_limit_bytes=VMEM_LIMIT)


def _dot(a, b):
    return jnp.dot(a.astype(BF16), b.astype(BF16), preferred_element_type=F32)


def _dot_nt(a, b):
    return lax.dot_general(a.astype(BF16), b.astype(BF16), (((1,), (1,)), ((), ())),
                           preferred_element_type=F32)


def _dot_tn(a, b):
    return lax.dot_general(a.astype(BF16), b.astype(BF16), (((0,), (0,)), ((), ())),
                           preferred_element_type=F32)


def _split_hi_lo(x):
    hi = x.astype(BF16)
    lo = (x - hi.astype(F32)).astype(BF16)
    return hi, lo


def _rms(x, g):
    ms = jnp.mean(x * x, axis=-1, keepdims=True)
    return x * lax.rsqrt(ms + RMS_EPS) * g


def _sigmoid(x):
    return 1.0 / (1.0 + jnp.exp(-x))


def _full(shape):
    return pl.BlockSpec(shape, lambda *_: (0,) * len(shape))


def _rwkv_pre_kernel(h_ref, hp_ref, ng_ref, mu_ref, wrkv_ref, w0_ref, w1_ref, w2_ref, a0_ref,
                     a1_ref, a2_ref, g1_ref, g2_ref, r_ref, k_ref, v_ref, lw_ref, a_ref, g_ref):
    i = pl.program_id(0)
    ng = ng_ref[...]
    u = _rms(h_ref[...], ng)
    up = _rms(hp_ref[7:8, :], ng)
    up = jnp.where(i == 0, 0.0, up)
    row = lax.broadcasted_iota(jnp.int32, u.shape, 0)
    u_prev = jnp.where(row == 0, up, pltpu.roll(u, 1, axis=0))
    dx = u_prev - u
    mix = lambda n: u + dx * mu_ref[n:n + 1, :]
    r_ref[...] = _dot(mix(0), wrkv_ref[0])
    k_ref[...] = _dot(mix(1), wrkv_ref[1])
    v_ref[...] = _dot(mix(2), wrkv_ref[2])
    z = w0_ref[...] + _dot(jnp.tanh(_dot(mix(3), w1_ref[...])), w2_ref[...])
    softplus_neg = jnp.maximum(-z, 0.0) + jnp.log(1.0 + jnp.exp(-jnp.abs(z)))
    lw_ref[...] = -jnp.exp(-softplus_neg - 0.5)
    a_ref[...] = _sigmoid(a0_ref[...] + _dot(_dot(mix(4), a1_ref[...]), a2_ref[...]))
    g_ref[...] = _dot(_sigmoid(_dot(mix(5), g1_ref[...])), g2_ref[...])


def _rwkv_pre(h, ng, mu, wrkv, w0, w1, w2, a0, a1, a2, g1, g2, tm=256):
    T, D = h.shape
    row = pl.BlockSpec((tm, D), lambda i: (i, 0))
    prev = pl.BlockSpec((8, D), lambda i: (jnp.maximum(i * (tm // 8) - 1, 0), 0))
    ins = [ng, mu, wrkv, w0, w1, w2, a0, a1, a2, g1, g2]
    return pl.pallas_call(
        _rwkv_pre_kernel,
        grid=(T // tm,),
        in_specs=[row, prev] + [_full(a.shape) for a in ins],
        out_specs=[row] * 6,
        out_shape=[jax.ShapeDtypeStruct((T, D), F32)] * 6,
        compiler_params=_cparams("parallel"),
        name="rwkv_pre",
    )(h, h, *ins)


def _scan_kernel(r_ref, k_ref, v_ref, lw_ref, a_ref, g_ref, kk_ref, ka_ref, rk_ref, gng_ref,
                 gnb_ref, o_ref, s_ref):
    L = SCAN_CHUNK
    n_pairs = r_ref.shape[1] // PAIR

    @pl.when(pl.program_id(0) == 0)
    def _():
        s_ref[...] = jnp.zeros_like(s_ref)

    row = lax.broadcasted_iota(jnp.int32, (L, L), 0)
    col = lax.broadcasted_iota(jnp.int32, (L, L), 1)
    tri_incl = jnp.where(col <= row, 1.0, 0.0).astype(BF16)
    eye = jnp.where(col == row, 1.0, 0.0).astype(F32)
    lane = lax.broadcasted_iota(jnp.int32, (1, PAIR), 1)
    head0 = lane < HEAD_DIM
    arow = lax.broadcasted_iota(jnp.int32, (4 * L, PAIR), 0)
    acol = lax.broadcasted_iota(jnp.int32, (4 * L, PAIR), 1)
    t_idx = arow % L + jnp.where(arow < 2 * L, 0, 1)
    j_idx = acol % L
    a_mask = j_idx < t_idx
    brow = lax.broadcasted_iota(jnp.int32, (PAIR, PAIR), 0)
    bcol = lax.broadcasted_iota(jnp.int32, (PAIR, PAIR), 1)
    blockdiag = (brow < HEAD_DIM) == (bcol < HEAD_DIM)

    def head_sum(x):
        s0 = jnp.sum(jnp.where(head0, x, 0.0), axis=-1, keepdims=True)
        s1 = jnp.sum(jnp.where(head0, 0.0, x), axis=-1, keepdims=True)
        return jnp.where(head0, s0, s1)

    for p in range(n_pairs):
        sl = slice(p * PAIR, (p + 1) * PAIR)
        lw = lw_ref[:, sl]
        lw_hi, lw_lo = _split_hi_lo(lw)
        cs = (jnp.dot(tri_incl, lw_hi, preferred_element_type=F32)
              + jnp.dot(tri_incl, lw_lo, preferred_element_type=F32))
        w_incl = jnp.exp(cs)
        w_excl = jnp.exp(cs - lw)
        w_inv = jnp.exp(-cs)
        w_last = jnp.exp(cs[L - 1:L, :])
        r = r_ref[:, sl]
        k = k_ref[:, sl]
        v = v_ref[:, sl]
        a = a_ref[:, sl]
        kk = k * kk_ref[:, sl]
        kk = kk / jnp.maximum(jnp.sqrt(head_sum(kk * kk)), 1e-12)
        kmod = k * (1.0 + (a - 1.0) * ka_ref[:, sl])
        beta = kk * a
        at = -kk * w_excl
        bt = beta * w_inv
        kt = kmod * w_inv
        rt = r * w_incl
        bhat = bt * w_last
        khat = kt * w_last

        x_stack = jnp.concatenate([jnp.where(head0, at, 0.0), jnp.where(head0, 0.0, at),
                                   jnp.where(head0, rt, 0.0), jnp.where(head0, 0.0, rt)], axis=0)
        bk = jnp.concatenate([bt, kt], axis=0)
        aa = jnp.where(a_mask, _dot_nt(x_stack, bk), 0.0)

        s = s_ref[p]
        g_inter = _dot_nt(at, s)
        vv = jnp.concatenate([v, v], axis=0)
        v_bf = v.astype(BF16)
        u_heads = []
        for h in range(2):
            a_rows = aa[h * L:(h + 1) * L, :]
            a_ab = a_rows[:, :L]
            t_inv = eye + a_ab
            pw = a_ab
            for _ in range(5):
                pw = _dot(pw, pw)
                t_inv = t_inv + _dot(t_inv, pw)
            rhs = g_inter + _dot(jnp.where(head0, 0.0, a_rows), vv)
            u_heads.append(_dot(t_inv, rhs))
        u = jnp.where(head0, u_heads[0], u_heads[1])
        y_heads = []
        for h in range(2):
            ar = aa[(2 + h) * L:(3 + h) * L, :]
            uv = jnp.concatenate([u_heads[h].astype(BF16), v_bf], axis=0)
            y_heads.append(_dot(ar, uv))
        y = _dot_nt(rt, s) + jnp.where(head0, y_heads[0], y_heads[1])

        uv = jnp.concatenate([u, v], axis=0)
        bkhat = jnp.concatenate([bhat, khat], axis=0)
        s_ref[p] = s * w_last + jnp.where(blockdiag, _dot_tn(uv, bkhat), 0.0)

        mean = head_sum(y) * (1.0 / HEAD_DIM)
        d = y - mean
        var = head_sum(d * d) * (1.0 / HEAD_DIM)
        yn = d * lax.rsqrt(var + GN_EPS) * gng_ref[:, sl] + gnb_ref[:, sl]
        bonus = head_sum(r * kmod * rk_ref[:, sl]) * v
        o_ref[:, sl] = ((yn + bonus) * g_ref[:, sl]).astype(o_ref.dtype)


def _rwkv_scan(r, k, v, lw, a, g, k_k, k_a, r_k, gn_g, gn_b):
    T, D = r.shape
    L = SCAN_CHUNK
    row = pl.BlockSpec((L, D), lambda c: (c, 0))
    par = _full((1, D))
    return pl.pallas_call(
        _scan_kernel,
        grid=(T // L,),
        in_specs=[row] * 6 + [par] * 5,
        out_specs=row,
        out_shape=jax.ShapeDtypeStruct((T, D), BF16),
        scratch_shapes=[pltpu.VMEM((D // PAIR, PAIR, PAIR), F32)],
        compiler_params=_cparams("arbitrary"),
        name="rwkv_scan",
    )(r, k, v, lw, a, g, k_k, k_a, r_k, gn_g, gn_b)


def _mm_res_kernel(x_ref, w_ref, res_ref, o_ref):
    o_ref[...] = res_ref[...] + _dot(x_ref[...], w_ref[...])


def _mm_res(x, w, res, tm=512):
    T, K = x.shape
    N = w.shape[1]
    return pl.pallas_call(
        _mm_res_kernel,
        grid=(T // tm,),
        in_specs=[pl.BlockSpec((tm, K), lambda i: (i, 0)), _full((K, N)),
                  pl.BlockSpec((tm, N), lambda i: (i, 0))],
        out_specs=pl.BlockSpec((tm, N), lambda i: (i, 0)),
        out_shape=jax.ShapeDtypeStruct((T, N), F32),
        compiler_params=_cparams("parallel"),
        name="mm_res",
    )(x, w, res)


def _swiglu_step(u_ref, wg_ref, wu_ref, wd_ref, acc_ref):
    u = u_ref[...]
    gate = jnp.dot(u, wg_ref[...], preferred_element_type=F32)
    up = jnp.dot(u, wu_ref[...], preferred_element_type=F32)
    hid = (gate * _sigmoid(gate) * up).astype(BF16)
    acc_ref[...] += jnp.dot(hid, wd_ref[...], preferred_element_type=F32)


def _ffn_kernel(h_ref, ng_ref, wg_ref, wu_ref, wd_ref, o_ref, u_ref, acc_ref):
    f = pl.program_id(1)

    @pl.when(f == 0)
    def _():
        u_ref[...] = _rms(h_ref[...], ng_ref[...]).astype(BF16)
        acc_ref[...] = jnp.zeros_like(acc_ref)

    _swiglu_step(u_ref, wg_ref, wu_ref, wd_ref, acc_ref)

    @pl.when(f == pl.num_programs(1) - 1)
    def _():
        o_ref[...] = h_ref[...] + acc_ref[...]


def _ffn(h, ng, w_gu, w_down, tm=1024, tf=512):
    T, D = h.shape
    F = w_down.shape[0]
    nf = F // tf
    row = pl.BlockSpec((tm, D), lambda i, f: (i, 0))
    return pl.pallas_call(
        _ffn_kernel,
        grid=(T // tm, nf),
        in_specs=[row, _full((1, D)),
                  pl.BlockSpec((D, tf), lambda i, f: (0, f)),
                  pl.BlockSpec((D, tf), lambda i, f: (0, f + nf)),
                  pl.BlockSpec((tf, D), lambda i, f: (f, 0))],
        out_specs=row,
        out_shape=jax.ShapeDtypeStruct((T, D), F32),
        scratch_shapes=[pltpu.VMEM((tm, D), BF16), pltpu.VMEM((tm, D), F32)],
        compiler_params=_cparams("parallel", "arbitrary"),
        name="ffn",
    )(h, ng, w_gu, w_gu, w_down)


def _moe_ffn_kernel(be_ref, nb_ref, x_ref, rw_ref, wg_ref, wu_ref, wd_ref, o_ref, u_ref, acc_ref):
    b = pl.program_id(0)
    f = pl.program_id(1)
    live = b < nb_ref[0]

    @pl.when(f == 0)
    def _():
        u_ref[...] = x_ref[...].astype(BF16)
        acc_ref[...] = jnp.zeros_like(acc_ref)

    @pl.when(live)
    def _():
        _swiglu_step(u_ref, wg_ref.at[0], wu_ref.at[0], wd_ref.at[0], acc_ref)

    @pl.when(f == pl.num_programs(1) - 1)
    def _():
        o_ref[...] = acc_ref[...] * rw_ref[...]


def _moe_ffn(block_e, n_live, xg, row_w, w_gu, w_down, tf=512):
    R, D = xg.shape
    F = w_down.shape[1]
    nf = F // tf
    tm = MOE_ROWS
    row = pl.BlockSpec((tm, D), lambda b, f, be, nb: (b, 0))
    grid_spec = pltpu.PrefetchScalarGridSpec(
        num_scalar_prefetch=2,
        grid=(R // tm, nf),
        in_specs=[row, pl.BlockSpec((tm, 1), lambda b, f, be, nb: (b, 0)),
                  pl.BlockSpec((1, D, tf), lambda b, f, be, nb: (be[b], 0, f)),
                  pl.BlockSpec((1, D, tf), lambda b, f, be, nb: (be[b], 0, f + nf)),
                  pl.BlockSpec((1, tf, D), lambda b, f, be, nb: (be[b], f, 0))],
        out_specs=row,
        scratch_shapes=[pltpu.VMEM((tm, D), BF16), pltpu.VMEM((tm, D), F32)],
    )
    return pl.pallas_call(
        _moe_ffn_kernel,
        grid_spec=grid_spec,
        out_shape=jax.ShapeDtypeStruct((R, D), F32),
        compiler_params=_cparams("parallel", "arbitrary"),
        name="moe_ffn",
    )(block_e, n_live, xg, row_w, w_gu, w_gu, w_down)


def _fox_pre_kernel(h_ref, ng_ref, w_ref, wf_ref, bf_ref, qg_ref, kg_ref, sel_ref, selt_ref,
                    q_ref, k_ref, v_ref, og_ref, c_ref, carry_ref):
    i = pl.program_id(0)
    tm, D = h_ref.shape

    @pl.when(i == 0)
    def _():
        carry_ref[...] = jnp.zeros_like(carry_ref)

    u = _rms(h_ref[...], ng_ref[...]).astype(BF16)

    def head_norm(x, gain):
        ms = _dot(x * x, sel_ref[...]) * (1.0 / HEAD_DIM)
        inv_hi, inv_lo = _split_hi_lo(lax.rsqrt(ms + RMS_EPS))
        inv = (jnp.dot(inv_hi, selt_ref[...], preferred_element_type=F32)
               + jnp.dot(inv_lo, selt_ref[...], preferred_element_type=F32))
        return x * inv * gain

    q = jnp.dot(u, w_ref[:, 0:D], preferred_element_type=F32)
    q_ref[...] = (head_norm(q, qg_ref[...]) * (HEAD_DIM ** -0.5)).astype(BF16)
    k = jnp.dot(u, w_ref[:, D:2 * D], preferred_element_type=F32)
    k_ref[...] = head_norm(k, kg_ref[...]).astype(BF16)
    v_ref[...] = jnp.dot(u, w_ref[:, 2 * D:3 * D], preferred_element_type=F32).astype(BF16)
    og_ref[...] = _sigmoid(jnp.dot(u, w_ref[:, 3 * D:4 * D], preferred_element_type=F32)).astype(BF16)

    fl = jnp.dot(u, wf_ref[...], preferred_element_type=F32) + bf_ref[...]
    log_f = jnp.minimum(fl, 0.0) - jnp.log(1.0 + jnp.exp(-jnp.abs(fl)))
    row = lax.broadcasted_iota(jnp.int32, (tm, tm), 0)
    col = lax.broadcasted_iota(jnp.int32, (tm, tm), 1)
    tri = jnp.where(col <= row, 1.0, 0.0).astype(BF16)
    hi, lo = _split_hi_lo(log_f)
    lo2 = (log_f - hi.astype(F32) - lo.astype(F32)).astype(BF16)
    c = (jnp.dot(tri, hi, preferred_element_type=F32) + jnp.dot(tri, lo, preferred_element_type=F32)
         + jnp.dot(tri, lo2, preferred_element_type=F32)) + carry_ref[...]
    c_ref[...] = c
    carry_ref[...] = c[tm - 1:tm, :]


def _fox_pre(h, ng, w_main, w_f, b_f, q_gain, k_gain, tm=256):
    T, D = h.shape
    H = D // HEAD_DIM
    sel = (jnp.arange(D)[:, None] // HEAD_DIM == jnp.arange(H)[None, :]).astype(BF16)
    row = pl.BlockSpec((tm, D), lambda i: (i, 0))
    ins = [ng, w_main, w_f, b_f, q_gain, k_gain, sel, sel.T]
    return pl.pallas_call(
        _fox_pre_kernel,
        grid=(T // tm,),
        in_specs=[row] + [_full(a.shape) for a in ins],
        out_specs=[row] * 4 + [pl.BlockSpec((tm, H), lambda i: (i, 0))],
        out_shape=[jax.ShapeDtypeStruct((T, D), BF16)] * 4 + [jax.ShapeDtypeStruct((T, H), F32)],
        scratch_shapes=[pltpu.VMEM((1, H), F32)],
        compiler_params=_cparams("arbitrary"),
        name="fox_pre",
    )(h, *ins)


def _fox_attn_kernel(q_ref, k_ref, v_ref, og_ref, cq_ref, ck_ref, o_ref, *, tq, tk):
    i = pl.program_id(1)
    lane = lax.broadcasted_iota(jnp.int32, (1, PAIR), 1)
    head0 = lane < HEAD_DIM
    q = q_ref[...]
    zero = jnp.zeros_like(q)
    q_heads = [jnp.where(head0, q, zero), jnp.where(head0, zero, q)]
    cq = cq_ref[0]
    cq_heads = [cq[:, 0:1], cq[:, 1:2]]
    qpos = i * tq + lax.broadcasted_iota(jnp.int32, (tq, tk), 0)
    kiota = lax.broadcasted_iota(jnp.int32, (tq, tk), 1)
    n_kv = (i * tq + tq + tk - 1) // tk

    def body(j, carry):
        start = pl.multiple_of(j * tk, tk)
        kb = k_ref[pl.ds(start, tk), :]
        vb = v_ref[pl.ds(start, tk), :]
        ck = ck_ref[0, :, pl.ds(start, tk)]
        causal = (j * tk + kiota) <= qpos
        out = []
        for h in range(2):
            m_prev, l_prev, acc_prev = carry[h]
            s = lax.dot_general(q_heads[h], kb, (((1,), (1,)), ((), ())), preferred_element_type=F32)
            s = s + (cq_heads[h] - ck[h:h + 1, :])
            s = jnp.where(causal, s, -jnp.inf)
            m_new = jnp.maximum(m_prev, jnp.max(s, axis=-1, keepdims=True))
            alpha = jnp.exp(m_prev - m_new)
            p = jnp.exp(s - m_new)
            l_new = alpha * l_prev + jnp.sum(p, axis=-1, keepdims=True)
            acc_new = alpha * acc_prev + jnp.dot(p.astype(BF16), vb, preferred_element_type=F32)
            out.append((m_new, l_new, acc_new))
        return tuple(out)

    init = tuple((jnp.full((tq, 1), -jnp.inf, F32), jnp.zeros((tq, 1), F32), jnp.zeros((tq, PAIR), F32))
                 for _ in range(2))
    res = lax.fori_loop(0, n_kv, body, init)
    o0 = res[0][2] / res[0][1]
    o1 = res[1][2] / res[1][1]
    o = jnp.where(head0, o0, o1)
    o_ref[...] = (o * og_ref[...].astype(F32)).astype(o_ref.dtype)


def _fox_attn(q, k, v, og, c_col, c_row, tq=512, tk=512):
    T, D = q.shape
    n_pairs = D // PAIR
    qspec = pl.BlockSpec((tq, PAIR), lambda p, i: (i, p))
    kvspec = pl.BlockSpec((T, PAIR), lambda p, i: (0, p))
    return pl.pallas_call(
        functools.partial(_fox_attn_kernel, tq=tq, tk=tk),
        grid=(n_pairs, T // tq),
        in_specs=[qspec, kvspec, kvspec, qspec,
                  pl.BlockSpec((1, tq, 2), lambda p, i: (p, i, 0)),
                  pl.BlockSpec((1, 2, T), lambda p, i: (p, 0, 0))],
        out_specs=qspec,
        out_shape=jax.ShapeDtypeStruct((T, D), BF16),
        compiler_params=_cparams("parallel", "arbitrary"),
        name="fox_attn",
    )(q, k, v, og, c_col, c_row)


def _router_kernel(h_ref, ng_ref, wr_ref, u_ref, info_ref, cnt_ref, carry_ref):
    i = pl.program_id(0)
    tm = h_ref.shape[0]
    E = N_EXPERTS

    @pl.when(i == 0)
    def _():
        carry_ref[...] = jnp.zeros_like(carry_ref)

    u = _rms(h_ref[...], ng_ref[...])
    u_ref[...] = u
    u_hi, u_lo = _split_hi_lo(u)
    w_hi, w_lo = _split_hi_lo(wr_ref[...])
    logits = (jnp.dot(u_hi, w_hi, preferred_element_type=F32)
              + jnp.dot(u_hi, w_lo, preferred_element_type=F32)
              + jnp.dot(u_lo, w_hi, preferred_element_type=F32))
    lane = lax.broadcasted_iota(jnp.int32, (tm, E), 1)
    v1 = jnp.max(logits, axis=-1, keepdims=True)
    e1 = jnp.min(jnp.where(logits == v1, lane, E), axis=-1, keepdims=True)
    rest = jnp.where(lane == e1, -jnp.inf, logits)
    v2 = jnp.max(rest, axis=-1, keepdims=True)
    e2 = jnp.min(jnp.where(rest == v2, lane, E), axis=-1, keepdims=True)
    g1 = 1.0 / (1.0 + jnp.exp(v2 - v1))
    g2 = 1.0 - g1
    oh1 = lane == e1
    oh2 = lane == e2
    onehot = jnp.where(oh1 | oh2, 1.0, 0.0).astype(BF16)
    row = lax.broadcasted_iota(jnp.int32, (tm, tm), 0)
    col = lax.broadcasted_iota(jnp.int32, (tm, tm), 1)
    tri_strict = jnp.where(col < row, 1.0, 0.0).astype(BF16)
    before = jnp.dot(tri_strict, onehot, preferred_element_type=F32) + carry_ref[...]
    rank1 = jnp.sum(jnp.where(oh1, before, 0.0), axis=-1, keepdims=True)
    rank2 = jnp.sum(jnp.where(oh2, before, 0.0), axis=-1, keepdims=True)
    info = jnp.where(lane == 0, e1.astype(F32), 0.0)
    info = jnp.where(lane == 1, e2.astype(F32), info)
    info = jnp.where(lane == 2, g1, info)
    info = jnp.where(lane == 3, g2, info)
    info = jnp.where(lane == 4, rank1, info)
    info = jnp.where(lane == 5, rank2, info)
    info_ref[...] = info
    total = carry_ref[...] + jnp.sum(onehot.astype(F32), axis=0, keepdims=True)
    carry_ref[...] = total
    cnt_ref[...] = total


def _router(h, ng, w_router, tm=512):
    T, D = h.shape
    E = N_EXPERTS
    row = pl.BlockSpec((tm, D), lambda i: (i, 0))
    return pl.pallas_call(
        _router_kernel,
        grid=(T // tm,),
        in_specs=[row, _full((1, D)), _full((D, E))],
        out_specs=[row, pl.BlockSpec((tm, E), lambda i: (i, 0)), _full((1, E))],
        out_shape=[jax.ShapeDtypeStruct((T, D), F32), jax.ShapeDtypeStruct((T, E), F32),
                   jax.ShapeDtypeStruct((1, E), F32)],
        scratch_shapes=[pltpu.VMEM((1, E), F32)],
        compiler_params=_cparams("arbitrary"),
        name="router",
    )(h, ng, w_router)


def _gather_kernel(idx_ref, src_ref, o_ref, sem):
    b = pl.program_id(0)
    rows = o_ref.shape[0]

    def row_copy(r):
        tok = idx_ref[b * rows + r]
        return pltpu.make_async_copy(src_ref.at[pl.ds(tok, 1)], o_ref.at[pl.ds(r, 1)], sem)

    def start(r, _):
        row_copy(r).start()
        return 0

    def wait(r, _):
        row_copy(r).wait()
        return 0

    lax.fori_loop(0, rows, start, 0)
    lax.fori_loop(0, rows, wait, 0)


def _gather_rows(idx, src, rows=MOE_ROWS):
    R = idx.shape[0]
    D = src.shape[1]
    grid_spec = pltpu.PrefetchScalarGridSpec(
        num_scalar_prefetch=1,
        grid=(R // rows,),
        in_specs=[pl.BlockSpec(memory_space=pl.ANY)],
        out_specs=pl.BlockSpec((rows, D), lambda b, idx: (b, 0)),
        scratch_shapes=[pltpu.SemaphoreType.DMA(())],
    )
    return pl.pallas_call(
        _gather_kernel,
        grid_spec=grid_spec,
        out_shape=jax.ShapeDtypeStruct((R, D), src.dtype),
        compiler_params=_cparams("arbitrary"),
        name="moe_gather",
    )(idx, src)


def _combine_kernel(pos_ref, h_ref, fg_ref, y_ref, o_ref, ya_ref, yb_ref, sem):
    b = pl.program_id(0)
    rows = h_ref.shape[0]

    def copies(r):
        base = 2 * (b * rows + r)
        return (pltpu.make_async_copy(y_ref.at[pl.ds(pos_ref[base], 1)], ya_ref.at[pl.ds(r, 1)], sem),
                pltpu.make_async_copy(y_ref.at[pl.ds(pos_ref[base + 1], 1)], yb_ref.at[pl.ds(r, 1)], sem))

    def start(r, _):
        ca, cb = copies(r)
        ca.start()
        cb.start()
        return 0

    def wait(r, _):
        ca, cb = copies(r)
        ca.wait()
        cb.wait()
        return 0

    lax.fori_loop(0, rows, start, 0)
    lax.fori_loop(0, rows, wait, 0)
    o_ref[...] = _rms(h_ref[...] + (ya_ref[...] + yb_ref[...]), fg_ref[...])


def _combine(pos, h, final_g, y, rows=256):
    T, D = h.shape
    row = pl.BlockSpec((rows, D), lambda b, pos: (b, 0))
    grid_spec = pltpu.PrefetchScalarGridSpec(
        num_scalar_prefetch=1,
        grid=(T // rows,),
        in_specs=[row, pl.BlockSpec((1, D), lambda b, pos: (0, 0)), pl.BlockSpec(memory_space=pl.ANY)],
        out_specs=row,
        scratch_shapes=[pltpu.VMEM((rows, D), F32), pltpu.VMEM((rows, D), F32),
                        pltpu.SemaphoreType.DMA(())],
    )
    return pl.pallas_call(
        _combine_kernel,
        grid_spec=grid_spec,
        out_shape=jax.ShapeDtypeStruct((T, D), F32),
        compiler_params=_cparams("arbitrary"),
        name="moe_combine",
    )(pos, h, final_g, y)


def _rwkv_layer(h, ng, mu, w_rkv, w0, w1, w2, a0, a1, a2, g1, g2, k_k, k_a, r_k, gn_g, gn_b, w_o):
    D = h.shape[1]
    vec = lambda z: z.reshape(1, D)
    b16 = lambda z: z.astype(BF16)
    r, k, v, lw, a, g = _rwkv_pre(h, vec(ng), mu, b16(w_rkv), vec(w0), b16(w1), b16(w2), vec(a0),
                                  b16(a1), b16(a2), b16(g1), b16(g2))
    o = _rwkv_scan(r, k, v, lw, a, g, vec(k_k), vec(k_a), vec(r_k), vec(gn_g), vec(gn_b))
    return _mm_res(o, b16(w_o), h)


def _fox_layer(h, ng, w_in, b_f, q_gain, k_gain, w_o):
    T, D = h.shape
    H = D // HEAD_DIM
    q, k, v, og, c = _fox_pre(h, ng.reshape(1, D), w_in[:, :4 * D].astype(BF16),
                              w_in[:, 4 * D:].astype(BF16), b_f.reshape(1, H),
                              jnp.tile(q_gain, H).reshape(1, D), jnp.tile(k_gain, H).reshape(1, D))
    c_col = c.reshape(T, H // 2, 2).transpose(1, 0, 2)
    c_row = c_col.transpose(0, 2, 1)
    o = _fox_attn(q, k, v, og, c_col, c_row)
    return _mm_res(o, w_o.astype(BF16), h)


def _moe_layer(h, ng, final_g, w_router, w_gu, w_down):
    T, D = h.shape
    E, RB = N_EXPERTS, MOE_ROWS
    u, info, cnt = _router(h, ng.reshape(1, D), w_router)
    e = info[:, 0:2].astype(jnp.int32)
    gate = info[:, 2:4]
    rank = info[:, 4:6].astype(jnp.int32)
    counts = cnt[0].astype(jnp.int32)
    padded = (counts + RB - 1) // RB * RB
    ends = jnp.cumsum(padded)
    starts = ends - padded
    n_blocks = (2 * T) // RB + E
    dest = (starts[e] + rank).reshape(-1)
    tok = jnp.repeat(jnp.arange(T, dtype=jnp.int32), 2)
    buf_tok = jnp.zeros((n_blocks * RB,), jnp.int32).at[dest].set(tok)
    buf_w = jnp.zeros((n_blocks * RB,), F32).at[dest].set(gate.reshape(-1))
    block_e = jnp.minimum(jnp.sum(jnp.arange(n_blocks)[:, None] * RB >= ends[None, :], axis=1),
                          E - 1).astype(jnp.int32)
    n_live = (ends[-1] // RB).astype(jnp.int32).reshape(1)
    xg = _gather_rows(buf_tok, u)
    y = _moe_ffn(block_e, n_live, xg, buf_w.reshape(-1, 1), w_gu.astype(BF16), w_down.astype(BF16))
    return _combine(dest, h, final_g.reshape(1, D), y)


def kernel(x, norm_g, final_g, rwkv_mu, rwkv_w_rkv, rwkv_w0, rwkv_w1, rwkv_w2, rwkv_a0, rwkv_a1,
           rwkv_a2, rwkv_g1, rwkv_g2, rwkv_k_k, rwkv_k_a, rwkv_r_k, rwkv_gn_g, rwkv_gn_b, rwkv_w_o,
           fox_w_in, fox_b_f, fox_q_gain, fox_k_gain, fox_w_o, ffn_w_gu, ffn_w_down,
           moe_w_router, moe_w_gu, moe_w_down):
    B, T, D = x.shape
    h = x.reshape(B * T, D)
    h = _rwkv_layer(h, norm_g[0, 0], rwkv_mu[0], rwkv_w_rkv[0], rwkv_w0[0], rwkv_w1[0], rwkv_w2[0],
                    rwkv_a0[0], rwkv_a1[0], rwkv_a2[0], rwkv_g1[0], rwkv_g2[0], rwkv_k_k[0],
                    rwkv_k_a[0], rwkv_r_k[0], rwkv_gn_g[0], rwkv_gn_b[0], rwkv_w_o[0])
    h = _ffn(h, norm_g[0, 1].reshape(1, D), ffn_w_gu[0].astype(BF16), ffn_w_down[0].astype(BF16))
    h = _fox_layer(h, norm_g[1, 0], fox_w_in[0], fox_b_f[0], fox_q_gain[0], fox_k_gain[0], fox_w_o[0])
    out = _moe_layer(h, norm_g[1, 1], final_g, moe_w_router[0], moe_w_gu[0], moe_w_down[0])
    return out.reshape(B, T, D)
```

```python
import functools

import jax
import jax.numpy as jnp
from jax import lax
from jax.experimental import pallas as pl
from jax.experimental.pallas import tpu as pltpu

F32 = jnp.float32
BF16 = jnp.bfloat16

HEAD_DIM = 64
PAIR = 2 * HEAD_DIM
GN_EPS = 64e-5
RMS_EPS = 1e-6
N_EXPERTS = 8
SCAN_CHUNK = 64
MOE_ROWS = 512
ATTN_BLOCK = 512
LOG2E = 1.4426950408889634
VMEM_LIMIT = 56 * 1024 * 1024


def _cparams(*sem):
    return pltpu.CompilerParams(dimension_semantics=sem, vmem_limit_bytes=VMEM_LIMIT)


def _dot(a, b):
    return jnp.dot(a.astype(BF16), b.astype(BF16), preferred_element_type=F32)


def _dot_nt(a, b):
    return lax.dot_general(a.astype(BF16), b.astype(BF16), (((1,), (1,)), ((), ())),
                           preferred_element_type=F32)


def _dot_tn(a, b):
    return lax.dot_general(a.astype(BF16), b.astype(BF16), (((0,), (0,)), ((), ())),
                           preferred_element_type=F32)


def _split_hi_lo(x):
    hi = x.astype(BF16)
    lo = (x - hi.astype(F32)).astype(BF16)
    return hi, lo


def _rms(x, g):
    ms = jnp.mean(x * x, axis=-1, keepdims=True)
    return x * lax.rsqrt(ms + RMS_EPS) * g


def _sigmoid(x):
    return 1.0 / (1.0 + jnp.exp(-x))


def _full(shape):
    return pl.BlockSpec(shape, lambda *_: (0,) * len(shape))


def _rwkv_pre_kernel(h_ref, hp_ref, ng_ref, mu_ref, wrkv_ref, w0_ref, w1_ref, w2_ref, a0_ref,
                     a1_ref, a2_ref, g1_ref, g2_ref, r_ref, k_ref, v_ref, lw_ref, a_ref, g_ref):
    i = pl.program_id(0)
    ng = ng_ref[...]
    u = _rms(h_ref[...], ng)
    up = _rms(hp_ref[7:8, :], ng)
    up = jnp.where(i == 0, 0.0, up)
    row = lax.broadcasted_iota(jnp.int32, u.shape, 0)
    u_prev = jnp.where(row == 0, up, pltpu.roll(u, 1, axis=0))
    dx = u_prev - u
    mix = lambda n: u + dx * mu_ref[n:n + 1, :]
    r_ref[...] = _dot(mix(0), wrkv_ref[0])
    k_ref[...] = _dot(mix(1), wrkv_ref[1])
    v_ref[...] = _dot(mix(2), wrkv_ref[2])
    z = w0_ref[...] + _dot(jnp.tanh(_dot(mix(3), w1_ref[...])), w2_ref[...])
    softplus_neg = jnp.maximum(-z, 0.0) + jnp.log(1.0 + jnp.exp(-jnp.abs(z)))
    lw_ref[...] = -jnp.exp(-softplus_neg - 0.5)
    a_ref[...] = _sigmoid(a0_ref[...] + _dot(_dot(mix(4), a1_ref[...]), a2_ref[...]))
    g_ref[...] = _dot(_sigmoid(_dot(mix(5), g1_ref[...])), g2_ref[...])


def _rwkv_pre(h, ng, mu, wrkv, w0, w1, w2, a0, a1, a2, g1, g2, tm=256):
    T, D = h.shape
    row = pl.BlockSpec((tm, D), lambda i: (i, 0))
    prev = pl.BlockSpec((8, D), lambda i: (jnp.maximum(i * (tm // 8) - 1, 0), 0))
    ins = [ng, mu, wrkv, w0, w1, w2, a0, a1, a2, g1, g2]
    return pl.pallas_call(
        _rwkv_pre_kernel,
        grid=(T // tm,),
        in_specs=[row, prev] + [_full(a.shape) for a in ins],
        out_specs=[row] * 6,
        out_shape=[jax.ShapeDtypeStruct((T, D), F32)] * 6,
        compiler_params=_cparams("parallel"),
        name="rwkv_pre",
    )(h, h, *ins)


def _scan_kernel(r_ref, k_ref, v_ref, lw_ref, a_ref, g_ref, kk_ref, ka_ref, rk_ref, gng_ref,
                 gnb_ref, o_ref, s_ref):
    L = SCAN_CHUNK
    n_pairs = r_ref.shape[1] // PAIR

    @pl.when(pl.program_id(0) == 0)
    def _():
        s_ref[...] = jnp.zeros_like(s_ref)

    row = lax.broadcasted_iota(jnp.int32, (L, L), 0)
    col = lax.broadcasted_iota(jnp.int32, (L, L), 1)
    tri_incl = jnp.where(col <= row, 1.0, 0.0).astype(BF16)
    eye = jnp.where(col == row, 1.0, 0.0).astype(F32)
    lane = lax.broadcasted_iota(jnp.int32, (1, PAIR), 1)
    head0 = lane < HEAD_DIM
    arow = lax.broadcasted_iota(jnp.int32, (4 * L, PAIR), 0)
    acol = lax.broadcasted_iota(jnp.int32, (4 * L, PAIR), 1)
    t_idx = arow % L + jnp.where(arow < 2 * L, 0, 1)
    j_idx = acol % L
    a_mask = j_idx < t_idx
    brow = lax.broadcasted_iota(jnp.int32, (PAIR, PAIR), 0)
    bcol = lax.broadcasted_iota(jnp.int32, (PAIR, PAIR), 1)
    blockdiag = (brow < HEAD_DIM) == (bcol < HEAD_DIM)

    def head_sum(x):
        s0 = jnp.sum(jnp.where(head0, x, 0.0), axis=-1, keepdims=True)
        s1 = jnp.sum(jnp.where(head0, 0.0, x), axis=-1, keepdims=True)
        return jnp.where(head0, s0, s1)

    pairs = range(n_pairs)
    heads = [(p, h) for p in pairs for h in range(2)]
    sls = [slice(p * PAIR, (p + 1) * PAIR) for p in pairs]

    lws = [lw_ref[:, sl] for sl in sls]
    splits = [_split_hi_lo(lw) for lw in lws]
    css = [jnp.dot(tri_incl, hi, preferred_element_type=F32) + jnp.dot(tri_incl, lo, preferred_element_type=F32)
           for hi, lo in splits]
    prep = []
    for p in pairs:
        sl, cs, lw = sls[p], css[p], lws[p]
        w_inv = jnp.exp(-cs)
        w_last = jnp.exp(cs[L - 1:L, :])
        r = r_ref[:, sl]
        k = k_ref[:, sl]
        a = a_ref[:, sl]
        kk = k * kk_ref[:, sl]
        kk = kk / jnp.maximum(jnp.sqrt(head_sum(kk * kk)), 1e-12)
        kmod = k * (1.0 + (a - 1.0) * ka_ref[:, sl])
        at = -kk * jnp.exp(cs - lw)
        bt = kk * a * w_inv
        kt = kmod * w_inv
        rt = r * jnp.exp(cs)
        prep.append(dict(at=at, rt=rt, bk=jnp.concatenate([bt, kt], axis=0).astype(BF16),
                         bkhat=jnp.concatenate([bt * w_last, kt * w_last], axis=0).astype(BF16),
                         w_last=w_last, v=v_ref[:, sl], rk=r * kmod))
    aas = []
    for p in pairs:
        at, rt = prep[p]["at"], prep[p]["rt"]
        x_stack = jnp.concatenate([jnp.where(head0, at, 0.0), jnp.where(head0, 0.0, at),
                                   jnp.where(head0, rt, 0.0), jnp.where(head0, 0.0, rt)], axis=0)
        aas.append(jnp.where(a_mask, _dot_nt(x_stack, prep[p]["bk"]), 0.0))
    states = [s_ref[p] for p in pairs]
    inter = [_dot_nt(jnp.concatenate([prep[p]["at"], prep[p]["rt"]], axis=0), states[p]) for p in pairs]
    v_bf = [prep[p]["v"].astype(BF16) for p in pairs]
    vv = [jnp.concatenate([v_bf[p], v_bf[p]], axis=0) for p in pairs]
    rhs = {}
    pw = {}
    t_inv = {}
    for p, h in heads:
        a_rows = aas[p][h * L:(h + 1) * L, :]
        rhs[p, h] = inter[p][:L, :] + _dot(jnp.where(head0, 0.0, a_rows), vv[p])
        pw[p, h] = a_rows[:, :L]
        t_inv[p, h] = eye + pw[p, h]
    for _ in range(5):
        for ph in heads:
            pw[ph] = _dot(pw[ph], pw[ph])
        for ph in heads:
            t_inv[ph] = t_inv[ph] + _dot(t_inv[ph], pw[ph])
    u_h = {ph: _dot(t_inv[ph], rhs[ph]) for ph in heads}
    y_h = {}
    for p, h in heads:
        ar = aas[p][(2 + h) * L:(3 + h) * L, :]
        y_h[p, h] = _dot(ar, jnp.concatenate([u_h[p, h].astype(BF16), v_bf[p]], axis=0))
    for p in pairs:
        sl = sls[p]
        v = prep[p]["v"]
        u = jnp.where(head0, u_h[p, 0], u_h[p, 1])
        uv = jnp.concatenate([u.astype(BF16), v_bf[p]], axis=0)
        s_ref[p] = states[p] * prep[p]["w_last"] + jnp.where(blockdiag, _dot_tn(uv, prep[p]["bkhat"]), 0.0)
        y = inter[p][L:, :] + jnp.where(head0, y_h[p, 0], y_h[p, 1])
        mean = head_sum(y) * (1.0 / HEAD_DIM)
        d = y - mean
        var = head_sum(d * d) * (1.0 / HEAD_DIM)
        yn = d * lax.rsqrt(var + GN_EPS) * gng_ref[:, sl] + gnb_ref[:, sl]
        bonus = head_sum(prep[p]["rk"] * rk_ref[:, sl]) * v
        o_ref[:, sl] = ((yn + bonus) * g_ref[:, sl]).astype(o_ref.dtype)


def _rwkv_scan(r, k, v, lw, a, g, k_k, k_a, r_k, gn_g, gn_b):
    T, D = r.shape
    L = SCAN_CHUNK
    row = pl.BlockSpec((L, D), lambda c: (c, 0))
    par = _full((1, D))
    return pl.pallas_call(
        _scan_kernel,
        grid=(T // L,),
        in_specs=[row] * 6 + [par] * 5,
        out_specs=row,
        out_shape=jax.ShapeDtypeStruct((T, D), BF16),
        scratch_shapes=[pltpu.VMEM((D // PAIR, PAIR, PAIR), F32)],
        compiler_params=_cparams("arbitrary"),
        name="rwkv_scan",
    )(r, k, v, lw, a, g, k_k, k_a, r_k, gn_g, gn_b)


def _mm_res_kernel(x_ref, w_ref, res_ref, o_ref):
    o_ref[...] = res_ref[...] + _dot(x_ref[...], w_ref[...])


def _mm_res(x, w, res, tm=512):
    T, K = x.shape
    N = w.shape[1]
    return pl.pallas_call(
        _mm_res_kernel,
        grid=(T // tm,),
        in_specs=[pl.BlockSpec((tm, K), lambda i: (i, 0)), _full((K, N)),
                  pl.BlockSpec((tm, N), lambda i: (i, 0))],
        out_specs=pl.BlockSpec((tm, N), lambda i: (i, 0)),
        out_shape=jax.ShapeDtypeStruct((T, N), F32),
        compiler_params=_cparams("parallel"),
        name="mm_res",
    )(x, w, res)


def _swiglu_step(u_ref, wg_ref, wu_ref, wd_ref, acc_ref):
    u = u_ref[...]
    gate = jnp.dot(u, wg_ref[...], preferred_element_type=F32)
    up = jnp.dot(u, wu_ref[...], preferred_element_type=F32)
    hid = (gate * _sigmoid(gate) * up).astype(BF16)
    acc_ref[...] += jnp.dot(hid, wd_ref[...], preferred_element_type=F32)


def _ffn_kernel(h_ref, ng_ref, wg_ref, wu_ref, wd_ref, o_ref, u_ref, acc_ref):
    f = pl.program_id(1)

    @pl.when(f == 0)
    def _():
        u_ref[...] = _rms(h_ref[...], ng_ref[...]).astype(BF16)
        acc_ref[...] = jnp.zeros_like(acc_ref)

    _swiglu_step(u_ref, wg_ref, wu_ref, wd_ref, acc_ref)

    @pl.when(f == pl.num_programs(1) - 1)
    def _():
        o_ref[...] = h_ref[...] + acc_ref[...]


def _ffn(h, ng, w_gu, w_down, tm=1024, tf=512):
    T, D = h.shape
    F = w_down.shape[0]
    nf = F // tf
    row = pl.BlockSpec((tm, D), lambda i, f: (i, 0))
    return pl.pallas_call(
        _ffn_kernel,
        grid=(T // tm, nf),
        in_specs=[row, _full((1, D)),
                  pl.BlockSpec((D, tf), lambda i, f: (0, f)),
                  pl.BlockSpec((D, tf), lambda i, f: (0, f + nf)),
                  pl.BlockSpec((tf, D), lambda i, f: (f, 0))],
        out_specs=row,
        out_shape=jax.ShapeDtypeStruct((T, D), F32),
        scratch_shapes=[pltpu.VMEM((tm, D), BF16), pltpu.VMEM((tm, D), F32)],
        compiler_params=_cparams("parallel", "arbitrary"),
        name="ffn",
    )(h, ng, w_gu, w_gu, w_down)


def _moe_ffn_kernel(be_ref, nb_ref, x_ref, rw_ref, wg_ref, wu_ref, wd_ref, o_ref, u_ref, acc_ref):
    b = pl.program_id(0)
    f = pl.program_id(1)
    live = b < nb_ref[0]

    @pl.when(f == 0)
    def _():
        u_ref[...] = x_ref[...].astype(BF16)
        acc_ref[...] = jnp.zeros_like(acc_ref)

    @pl.when(live)
    def _():
        _swiglu_step(u_ref, wg_ref.at[0], wu_ref.at[0], wd_ref.at[0], acc_ref)

    @pl.when(f == pl.num_programs(1) - 1)
    def _():
        o_ref[...] = acc_ref[...] * rw_ref[...]


def _moe_ffn(block_e, n_live, xg, row_w, w_gu, w_down, tf=512):
    R, D = xg.shape
    F = w_down.shape[1]
    nf = F // tf
    tm = MOE_ROWS
    row = pl.BlockSpec((tm, D), lambda b, f, be, nb: (b, 0))
    grid_spec = pltpu.PrefetchScalarGridSpec(
        num_scalar_prefetch=2,
        grid=(R // tm, nf),
        in_specs=[row, pl.BlockSpec((tm, 1), lambda b, f, be, nb: (b, 0)),
                  pl.BlockSpec((1, D, tf), lambda b, f, be, nb: (be[b], 0, f)),
                  pl.BlockSpec((1, D, tf), lambda b, f, be, nb: (be[b], 0, f + nf)),
                  pl.BlockSpec((1, tf, D), lambda b, f, be, nb: (be[b], f, 0))],
        out_specs=row,
        scratch_shapes=[pltpu.VMEM((tm, D), BF16), pltpu.VMEM((tm, D), F32)],
    )
    return pl.pallas_call(
        _moe_ffn_kernel,
        grid_spec=grid_spec,
        out_shape=jax.ShapeDtypeStruct((R, D), F32),
        compiler_params=_cparams("parallel", "arbitrary"),
        name="moe_ffn",
    )(block_e, n_live, xg, row_w, w_gu, w_gu, w_down)


def _fox_pre_kernel(h_ref, ng_ref, w_ref, wf_ref, bf_ref, qg_ref, kg_ref, sel_ref, selt_ref,
                    q_ref, k_ref, v_ref, og_ref, c2_ref, chi_ref, clo_ref, clo2_ref, carry_ref):
    i = pl.program_id(0)
    tm, D = h_ref.shape

    @pl.when(i == 0)
    def _():
        carry_ref[...] = jnp.zeros_like(carry_ref)

    u = _rms(h_ref[...], ng_ref[...]).astype(BF16)

    def head_norm(x, gain):
        ms = _dot(x * x, sel_ref[...]) * (1.0 / HEAD_DIM)
        inv_hi, inv_lo = _split_hi_lo(lax.rsqrt(ms + RMS_EPS))
        inv = (jnp.dot(inv_hi, selt_ref[...], preferred_element_type=F32)
               + jnp.dot(inv_lo, selt_ref[...], preferred_element_type=F32))
        return x * inv * gain

    q = jnp.dot(u, w_ref[:, 0:D], preferred_element_type=F32)
    q_ref[...] = (head_norm(q, qg_ref[...]) * (HEAD_DIM ** -0.5 * LOG2E)).astype(BF16)
    k = jnp.dot(u, w_ref[:, D:2 * D], preferred_element_type=F32)
    k_ref[...] = head_norm(k, kg_ref[...]).astype(BF16)
    v_ref[...] = jnp.dot(u, w_ref[:, 2 * D:3 * D], preferred_element_type=F32).astype(BF16)
    og_ref[...] = _sigmoid(jnp.dot(u, w_ref[:, 3 * D:4 * D], preferred_element_type=F32)).astype(BF16)

    fl = jnp.dot(u, wf_ref[...], preferred_element_type=F32) + bf_ref[...]
    log_f = jnp.minimum(fl, 0.0) - jnp.log(1.0 + jnp.exp(-jnp.abs(fl)))
    row = lax.broadcasted_iota(jnp.int32, (tm, tm), 0)
    col = lax.broadcasted_iota(jnp.int32, (tm, tm), 1)
    tri = jnp.where(col <= row, 1.0, 0.0).astype(BF16)
    hi, lo = _split_hi_lo(log_f)
    lo2 = (log_f - hi.astype(F32) - lo.astype(F32)).astype(BF16)
    c = (jnp.dot(tri, hi, preferred_element_type=F32) + jnp.dot(tri, lo, preferred_element_type=F32)
         + jnp.dot(tri, lo2, preferred_element_type=F32)) + carry_ref[...]
    carry_ref[...] = c[tm - 1:tm, :]
    c2 = c * LOG2E
    c2_ref[...] = c2
    c_hi, c_lo = _split_hi_lo(c2)
    chi_ref[...] = c_hi
    clo_ref[...] = c_lo
    clo2_ref[...] = (c2 - c_hi.astype(F32) - c_lo.astype(F32)).astype(BF16)


def _fox_pre(h, ng, w_main, w_f, b_f, q_gain, k_gain, tm=256):
    T, D = h.shape
    H = D // HEAD_DIM
    sel = (jnp.arange(D)[:, None] // HEAD_DIM == jnp.arange(H)[None, :]).astype(BF16)
    row = pl.BlockSpec((tm, D), lambda i: (i, 0))
    ins = [ng, w_main, w_f, b_f, q_gain, k_gain, sel, sel.T]
    return pl.pallas_call(
        _fox_pre_kernel,
        grid=(T // tm,),
        in_specs=[row] + [_full(a.shape) for a in ins],
        out_specs=[row] * 4 + [pl.BlockSpec((tm, H), lambda i: (i, 0))] * 4,
        out_shape=([jax.ShapeDtypeStruct((T, D), BF16)] * 4 + [jax.ShapeDtypeStruct((T, H), F32)]
                   + [jax.ShapeDtypeStruct((T, H), BF16)] * 3),
        scratch_shapes=[pltpu.VMEM((1, H), F32)],
        compiler_params=_cparams("arbitrary"),
        name="fox_pre",
    )(h, *ins)


def _fox_attn_kernel(jlo_ref, q_ref, k_ref, v_ref, og_ref, o_ref, *, blk):
    p_idx = pl.program_id(0)
    i = pl.program_id(1)
    lane = lax.broadcasted_iota(jnp.int32, (1, PAIR), 1)
    head0 = lane < HEAD_DIM
    q_heads = [q_ref[:, 0:PAIR], q_ref[:, PAIR:2 * PAIR]]
    causal = (lax.broadcasted_iota(jnp.int32, (blk, blk), 1)
              <= lax.broadcasted_iota(jnp.int32, (blk, blk), 0))

    def step(j, carry, masked):
        start = pl.multiple_of(j * blk, blk)
        vb = v_ref[pl.ds(start, blk), :]
        out = []
        for h in range(2):
            m_prev, l_prev, acc_prev = carry[h]
            kb = k_ref[pl.ds(start, blk), h * PAIR:(h + 1) * PAIR]
            s = lax.dot_general(q_heads[h], kb, (((1,), (1,)), ((), ())), preferred_element_type=F32)
            if masked:
                s = jnp.where(causal, s, -jnp.inf)
            m_new = jnp.maximum(m_prev, jnp.max(s, axis=-1, keepdims=True))
            alpha = jnp.exp2(m_prev - m_new)
            p = jnp.exp2(s - m_new)
            l_new = alpha * l_prev + jnp.sum(p, axis=-1, keepdims=True)
            acc_new = alpha * acc_prev + jnp.dot(p.astype(BF16), vb, preferred_element_type=F32)
            out.append((m_new, l_new, acc_new))
        return tuple(out)

    init = tuple((jnp.full((blk, 1), -jnp.inf, F32), jnp.zeros((blk, 1), F32), jnp.zeros((blk, PAIR), F32))
                 for _ in range(2))
    jlo = jlo_ref[p_idx * pl.num_programs(1) + i]
    res = lax.fori_loop(jlo, i, lambda j, c: step(j, c, False), init)
    res = step(i, res, True)
    o0 = res[0][2] / res[0][1]
    o1 = res[1][2] / res[1][1]
    o = jnp.where(head0, o0, o1)
    o_ref[...] = (o * og_ref[...].astype(F32)).astype(o_ref.dtype)


def _fox_attn(jlo, q_aug, k_aug, v, og, blk):
    T, D = v.shape
    n_pairs = D // PAIR
    qspec = pl.BlockSpec((blk, 2 * PAIR), lambda p, i, jlo: (i, p))
    ospec = pl.BlockSpec((blk, PAIR), lambda p, i, jlo: (i, p))
    grid_spec = pltpu.PrefetchScalarGridSpec(
        num_scalar_prefetch=1,
        grid=(n_pairs, T // blk),
        in_specs=[qspec,
                  pl.BlockSpec((T, 2 * PAIR), lambda p, i, jlo: (0, p)),
                  pl.BlockSpec((T, PAIR), lambda p, i, jlo: (0, p)),
                  ospec],
        out_specs=ospec,
    )
    return pl.pallas_call(
        functools.partial(_fox_attn_kernel, blk=blk),
        grid_spec=grid_spec,
        out_shape=jax.ShapeDtypeStruct((T, D), BF16),
        compiler_params=_cparams("parallel", "arbitrary"),
        name="fox_attn",
    )(jlo, q_aug, k_aug, v, og)


def _augment(x, cols):
    T, H, n = cols.shape
    extra = jnp.pad(cols, ((0, 0), (0, 0), (0, HEAD_DIM - n)))
    return jnp.concatenate([x.reshape(T, H, HEAD_DIM), extra], axis=-1).reshape(T, H * PAIR)


def _first_live_block(c2, q_gain, k_gain, blk):
    T, H = c2.shape
    bound = 8.0 * LOG2E * 1.02 * jnp.max(jnp.abs(q_gain)) * jnp.max(jnp.abs(k_gain))
    c_first = c2[0::blk]
    c_last = c2[blk - 1::blk]
    dead = c_last[None, :, :] > c_first[:, None, :] + (2.0 * bound + 104.0 * LOG2E + 1.0)
    jlo = jnp.sum(dead, axis=1).astype(jnp.int32)
    jlo = jnp.min(jlo.reshape(-1, H // 2, 2), axis=-1)
    return jlo.T.reshape(-1)


def _router_kernel(h_ref, ng_ref, wr_ref, u_ref, info_ref, cnt_ref, carry_ref):
    i = pl.program_id(0)
    tm = h_ref.shape[0]
    E = N_EXPERTS

    @pl.when(i == 0)
    def _():
        carry_ref[...] = jnp.zeros_like(carry_ref)

    u = _rms(h_ref[...], ng_ref[...])
    u_ref[...] = u
    u_hi, u_lo = _split_hi_lo(u)
    w_hi, w_lo = _split_hi_lo(wr_ref[...])
    logits = (jnp.dot(u_hi, w_hi, preferred_element_type=F32)
              + jnp.dot(u_hi, w_lo, preferred_element_type=F32)
              + jnp.dot(u_lo, w_hi, preferred_element_type=F32))
    lane = lax.broadcasted_iota(jnp.int32, (tm, E), 1)
    v1 = jnp.max(logits, axis=-1, keepdims=True)
    e1 = jnp.min(jnp.where(logits == v1, lane, E), axis=-1, keepdims=True)
    rest = jnp.where(lane == e1, -jnp.inf, logits)
    v2 = jnp.max(rest, axis=-1, keepdims=True)
    e2 = jnp.min(jnp.where(rest == v2, lane, E), axis=-1, keepdims=True)
    g1 = 1.0 / (1.0 + jnp.exp(v2 - v1))
    g2 = 1.0 - g1
    oh1 = lane == e1
    oh2 = lane == e2
    onehot = jnp.where(oh1 | oh2, 1.0, 0.0).astype(BF16)
    row = lax.broadcasted_iota(jnp.int32, (tm, tm), 0)
    col = lax.broadcasted_iota(jnp.int32, (tm, tm), 1)
    tri_strict = jnp.where(col < row, 1.0, 0.0).astype(BF16)
    before = jnp.dot(tri_strict, onehot, preferred_element_type=F32) + carry_ref[...]
    rank1 = jnp.sum(jnp.where(oh1, before, 0.0), axis=-1, keepdims=True)
    rank2 = jnp.sum(jnp.where(oh2, before, 0.0), axis=-1, keepdims=True)
    info = jnp.where(lane == 0, e1.astype(F32), 0.0)
    info = jnp.where(lane == 1, e2.astype(F32), info)
    info = jnp.where(lane == 2, g1, info)
    info = jnp.where(lane == 3, g2, info)
    info = jnp.where(lane == 4, rank1, info)
    info = jnp.where(lane == 5, rank2, info)
    info_ref[...] = info
    total = carry_ref[...] + jnp.sum(onehot.astype(F32), axis=0, keepdims=True)
    carry_ref[...] = total
    cnt_ref[...] = total


def _router(h, ng, w_router, tm=512):
    T, D = h.shape
    E = N_EXPERTS
    row = pl.BlockSpec((tm, D), lambda i: (i, 0))
    return pl.pallas_call(
        _router_kernel,
        grid=(T // tm,),
        in_specs=[row, _full((1, D)), _full((D, E))],
        out_specs=[row, pl.BlockSpec((tm, E), lambda i: (i, 0)), _full((1, E))],
        out_shape=[jax.ShapeDtypeStruct((T, D), F32), jax.ShapeDtypeStruct((T, E), F32),
                   jax.ShapeDtypeStruct((1, E), F32)],
        scratch_shapes=[pltpu.VMEM((1, E), F32)],
        compiler_params=_cparams("arbitrary"),
        name="router",
    )(h, ng, w_router)


def _gather_kernel(idx_ref, src_ref, o_ref, sem):
    b = pl.program_id(0)
    rows = o_ref.shape[0]

    def row_copy(r):
        tok = idx_ref[b * rows + r]
        return pltpu.make_async_copy(src_ref.at[pl.ds(tok, 1)], o_ref.at[pl.ds(r, 1)], sem)

    def start(r, _):
        row_copy(r).start()
        return 0

    def wait(r, _):
        row_copy(r).wait()
        return 0

    lax.fori_loop(0, rows, start, 0)
    lax.fori_loop(0, rows, wait, 0)


def _gather_rows(idx, src, rows=MOE_ROWS):
    R = idx.shape[0]
    D = src.shape[1]
    grid_spec = pltpu.PrefetchScalarGridSpec(
        num_scalar_prefetch=1,
        grid=(R // rows,),
        in_specs=[pl.BlockSpec(memory_space=pl.ANY)],
        out_specs=pl.BlockSpec((rows, D), lambda b, idx: (b, 0)),
        scratch_shapes=[pltpu.SemaphoreType.DMA(())],
    )
    return pl.pallas_call(
        _gather_kernel,
        grid_spec=grid_spec,
        out_shape=jax.ShapeDtypeStruct((R, D), src.dtype),
        compiler_params=_cparams("arbitrary"),
        name="moe_gather",
    )(idx, src)


def _combine_kernel(pos_ref, h_ref, fg_ref, y_ref, o_ref, ya_ref, yb_ref, sem):
    b = pl.program_id(0)
    rows = h_ref.shape[0]

    def copies(r):
        base = 2 * (b * rows + r)
        return (pltpu.make_async_copy(y_ref.at[pl.ds(pos_ref[base], 1)], ya_ref.at[pl.ds(r, 1)], sem),
                pltpu.make_async_copy(y_ref.at[pl.ds(pos_ref[base + 1], 1)], yb_ref.at[pl.ds(r, 1)], sem))

    def start(r, _):
        ca, cb = copies(r)
        ca.start()
        cb.start()
        return 0

    def wait(r, _):
        ca, cb = copies(r)
        ca.wait()
        cb.wait()
        return 0

    lax.fori_loop(0, rows, start, 0)
    lax.fori_loop(0, rows, wait, 0)
    o_ref[...] = _rms(h_ref[...] + (ya_ref[...] + yb_ref[...]), fg_ref[...])


def _combine(pos, h, final_g, y, rows=256):
    T, D = h.shape
    row = pl.BlockSpec((rows, D), lambda b, pos: (b, 0))
    grid_spec = pltpu.PrefetchScalarGridSpec(
        num_scalar_prefetch=1,
        grid=(T // rows,),
        in_specs=[row, pl.BlockSpec((1, D), lambda b, pos: (0, 0)), pl.BlockSpec(memory_space=pl.ANY)],
        out_specs=row,
        scratch_shapes=[pltpu.VMEM((rows, D), F32), pltpu.VMEM((rows, D), F32),
                        pltpu.SemaphoreType.DMA(())],
    )
    return pl.pallas_call(
        _combine_kernel,
        grid_spec=grid_spec,
        out_shape=jax.ShapeDtypeStruct((T, D), F32),
        compiler_params=_cparams("arbitrary"),
        name="moe_combine",
    )(pos, h, final_g, y)


def _rwkv_layer(h, ng, mu, w_rkv, w0, w1, w2, a0, a1, a2, g1, g2, k_k, k_a, r_k, gn_g, gn_b, w_o):
    D = h.shape[1]
    vec = lambda z: z.reshape(1, D)
    b16 = lambda z: z.astype(BF16)
    r, k, v, lw, a, g = _rwkv_pre(h, vec(ng), mu, b16(w_rkv), vec(w0), b16(w1), b16(w2), vec(a0),
                                  b16(a1), b16(a2), b16(g1), b16(g2))
    o = _rwkv_scan(r, k, v, lw, a, g, vec(k_k), vec(k_a), vec(r_k), vec(gn_g), vec(gn_b))
    return _mm_res(o, b16(w_o), h)


def _fox_layer(h, ng, w_in, b_f, q_gain, k_gain, w_o):
    T, D = h.shape
    H = D // HEAD_DIM
    q, k, v, og, c2, hi, lo, lo2 = _fox_pre(
        h, ng.reshape(1, D), w_in[:, :4 * D].astype(BF16), w_in[:, 4 * D:].astype(BF16),
        b_f.reshape(1, H), jnp.tile(q_gain, H).reshape(1, D), jnp.tile(k_gain, H).reshape(1, D))
    one = jnp.ones_like(hi)
    q_aug = _augment(q, jnp.stack([hi, lo, lo2, one, one, one], axis=-1))
    k_aug = _augment(k, jnp.stack([one, one, one, -hi, -lo, -lo2], axis=-1))
    blk = min(ATTN_BLOCK, T)
    jlo = _first_live_block(c2, q_gain, k_gain, blk)
    o = _fox_attn(jlo, q_aug, k_aug, v, og, blk)
    return _mm_res(o, w_o.astype(BF16), h)


def _moe_layer(h, ng, final_g, w_router, w_gu, w_down):
    T, D = h.shape
    E, RB = N_EXPERTS, MOE_ROWS
    u, info, cnt = _router(h, ng.reshape(1, D), w_router)
    e = info[:, 0:2].astype(jnp.int32)
    gate = info[:, 2:4]
    rank = info[:, 4:6].astype(jnp.int32)
    counts = cnt[0].astype(jnp.int32)
    padded = (counts + RB - 1) // RB * RB
    ends = jnp.cumsum(padded)
    starts = ends - padded
    n_blocks = (2 * T) // RB + E
    dest = (starts[e] + rank).reshape(-1)
    tok = jnp.repeat(jnp.arange(T, dtype=jnp.int32), 2)
    buf_tok = jnp.zeros((n_blocks * RB,), jnp.int32).at[dest].set(tok)
    buf_w = jnp.zeros((n_blocks * RB,), F32).at[dest].set(gate.reshape(-1))
    block_e = jnp.minimum(jnp.sum(jnp.arange(n_blocks)[:, None] * RB >= ends[None, :], axis=1),
                          E - 1).astype(jnp.int32)
    n_live = (ends[-1] // RB).astype(jnp.int32).reshape(1)
    xg = _gather_rows(buf_tok, u)
    y = _moe_ffn(block_e, n_live, xg, buf_w.reshape(-1, 1), w_gu.astype(BF16), w_down.astype(BF16))
    return _combine(dest, h, final_g.reshape(1, D), y)


def kernel(x, norm_g, final_g, rwkv_mu, rwkv_w_rkv, rwkv_w0, rwkv_w1, rwkv_w2, rwkv_a0, rwkv_a1,
           rwkv_a2, rwkv_g1, rwkv_g2, rwkv_k_k, rwkv_k_a, rwkv_r_k, rwkv_gn_g, rwkv_gn_b, rwkv_w_o,
           fox_w_in, fox_b_f, fox_q_gain, fox_k_gain, fox_w_o, ffn_w_gu, ffn_w_down,
           moe_w_router, moe_w_gu, moe_w_down):
    B, T, D = x.shape
    h = x.reshape(B * T, D)
    h = _rwkv_layer(h, norm_g[0, 0], rwkv_mu[0], rwkv_w_rkv[0], rwkv_w0[0], rwkv_w1[0], rwkv_w2[0],
                    rwkv_a0[0], rwkv_a1[0], rwkv_a2[0], rwkv_g1[0], rwkv_g2[0], rwkv_k_k[0],
                    rwkv_k_a[0], rwkv_r_k[0], rwkv_gn_g[0], rwkv_gn_b[0], rwkv_w_o[0])
    h = _ffn(h, norm_g[0, 1].reshape(1, D), ffn_w_gu[0].astype(BF16), ffn_w_down[0].astype(BF16))
    h = _fox_layer(h, norm_g[1, 0], fox_w_in[0], fox_b_f[0], fox_q_gain[0], fox_k_gain[0], fox_w_o[0])
    out = _moe_layer(h, norm_g[1, 1], final_g, moe_w_router[0], moe_w_gu[0], moe_w_down[0])
    return out.reshape(B, T, D)
```

```python
import functools

import jax
import jax.numpy as jnp
from jax import lax
from jax.experimental import pallas as pl
from jax.experimental.pallas import tpu as pltpu

F32 = jnp.float32
BF16 = jnp.bfloat16

HEAD_DIM = 64
PAIR = 2 * HEAD_DIM
GN_EPS = 64e-5
RMS_EPS = 1e-6
N_EXPERTS = 8
SCAN_CHUNK = 64
MOE_ROWS = 512
MOE_CHUNK = 512
ATTN_BLOCK = 512
LOG2E = 1.4426950408889634
VMEM_LIMIT = 56 * 1024 * 1024


def _cparams(*sem):
    return pltpu.CompilerParams(dimension_semantics=sem, vmem_limit_bytes=VMEM_LIMIT)


def _dot(a, b):
    return jnp.dot(a.astype(BF16), b.astype(BF16), preferred_element_type=F32)


def _dot_nt(a, b):
    return lax.dot_general(a.astype(BF16), b.astype(BF16), (((1,), (1,)), ((), ())),
                           preferred_element_type=F32)


def _dot_tn(a, b):
    return lax.dot_general(a.astype(BF16), b.astype(BF16), (((0,), (0,)), ((), ())),
                           preferred_element_type=F32)


def _split_hi_lo(x):
    hi = x.astype(BF16)
    lo = (x - hi.astype(F32)).astype(BF16)
    return hi, lo


def _rms(x, g):
    ms = jnp.mean(x * x, axis=-1, keepdims=True)
    return x * lax.rsqrt(ms + RMS_EPS) * g


def _sigmoid(x):
    return 1.0 / (1.0 + jnp.exp(-x))


def _full(shape):
    return pl.BlockSpec(shape, lambda *_: (0,) * len(shape))


def _rwkv_pre_kernel(h_ref, hp_ref, ng_ref, mu_ref, wrkv_ref, w0_ref, w1_ref, w2_ref, a0_ref,
                     a1_ref, a2_ref, g1_ref, g2_ref, r_ref, k_ref, v_ref, lw_ref, a_ref, g_ref):
    i = pl.program_id(0)
    ng = ng_ref[...]
    u = _rms(h_ref[...], ng)
    up = _rms(hp_ref[7:8, :], ng)
    up = jnp.where(i == 0, 0.0, up)
    row = lax.broadcasted_iota(jnp.int32, u.shape, 0)
    u_prev = jnp.where(row == 0, up, pltpu.roll(u, 1, axis=0))
    dx = u_prev - u
    mix = lambda n: u + dx * mu_ref[n:n + 1, :]
    r_ref[...] = _dot(mix(0), wrkv_ref[0])
    k_ref[...] = _dot(mix(1), wrkv_ref[1])
    v_ref[...] = _dot(mix(2), wrkv_ref[2])
    z = w0_ref[...] + _dot(jnp.tanh(_dot(mix(3), w1_ref[...])), w2_ref[...])
    softplus_neg = jnp.maximum(-z, 0.0) + jnp.log(1.0 + jnp.exp(-jnp.abs(z)))
    lw_ref[...] = -jnp.exp(-softplus_neg - 0.5)
    a_ref[...] = _sigmoid(a0_ref[...] + _dot(_dot(mix(4), a1_ref[...]), a2_ref[...]))
    g_ref[...] = _dot(_sigmoid(_dot(mix(5), g1_ref[...])), g2_ref[...])


def _rwkv_pre(h, ng, mu, wrkv, w0, w1, w2, a0, a1, a2, g1, g2, tm=256):
    T, D = h.shape
    row = pl.BlockSpec((tm, D), lambda i: (i, 0))
    prev = pl.BlockSpec((8, D), lambda i: (jnp.maximum(i * (tm // 8) - 1, 0), 0))
    ins = [ng, mu, wrkv, w0, w1, w2, a0, a1, a2, g1, g2]
    return pl.pallas_call(
        _rwkv_pre_kernel,
        grid=(T // tm,),
        in_specs=[row, prev] + [_full(a.shape) for a in ins],
        out_specs=[row] * 6,
        out_shape=[jax.ShapeDtypeStruct((T, D), F32)] * 6,
        compiler_params=_cparams("parallel"),
        name="rwkv_pre",
    )(h, h, *ins)


def _scan_kernel(r_ref, k_ref, v_ref, lw_ref, a_ref, g_ref, kk_ref, ka_ref, rk_ref, gng_ref,
                 gnb_ref, o_ref, s_ref):
    L = SCAN_CHUNK
    n_pairs = r_ref.shape[1] // PAIR

    @pl.when(pl.program_id(0) == 0)
    def _():
        s_ref[...] = jnp.zeros_like(s_ref)

    row = lax.broadcasted_iota(jnp.int32, (L, L), 0)
    col = lax.broadcasted_iota(jnp.int32, (L, L), 1)
    tri_incl = jnp.where(col <= row, 1.0, 0.0).astype(BF16)
    eye = jnp.where(col == row, 1.0, 0.0).astype(F32)
    lane = lax.broadcasted_iota(jnp.int32, (1, PAIR), 1)
    head0 = lane < HEAD_DIM
    arow = lax.broadcasted_iota(jnp.int32, (4 * L, PAIR), 0)
    acol = lax.broadcasted_iota(jnp.int32, (4 * L, PAIR), 1)
    t_idx = arow % L + jnp.where(arow < 2 * L, 0, 1)
    j_idx = acol % L
    a_mask = j_idx < t_idx
    brow = lax.broadcasted_iota(jnp.int32, (PAIR, PAIR), 0)
    bcol = lax.broadcasted_iota(jnp.int32, (PAIR, PAIR), 1)
    blockdiag = (brow < HEAD_DIM) == (bcol < HEAD_DIM)

    def head_sum(x):
        s0 = jnp.sum(jnp.where(head0, x, 0.0), axis=-1, keepdims=True)
        s1 = jnp.sum(jnp.where(head0, 0.0, x), axis=-1, keepdims=True)
        return jnp.where(head0, s0, s1)

    pairs = range(n_pairs)
    heads = [(p, h) for p in pairs for h in range(2)]
    sls = [slice(p * PAIR, (p + 1) * PAIR) for p in pairs]

    lws = [lw_ref[:, sl] for sl in sls]
    splits = [_split_hi_lo(lw) for lw in lws]
    css = [jnp.dot(tri_incl, hi, preferred_element_type=F32) + jnp.dot(tri_incl, lo, preferred_element_type=F32)
           for hi, lo in splits]
    prep = []
    for p in pairs:
        sl, cs, lw = sls[p], css[p], lws[p]
        w_inv = jnp.exp(-cs)
        w_last = jnp.exp(cs[L - 1:L, :])
        r = r_ref[:, sl]
        k = k_ref[:, sl]
        a = a_ref[:, sl]
        kk = k * kk_ref[:, sl]
        kk = kk / jnp.maximum(jnp.sqrt(head_sum(kk * kk)), 1e-12)
        kmod = k * (1.0 + (a - 1.0) * ka_ref[:, sl])
        at = -kk * jnp.exp(cs - lw)
        bt = kk * a * w_inv
        kt = kmod * w_inv
        rt = r * jnp.exp(cs)
        prep.append(dict(at=at, rt=rt, bk=jnp.concatenate([bt, kt], axis=0).astype(BF16),
                         bkhat=jnp.concatenate([bt * w_last, kt * w_last], axis=0).astype(BF16),
                         w_last=w_last, v=v_ref[:, sl], rk=r * kmod))
    aas = []
    for p in pairs:
        at, rt = prep[p]["at"], prep[p]["rt"]
        x_stack = jnp.concatenate([jnp.where(head0, at, 0.0), jnp.where(head0, 0.0, at),
                                   jnp.where(head0, rt, 0.0), jnp.where(head0, 0.0, rt)], axis=0)
        aas.append(jnp.where(a_mask, _dot_nt(x_stack, prep[p]["bk"]), 0.0))
    states = [s_ref[p] for p in pairs]
    inter = [_dot_nt(jnp.concatenate([prep[p]["at"], prep[p]["rt"]], axis=0), states[p]) for p in pairs]
    v_bf = [prep[p]["v"].astype(BF16) for p in pairs]
    vv = [jnp.concatenate([v_bf[p], v_bf[p]], axis=0) for p in pairs]
    rhs = {}
    pw = {}
    t_inv = {}
    for p, h in heads:
        a_rows = aas[p][h * L:(h + 1) * L, :]
        rhs[p, h] = inter[p][:L, :] + _dot(jnp.where(head0, 0.0, a_rows), vv[p])
        pw[p, h] = a_rows[:, :L]
        t_inv[p, h] = eye + pw[p, h]
    for _ in range(5):
        for ph in heads:
            pw[ph] = _dot(pw[ph], pw[ph])
        for ph in heads:
            t_inv[ph] = t_inv[ph] + _dot(t_inv[ph], pw[ph])
    u_h = {ph: _dot(t_inv[ph], rhs[ph]) for ph in heads}
    y_h = {}
    for p, h in heads:
        ar = aas[p][(2 + h) * L:(3 + h) * L, :]
        y_h[p, h] = _dot(ar, jnp.concatenate([u_h[p, h].astype(BF16), v_bf[p]], axis=0))
    for p in pairs:
        sl = sls[p]
        v = prep[p]["v"]
        u = jnp.where(head0, u_h[p, 0], u_h[p, 1])
        uv = jnp.concatenate([u.astype(BF16), v_bf[p]], axis=0)
        s_ref[p] = states[p] * prep[p]["w_last"] + jnp.where(blockdiag, _dot_tn(uv, prep[p]["bkhat"]), 0.0)
        y = inter[p][L:, :] + jnp.where(head0, y_h[p, 0], y_h[p, 1])
        mean = head_sum(y) * (1.0 / HEAD_DIM)
        d = y - mean
        var = head_sum(d * d) * (1.0 / HEAD_DIM)
        yn = d * lax.rsqrt(var + GN_EPS) * gng_ref[:, sl] + gnb_ref[:, sl]
        bonus = head_sum(prep[p]["rk"] * rk_ref[:, sl]) * v
        o_ref[:, sl] = ((yn + bonus) * g_ref[:, sl]).astype(o_ref.dtype)


def _rwkv_scan(r, k, v, lw, a, g, k_k, k_a, r_k, gn_g, gn_b):
    T, D = r.shape
    L = SCAN_CHUNK
    row = pl.BlockSpec((L, D), lambda c: (c, 0))
    par = _full((1, D))
    return pl.pallas_call(
        _scan_kernel,
        grid=(T // L,),
        in_specs=[row] * 6 + [par] * 5,
        out_specs=row,
        out_shape=jax.ShapeDtypeStruct((T, D), BF16),
        scratch_shapes=[pltpu.VMEM((D // PAIR, PAIR, PAIR), F32)],
        compiler_params=_cparams("arbitrary"),
        name="rwkv_scan",
    )(r, k, v, lw, a, g, k_k, k_a, r_k, gn_g, gn_b)


def _mm_res_kernel(x_ref, w_ref, res_ref, o_ref):
    o_ref[...] = res_ref[...] + _dot(x_ref[...], w_ref[...])


def _mm_res(x, w, res, tm=512):
    T, K = x.shape
    N = w.shape[1]
    return pl.pallas_call(
        _mm_res_kernel,
        grid=(T // tm,),
        in_specs=[pl.BlockSpec((tm, K), lambda i: (i, 0)), _full((K, N)),
                  pl.BlockSpec((tm, N), lambda i: (i, 0))],
        out_specs=pl.BlockSpec((tm, N), lambda i: (i, 0)),
        out_shape=jax.ShapeDtypeStruct((T, N), F32),
        compiler_params=_cparams("parallel"),
        name="mm_res",
    )(x, w, res)


def _swiglu_step(u_ref, wg_ref, wu_ref, wd_ref, acc_ref):
    u = u_ref[...]
    gate = jnp.dot(u, wg_ref[...], preferred_element_type=F32)
    up = jnp.dot(u, wu_ref[...], preferred_element_type=F32)
    hid = (gate * _sigmoid(gate) * up).astype(BF16)
    acc_ref[...] += jnp.dot(hid, wd_ref[...], preferred_element_type=F32)


def _ffn_kernel(h_ref, ng_ref, wg_ref, wu_ref, wd_ref, o_ref, u_ref, acc_ref):
    f = pl.program_id(1)

    @pl.when(f == 0)
    def _():
        u_ref[...] = _rms(h_ref[...], ng_ref[...]).astype(BF16)
        acc_ref[...] = jnp.zeros_like(acc_ref)

    _swiglu_step(u_ref, wg_ref, wu_ref, wd_ref, acc_ref)

    @pl.when(f == pl.num_programs(1) - 1)
    def _():
        o_ref[...] = h_ref[...] + acc_ref[...]


def _ffn(h, ng, w_gu, w_down, tm=1024, tf=512):
    T, D = h.shape
    F = w_down.shape[0]
    nf = F // tf
    row = pl.BlockSpec((tm, D), lambda i, f: (i, 0))
    return pl.pallas_call(
        _ffn_kernel,
        grid=(T // tm, nf),
        in_specs=[row, _full((1, D)),
                  pl.BlockSpec((D, tf), lambda i, f: (0, f)),
                  pl.BlockSpec((D, tf), lambda i, f: (0, f + nf)),
                  pl.BlockSpec((tf, D), lambda i, f: (f, 0))],
        out_specs=row,
        out_shape=jax.ShapeDtypeStruct((T, D), F32),
        scratch_shapes=[pltpu.VMEM((tm, D), BF16), pltpu.VMEM((tm, D), F32)],
        compiler_params=_cparams("parallel", "arbitrary"),
        name="ffn",
    )(h, ng, w_gu, w_gu, w_down)


def _moe_ffn_kernel(be_ref, nb_ref, clo_ref, chi_ref, dest_ref, tok_hbm, wg_ref, wu_ref, wd_ref, o_ref,
                    tok_ref, x_ref, acc_ref, sem):
    b = pl.program_id(0)
    f = pl.program_id(1)
    live = b < nb_ref[0]
    rows = o_ref.shape[0]
    ch = MOE_CHUNK

    @pl.when((b == 0) & (f == 0))
    def _():
        cp = pltpu.make_async_copy(tok_hbm, tok_ref, sem)
        cp.start()
        cp.wait()

    @pl.when(f == 0)
    def _():
        acc_ref[...] = jnp.zeros_like(acc_ref)

    @pl.when((f == 0) & live)
    def _():
        row_id = b * rows + lax.broadcasted_iota(jnp.int32, (rows, 1), 0)

        def chunk(c, _):
            start = pl.multiple_of(c * ch, ch)
            d = dest_ref[:, pl.ds(start, ch)]
            onehot = (jnp.where(row_id == d[0:1, :], 1.0, 0.0)
                      + jnp.where(row_id == d[1:2, :], 1.0, 0.0)).astype(BF16)
            acc_ref[...] += jnp.dot(onehot, tok_ref[pl.ds(start, ch), :], preferred_element_type=F32)
            return 0

        lax.fori_loop(clo_ref[b], chi_ref[b] + 1, chunk, 0)
        x_ref[...] = acc_ref[...].astype(BF16)
        acc_ref[...] = jnp.zeros_like(acc_ref)

    @pl.when(live)
    def _():
        _swiglu_step(x_ref, wg_ref.at[0], wu_ref.at[0], wd_ref.at[0], acc_ref)

    @pl.when(f == pl.num_programs(1) - 1)
    def _():
        o_ref[...] = acc_ref[...]


def _moe_ffn(block_e, n_live, c_lo, c_hi, dest_t, tok, w_gu, w_down, n_blocks, tf=512):
    T, D = tok.shape
    F = w_down.shape[1]
    nf = F // tf
    tm = MOE_ROWS
    grid_spec = pltpu.PrefetchScalarGridSpec(
        num_scalar_prefetch=4,
        grid=(n_blocks, nf),
        in_specs=[pl.BlockSpec((2, T), lambda b, f, *_: (0, 0)),
                  pl.BlockSpec(memory_space=pl.ANY),
                  pl.BlockSpec((1, D, tf), lambda b, f, be, *_: (be[b], 0, f)),
                  pl.BlockSpec((1, D, tf), lambda b, f, be, *_: (be[b], 0, f + nf)),
                  pl.BlockSpec((1, tf, D), lambda b, f, be, *_: (be[b], f, 0))],
        out_specs=pl.BlockSpec((tm, D), lambda b, f, *_: (b, 0)),
        scratch_shapes=[pltpu.VMEM((T, D), BF16), pltpu.VMEM((tm, D), BF16), pltpu.VMEM((tm, D), F32),
                        pltpu.SemaphoreType.DMA(())],
    )
    return pl.pallas_call(
        _moe_ffn_kernel,
        grid_spec=grid_spec,
        out_shape=jax.ShapeDtypeStruct((n_blocks * tm, D), F32),
        compiler_params=_cparams("arbitrary", "arbitrary"),
        name="moe_ffn",
    )(block_e, n_live, c_lo, c_hi, dest_t, tok, w_gu, w_gu, w_down)


def _fox_pre_kernel(h_ref, ng_ref, w_ref, wf_ref, bf_ref, qg_ref, kg_ref, sel_ref, selt_ref,
                    q_ref, k_ref, v_ref, og_ref, c2_ref, chi_ref, clo_ref, clo2_ref, carry_ref):
    i = pl.program_id(0)
    tm, D = h_ref.shape

    @pl.when(i == 0)
    def _():
        carry_ref[...] = jnp.zeros_like(carry_ref)

    u = _rms(h_ref[...], ng_ref[...]).astype(BF16)

    def head_norm(x, gain):
        ms = _dot(x * x, sel_ref[...]) * (1.0 / HEAD_DIM)
        inv_hi, inv_lo = _split_hi_lo(lax.rsqrt(ms + RMS_EPS))
        inv = (jnp.dot(inv_hi, selt_ref[...], preferred_element_type=F32)
               + jnp.dot(inv_lo, selt_ref[...], preferred_element_type=F32))
        return x * inv * gain

    q = jnp.dot(u, w_ref[:, 0:D], preferred_element_type=F32)
    q_ref[...] = (head_norm(q, qg_ref[...]) * (HEAD_DIM ** -0.5 * LOG2E)).astype(BF16)
    k = jnp.dot(u, w_ref[:, D:2 * D], preferred_element_type=F32)
    k_ref[...] = head_norm(k, kg_ref[...]).astype(BF16)
    v_ref[...] = jnp.dot(u, w_ref[:, 2 * D:3 * D], preferred_element_type=F32).astype(BF16)
    og_ref[...] = _sigmoid(jnp.dot(u, w_ref[:, 3 * D:4 * D], preferred_element_type=F32)).astype(BF16)

    fl = jnp.dot(u, wf_ref[...], preferred_element_type=F32) + bf_ref[...]
    log_f = jnp.minimum(fl, 0.0) - jnp.log(1.0 + jnp.exp(-jnp.abs(fl)))
    row = lax.broadcasted_iota(jnp.int32, (tm, tm), 0)
    col = lax.broadcasted_iota(jnp.int32, (tm, tm), 1)
    tri = jnp.where(col <= row, 1.0, 0.0).astype(BF16)
    hi, lo = _split_hi_lo(log_f)
    lo2 = (log_f - hi.astype(F32) - lo.astype(F32)).astype(BF16)
    c = (jnp.dot(tri, hi, preferred_element_type=F32) + jnp.dot(tri, lo, preferred_element_type=F32)
         + jnp.dot(tri, lo2, preferred_element_type=F32)) + carry_ref[...]
    carry_ref[...] = c[tm - 1:tm, :]
    c2 = c * LOG2E
    c2_ref[...] = c2
    c_hi, c_lo = _split_hi_lo(c2)
    chi_ref[...] = c_hi
    clo_ref[...] = c_lo
    clo2_ref[...] = (c2 - c_hi.astype(F32) - c_lo.astype(F32)).astype(BF16)


def _fox_pre(h, ng, w_main, w_f, b_f, q_gain, k_gain, tm=256):
    T, D = h.shape
    H = D // HEAD_DIM
    sel = (jnp.arange(D)[:, None] // HEAD_DIM == jnp.arange(H)[None, :]).astype(BF16)
    row = pl.BlockSpec((tm, D), lambda i: (i, 0))
    ins = [ng, w_main, w_f, b_f, q_gain, k_gain, sel, sel.T]
    return pl.pallas_call(
        _fox_pre_kernel,
        grid=(T // tm,),
        in_specs=[row] + [_full(a.shape) for a in ins],
        out_specs=[row] * 4 + [pl.BlockSpec((tm, H), lambda i: (i, 0))] * 4,
        out_shape=([jax.ShapeDtypeStruct((T, D), BF16)] * 4 + [jax.ShapeDtypeStruct((T, H), F32)]
                   + [jax.ShapeDtypeStruct((T, H), BF16)] * 3),
        scratch_shapes=[pltpu.VMEM((1, H), F32)],
        compiler_params=_cparams("arbitrary"),
        name="fox_pre",
    )(h, *ins)


def _fox_attn_kernel(jlo_ref, q_ref, k_ref, vt_ref, og_ref, o_ref, s_ref, *, blk):
    p_idx = pl.program_id(0)
    i = pl.program_id(1)
    q_heads = [q_ref[:, 0:PAIR], q_ref[:, PAIR:2 * PAIR]]
    causal = (lax.broadcasted_iota(jnp.int32, (blk, blk), 0)
              <= lax.broadcasted_iota(jnp.int32, (blk, blk), 1))

    def scores_into(j, slot):
        start = pl.multiple_of(j * blk, blk)
        for h in range(2):
            s_ref[slot, h] = lax.dot_general(k_ref[pl.ds(start, blk), h * PAIR:(h + 1) * PAIR], q_heads[h],
                                             (((1,), (1,)), ((), ())), preferred_element_type=F32)

    def softmax_pv(j, slot, carry, masked):
        start = pl.multiple_of(j * blk, blk)
        out = []
        for h in range(2):
            m_prev, l_prev, acc_prev = carry[h]
            vtb = vt_ref[h * HEAD_DIM:(h + 1) * HEAD_DIM, pl.ds(start, blk)]
            s = s_ref[slot, h]
            if masked:
                s = jnp.where(causal, s, -jnp.inf)
            m_new = jnp.maximum(m_prev, jnp.max(s, axis=0, keepdims=True))
            alpha = jnp.exp2(m_prev - m_new)
            p = jnp.exp2(s - m_new)
            l_new = alpha * l_prev + jnp.sum(p, axis=0, keepdims=True)
            acc_new = alpha * acc_prev + jnp.dot(vtb, p.astype(BF16), preferred_element_type=F32)
            out.append((m_new, l_new, acc_new))
        return tuple(out)

    init = tuple((jnp.full((1, blk), -jnp.inf, F32), jnp.zeros((1, blk), F32), jnp.zeros((HEAD_DIM, blk), F32))
                 for _ in range(2))
    jlo = jlo_ref[p_idx * pl.num_programs(1) + i]
    total = i - jlo + 1
    scores_into(jlo, 0)

    def two_blocks(t, carry):
        j = jlo + 2 * t
        scores_into(j + 1, 1)
        carry = softmax_pv(j, 0, carry, False)
        scores_into(j + 2, 0)
        return softmax_pv(j + 1, 1, carry, False)

    res = lax.fori_loop(0, (total - 1) // 2, two_blocks, init)

    def last_one(carry):
        return softmax_pv(i, 0, carry, True)

    def last_two(carry):
        scores_into(i, 1)
        carry = softmax_pv(i - 1, 0, carry, False)
        return softmax_pv(i, 1, carry, True)

    res = lax.cond(total % 2 == 1, last_one, last_two, res)
    o_t = jnp.concatenate([res[0][2] / res[0][1], res[1][2] / res[1][1]], axis=0)
    o_ref[...] = (o_t.T * og_ref[...].astype(F32)).astype(o_ref.dtype)


def _fox_attn(jlo, q_aug, k_aug, v_t, og, blk):
    D, T = v_t.shape
    n_pairs = D // PAIR
    qspec = pl.BlockSpec((blk, 2 * PAIR), lambda p, i, jlo: (i, p))
    ospec = pl.BlockSpec((blk, PAIR), lambda p, i, jlo: (i, p))
    grid_spec = pltpu.PrefetchScalarGridSpec(
        num_scalar_prefetch=1,
        grid=(n_pairs, T // blk),
        in_specs=[qspec,
                  pl.BlockSpec((T, 2 * PAIR), lambda p, i, jlo: (0, p)),
                  pl.BlockSpec((PAIR, T), lambda p, i, jlo: (p, 0)),
                  ospec],
        out_specs=ospec,
        scratch_shapes=[pltpu.VMEM((2, 2, blk, blk), F32)],
    )
    return pl.pallas_call(
        functools.partial(_fox_attn_kernel, blk=blk),
        grid_spec=grid_spec,
        out_shape=jax.ShapeDtypeStruct((T, D), BF16),
        compiler_params=_cparams("parallel", "arbitrary"),
        name="fox_attn",
    )(jlo, q_aug, k_aug, v_t, og)


def _augment(x, cols):
    T, H, n = cols.shape
    extra = jnp.pad(cols, ((0, 0), (0, 0), (0, HEAD_DIM - n)))
    return jnp.concatenate([x.reshape(T, H, HEAD_DIM), extra], axis=-1).reshape(T, H * PAIR)


def _first_live_block(c2, q_gain, k_gain, blk):
    T, H = c2.shape
    bound = 8.0 * LOG2E * 1.02 * jnp.max(jnp.abs(q_gain)) * jnp.max(jnp.abs(k_gain))
    c_first = c2[0::blk]
    c_last = c2[blk - 1::blk]
    dead = c_last[None, :, :] > c_first[:, None, :] + (2.0 * bound + 104.0 * LOG2E + 1.0)
    jlo = jnp.sum(dead, axis=1).astype(jnp.int32)
    jlo = jnp.min(jlo.reshape(-1, H // 2, 2), axis=-1)
    return jlo.T.reshape(-1)


def _router_kernel(h_ref, ng_ref, wr_ref, u_ref, info_ref, cnt_ref, csr_ref, carry_ref):
    i = pl.program_id(0)
    tm = h_ref.shape[0]
    E = N_EXPERTS

    @pl.when(i == 0)
    def _():
        carry_ref[...] = jnp.zeros_like(carry_ref)

    csr_ref[0] = carry_ref[...]
    u = _rms(h_ref[...], ng_ref[...])
    u_ref[...] = u.astype(BF16)
    u_hi, u_lo = _split_hi_lo(u)
    w_hi, w_lo = _split_hi_lo(wr_ref[...])
    logits = (jnp.dot(u_hi, w_hi, preferred_element_type=F32)
              + jnp.dot(u_hi, w_lo, preferred_element_type=F32)
              + jnp.dot(u_lo, w_hi, preferred_element_type=F32))
    lane = lax.broadcasted_iota(jnp.int32, (tm, E), 1)
    v1 = jnp.max(logits, axis=-1, keepdims=True)
    e1 = jnp.min(jnp.where(logits == v1, lane, E), axis=-1, keepdims=True)
    rest = jnp.where(lane == e1, -jnp.inf, logits)
    v2 = jnp.max(rest, axis=-1, keepdims=True)
    e2 = jnp.min(jnp.where(rest == v2, lane, E), axis=-1, keepdims=True)
    g1 = 1.0 / (1.0 + jnp.exp(v2 - v1))
    g2 = 1.0 - g1
    oh1 = lane == e1
    oh2 = lane == e2
    onehot = jnp.where(oh1 | oh2, 1.0, 0.0).astype(BF16)
    row = lax.broadcasted_iota(jnp.int32, (tm, tm), 0)
    col = lax.broadcasted_iota(jnp.int32, (tm, tm), 1)
    tri_strict = jnp.where(col < row, 1.0, 0.0).astype(BF16)
    before = jnp.dot(tri_strict, onehot, preferred_element_type=F32) + carry_ref[...]
    rank1 = jnp.sum(jnp.where(oh1, before, 0.0), axis=-1, keepdims=True)
    rank2 = jnp.sum(jnp.where(oh2, before, 0.0), axis=-1, keepdims=True)
    info = jnp.where(lane == 0, e1.astype(F32), 0.0)
    info = jnp.where(lane == 1, e2.astype(F32), info)
    info = jnp.where(lane == 2, g1, info)
    info = jnp.where(lane == 3, g2, info)
    info = jnp.where(lane == 4, rank1, info)
    info = jnp.where(lane == 5, rank2, info)
    info_ref[...] = info
    total = carry_ref[...] + jnp.sum(onehot.astype(F32), axis=0, keepdims=True)
    carry_ref[...] = total
    cnt_ref[...] = total


def _router(h, ng, w_router):
    T, D = h.shape
    E = N_EXPERTS
    tm = min(MOE_CHUNK, T)
    row = pl.BlockSpec((tm, D), lambda i: (i, 0))
    return pl.pallas_call(
        _router_kernel,
        grid=(T // tm,),
        in_specs=[row, _full((1, D)), _full((D, E))],
        out_specs=[row, pl.BlockSpec((tm, E), lambda i: (i, 0)), _full((1, E)),
                   pl.BlockSpec((1, 1, E), lambda i: (i, 0, 0))],
        out_shape=[jax.ShapeDtypeStruct((T, D), BF16), jax.ShapeDtypeStruct((T, E), F32),
                   jax.ShapeDtypeStruct((1, E), F32), jax.ShapeDtypeStruct((T // tm, 1, E), F32)],
        scratch_shapes=[pltpu.VMEM((1, E), F32)],
        compiler_params=_cparams("arbitrary"),
        name="router",
    )(h, ng, w_router)


def _combine_kernel(pos_ref, h_ref, gate_ref, fg_ref, y_ref, o_ref, ya_ref, yb_ref, sem):
    b = pl.program_id(0)
    nb = pl.num_programs(0)
    rows = h_ref.shape[0]

    def copies(blk, r):
        slot = blk % 2
        base = 2 * (blk * rows + r)
        return (pltpu.make_async_copy(y_ref.at[pl.ds(pos_ref[base], 1)], ya_ref.at[slot, pl.ds(r, 1)],
                                      sem.at[slot]),
                pltpu.make_async_copy(y_ref.at[pl.ds(pos_ref[base + 1], 1)], yb_ref.at[slot, pl.ds(r, 1)],
                                      sem.at[slot]))

    def issue(blk):
        def body(r, _):
            ca, cb = copies(blk, r)
            ca.start()
            cb.start()
            return 0
        lax.fori_loop(0, rows, body, 0, unroll=8)

    @pl.when(b == 0)
    def _():
        issue(0)

    @pl.when(b + 1 < nb)
    def _():
        issue(b + 1)

    def wait(r, _):
        ca, cb = copies(b, r)
        ca.wait()
        cb.wait()
        return 0

    lax.fori_loop(0, rows, wait, 0, unroll=8)
    slot = b % 2
    gate = gate_ref[...]
    moe = gate[:, 0:1] * ya_ref[slot] + gate[:, 1:2] * yb_ref[slot]
    o_ref[...] = _rms(h_ref[...] + moe, fg_ref[...])


def _combine(pos, h, gate, final_g, y, rows=256):
    T, D = h.shape
    row = pl.BlockSpec((rows, D), lambda b, pos: (b, 0))
    grid_spec = pltpu.PrefetchScalarGridSpec(
        num_scalar_prefetch=1,
        grid=(T // rows,),
        in_specs=[row, pl.BlockSpec((rows, 2), lambda b, pos: (b, 0)),
                  pl.BlockSpec((1, D), lambda b, pos: (0, 0)), pl.BlockSpec(memory_space=pl.ANY)],
        out_specs=row,
        scratch_shapes=[pltpu.VMEM((2, rows, D), F32), pltpu.VMEM((2, rows, D), F32),
                        pltpu.SemaphoreType.DMA((2,))],
    )
    return pl.pallas_call(
        _combine_kernel,
        grid_spec=grid_spec,
        out_shape=jax.ShapeDtypeStruct((T, D), F32),
        compiler_params=_cparams("arbitrary"),
        name="moe_combine",
    )(pos, h, gate, final_g, y)


def _rwkv_layer(h, ng, mu, w_rkv, w0, w1, w2, a0, a1, a2, g1, g2, k_k, k_a, r_k, gn_g, gn_b, w_o):
    D = h.shape[1]
    vec = lambda z: z.reshape(1, D)
    b16 = lambda z: z.astype(BF16)
    r, k, v, lw, a, g = _rwkv_pre(h, vec(ng), mu, b16(w_rkv), vec(w0), b16(w1), b16(w2), vec(a0),
                                  b16(a1), b16(a2), b16(g1), b16(g2))
    o = _rwkv_scan(r, k, v, lw, a, g, vec(k_k), vec(k_a), vec(r_k), vec(gn_g), vec(gn_b))
    return _mm_res(o, b16(w_o), h)


def _fox_layer(h, ng, w_in, b_f, q_gain, k_gain, w_o):
    T, D = h.shape
    H = D // HEAD_DIM
    order = jnp.argsort(b_f)
    w_main = w_in[:, :4 * D].reshape(D, 4, H, HEAD_DIM)[:, :, order, :].reshape(D, 4 * D)
    w_out = w_o.reshape(H, HEAD_DIM, D)[order].reshape(D, D)
    q, k, v, og, c2, hi, lo, lo2 = _fox_pre(
        h, ng.reshape(1, D), w_main.astype(BF16), w_in[:, 4 * D:][:, order].astype(BF16),
        b_f[order].reshape(1, H), jnp.tile(q_gain, H).reshape(1, D), jnp.tile(k_gain, H).reshape(1, D))
    one = jnp.ones_like(hi)
    q_aug = _augment(q, jnp.stack([hi, lo, lo2, one, one, one], axis=-1))
    k_aug = _augment(k, jnp.stack([one, one, one, -hi, -lo, -lo2], axis=-1))
    blk = min(ATTN_BLOCK, T)
    jlo = _first_live_block(c2, q_gain, k_gain, blk)
    o = _fox_attn(jlo, q_aug, k_aug, v.T, og, blk)
    return _mm_res(o, w_out.astype(BF16), h)


def _moe_layer(h, ng, final_g, w_router, w_gu, w_down):
    T, D = h.shape
    E, RB = N_EXPERTS, MOE_ROWS
    u, info, cnt, csr = _router(h, ng.reshape(1, D), w_router)
    e = info[:, 0:2].astype(jnp.int32)
    gate = info[:, 2:4]
    rank = info[:, 4:6].astype(jnp.int32)
    counts = cnt[0].astype(jnp.int32)
    padded = (counts + RB - 1) // RB * RB
    ends = jnp.cumsum(padded)
    starts = ends - padded
    n_blocks = (2 * T) // RB + E
    dest = starts[e] + rank
    blocks = jnp.arange(n_blocks, dtype=jnp.int32)
    block_e = jnp.minimum(jnp.sum(blocks[:, None] * RB >= ends[None, :], axis=1), E - 1).astype(jnp.int32)
    n_live = (ends[-1] // RB).astype(jnp.int32).reshape(1)
    first_rank = blocks * RB - starts[block_e]
    csr_e = csr[:, 0, :].astype(jnp.int32)[:, block_e]
    n_chunks = csr_e.shape[0]
    c_lo = jnp.clip(jnp.sum(csr_e <= first_rank[None, :], axis=0) - 1, 0, n_chunks - 1).astype(jnp.int32)
    c_hi = jnp.clip(jnp.sum(csr_e < first_rank[None, :] + RB, axis=0) - 1, 0, n_chunks - 1).astype(jnp.int32)
    y = _moe_ffn(block_e, n_live, c_lo, c_hi, dest.T, u, w_gu.astype(BF16), w_down.astype(BF16), n_blocks)
    return _combine(dest.reshape(-1), h, gate, final_g.reshape(1, D), y)


def kernel(x, norm_g, final_g, rwkv_mu, rwkv_w_rkv, rwkv_w0, rwkv_w1, rwkv_w2, rwkv_a0, rwkv_a1,
           rwkv_a2, rwkv_g1, rwkv_g2, rwkv_k_k, rwkv_k_a, rwkv_r_k, rwkv_gn_g, rwkv_gn_b, rwkv_w_o,
           fox_w_in, fox_b_f, fox_q_gain, fox_k_gain, fox_w_o, ffn_w_gu, ffn_w_down,
           moe_w_router, moe_w_gu, moe_w_down):
    B, T, D = x.shape
    h = x.reshape(B * T, D)
    h = _rwkv_layer(h, norm_g[0, 0], rwkv_mu[0], rwkv_w_rkv[0], rwkv_w0[0], rwkv_w1[0], rwkv_w2[0],
                    rwkv_a0[0], rwkv_a1[0], rwkv_a2[0], rwkv_g1[0], rwkv_g2[0], rwkv_k_k[0],
                    rwkv_k_a[0], rwkv_r_k[0], rwkv_gn_g[0], rwkv_gn_b[0], rwkv_w_o[0])
    h = _ffn(h, norm_g[0, 1].reshape(1, D), ffn_w_gu[0].astype(BF16), ffn_w_down[0].astype(BF16))
    h = _fox_layer(h, norm_g[1, 0], fox_w_in[0], fox_b_f[0], fox_q_gain[0], fox_k_gain[0], fox_w_o[0])
    out = _moe_layer(h, norm_g[1, 1], final_g, moe_w_router[0], moe_w_gu[0], moe_w_down[0])
    return out.reshape(B, T, D)
```

```python
import functools

import jax
import jax.numpy as jnp
from jax import lax
from jax.experimental import pallas as pl
from jax.experimental.pallas import tpu as pltpu

F32 = jnp.float32
BF16 = jnp.bfloat16

HEAD_DIM = 64
PAIR = 2 * HEAD_DIM
GN_EPS = 64e-5
RMS_EPS = 1e-6
N_EXPERTS = 8
SCAN_CHUNK = 64
SCAN_CHUNKS_PER_STEP = 2
MOE_ROWS = 512
MOE_SUB = 128
MOE_CHUNK = 512
ATTN_BLOCK = 512
LOG2E = 1.4426950408889634
VMEM_LIMIT = 56 * 1024 * 1024
VMEM_LIMIT_MOE = 60 * 1024 * 1024


def _cparams(*sem, vmem_limit=VMEM_LIMIT):
    return pltpu.CompilerParams(dimension_semantics=sem, vmem_limit_bytes=vmem_limit)


def _dot(a, b):
    return jnp.dot(a.astype(BF16), b.astype(BF16), preferred_element_type=F32)


def _dot_nt(a, b):
    return lax.dot_general(a.astype(BF16), b.astype(BF16), (((1,), (1,)), ((), ())),
                           preferred_element_type=F32)


def _dot_tn(a, b):
    return lax.dot_general(a.astype(BF16), b.astype(BF16), (((0,), (0,)), ((), ())),
                           preferred_element_type=F32)


def _split_hi_lo(x):
    hi = x.astype(BF16)
    lo = (x - hi.astype(F32)).astype(BF16)
    return hi, lo


def _rms(x, g):
    ms = jnp.mean(x * x, axis=-1, keepdims=True)
    return x * lax.rsqrt(ms + RMS_EPS) * g


def _sigmoid(x):
    return 1.0 / (1.0 + jnp.exp(-x))


def _full(shape):
    return pl.BlockSpec(shape, lambda *_: (0,) * len(shape))


def _rwkv_pre_kernel(h_ref, hp_ref, ng_ref, mu_ref, wrkv_ref, w0_ref, w1_ref, w2_ref, a0_ref,
                     a1_ref, a2_ref, g1_ref, g2_ref, r_ref, k_ref, v_ref, lw_ref, a_ref, g_ref):
    i = pl.program_id(0)
    ng = ng_ref[...]
    u = _rms(h_ref[...], ng)
    up = _rms(hp_ref[7:8, :], ng)
    up = jnp.where(i == 0, 0.0, up)
    row = lax.broadcasted_iota(jnp.int32, u.shape, 0)
    u_prev = jnp.where(row == 0, up, pltpu.roll(u, 1, axis=0))
    dx = u_prev - u
    mix = lambda n: u + dx * mu_ref[n:n + 1, :]
    r_ref[...] = _dot(mix(0), wrkv_ref[0])
    k_ref[...] = _dot(mix(1), wrkv_ref[1])
    v_ref[...] = _dot(mix(2), wrkv_ref[2])
    z = w0_ref[...] + _dot(jnp.tanh(_dot(mix(3), w1_ref[...])), w2_ref[...])
    softplus_neg = jnp.maximum(-z, 0.0) + jnp.log(1.0 + jnp.exp(-jnp.abs(z)))
    lw_ref[...] = -jnp.exp(-softplus_neg - 0.5)
    a_ref[...] = _sigmoid(a0_ref[...] + _dot(_dot(mix(4), a1_ref[...]), a2_ref[...]))
    g_ref[...] = _dot(_sigmoid(_dot(mix(5), g1_ref[...])), g2_ref[...])


def _rwkv_pre(h, ng, mu, wrkv, w0, w1, w2, a0, a1, a2, g1, g2, tm=256):
    T, D = h.shape
    row = pl.BlockSpec((tm, D), lambda i: (i, 0))
    prev = pl.BlockSpec((8, D), lambda i: (jnp.maximum(i * (tm // 8) - 1, 0), 0))
    ins = [ng, mu, wrkv, w0, w1, w2, a0, a1, a2, g1, g2]
    return pl.pallas_call(
        _rwkv_pre_kernel,
        grid=(T // tm,),
        in_specs=[row, prev] + [_full(a.shape) for a in ins],
        out_specs=[row] * 6,
        out_shape=[jax.ShapeDtypeStruct((T, D), F32)] * 6,
        compiler_params=_cparams("parallel"),
        name="rwkv_pre",
    )(h, h, *ins)


def _scan_kernel(r_ref, k_ref, v_ref, lw_ref, a_ref, g_ref, kk_ref, ka_ref, rk_ref, gng_ref,
                 gnb_ref, o_ref, s_ref):
    L = SCAN_CHUNK
    n_pairs = r_ref.shape[1] // PAIR

    @pl.when(pl.program_id(0) == 0)
    def _():
        s_ref[...] = jnp.zeros_like(s_ref)

    row = lax.broadcasted_iota(jnp.int32, (L, L), 0)
    col = lax.broadcasted_iota(jnp.int32, (L, L), 1)
    tri_incl = jnp.where(col <= row, 1.0, 0.0).astype(BF16)
    eye = jnp.where(col == row, 1.0, 0.0).astype(F32)
    lane = lax.broadcasted_iota(jnp.int32, (1, PAIR), 1)
    head0 = lane < HEAD_DIM
    arow = lax.broadcasted_iota(jnp.int32, (4 * L, PAIR), 0)
    acol = lax.broadcasted_iota(jnp.int32, (4 * L, PAIR), 1)
    t_idx = arow % L + jnp.where(arow < 2 * L, 0, 1)
    j_idx = acol % L
    a_mask = j_idx < t_idx
    brow = lax.broadcasted_iota(jnp.int32, (PAIR, PAIR), 0)
    bcol = lax.broadcasted_iota(jnp.int32, (PAIR, PAIR), 1)
    blockdiag = (brow < HEAD_DIM) == (bcol < HEAD_DIM)

    def head_sum(x):
        s0 = jnp.sum(jnp.where(head0, x, 0.0), axis=-1, keepdims=True)
        s1 = jnp.sum(jnp.where(head0, 0.0, x), axis=-1, keepdims=True)
        return jnp.where(head0, s0, s1)

    pairs = range(n_pairs)
    heads = [(p, h) for p in pairs for h in range(2)]
    sls = [slice(p * PAIR, (p + 1) * PAIR) for p in pairs]
    n_chunks = r_ref.shape[0] // L
    chunks = range(n_chunks)
    rows = [slice(c * L, (c + 1) * L) for c in chunks]
    cp = [(c, p) for c in chunks for p in pairs]
    cph = [(c, p, h) for c in chunks for p, h in heads]

    lws = {(c, p): lw_ref[rows[c], sls[p]] for c, p in cp}
    splits = {k: _split_hi_lo(lw) for k, lw in lws.items()}
    css = {k: jnp.dot(tri_incl, hi, preferred_element_type=F32) + jnp.dot(tri_incl, lo, preferred_element_type=F32)
           for k, (hi, lo) in splits.items()}
    prep = {}
    for c, p in cp:
        rw, sl, cs, lw = rows[c], sls[p], css[c, p], lws[c, p]
        w_inv = jnp.exp(-cs)
        w_last = jnp.exp(cs[L - 1:L, :])
        r = r_ref[rw, sl]
        k = k_ref[rw, sl]
        a = a_ref[rw, sl]
        kk = k * kk_ref[:, sl]
        kk = kk / jnp.maximum(jnp.sqrt(head_sum(kk * kk)), 1e-12)
        kmod = k * (1.0 + (a - 1.0) * ka_ref[:, sl])
        at = -kk * jnp.exp(cs - lw)
        bt = kk * a * w_inv
        kt = kmod * w_inv
        rt = r * jnp.exp(cs)
        v = v_ref[rw, sl]
        prep[c, p] = dict(atrt=jnp.concatenate([at, rt], axis=0).astype(BF16),
                          x_stack=jnp.concatenate([jnp.where(head0, at, 0.0), jnp.where(head0, 0.0, at),
                                                   jnp.where(head0, rt, 0.0), jnp.where(head0, 0.0, rt)],
                                                  axis=0).astype(BF16),
                          bk=jnp.concatenate([bt, kt], axis=0).astype(BF16),
                          bkhat=jnp.concatenate([bt * w_last, kt * w_last], axis=0).astype(BF16),
                          w_last=w_last, v=v, v_bf=v.astype(BF16), rk=r * kmod)
    aas = {k: jnp.where(a_mask, _dot_nt(prep[k]["x_stack"], prep[k]["bk"]), 0.0) for k in cp}
    akv = {}
    pw = {}
    t_inv = {}
    for c, p, h in cph:
        a_rows = aas[c, p][h * L:(h + 1) * L, :]
        vv = jnp.concatenate([prep[c, p]["v_bf"], prep[c, p]["v_bf"]], axis=0)
        akv[c, p, h] = _dot(jnp.where(head0, 0.0, a_rows), vv)
        pw[c, p, h] = a_rows[:, :L]
        t_inv[c, p, h] = eye + pw[c, p, h]
    for _ in range(5):
        for k in cph:
            pw[k] = _dot(pw[k], pw[k])
        for k in cph:
            t_inv[k] = t_inv[k] + _dot(t_inv[k], pw[k])

    states = [s_ref[p] for p in pairs]
    for c in chunks:
        inter = [_dot_nt(prep[c, p]["atrt"], states[p]) for p in pairs]
        u_h = {(p, h): _dot(t_inv[c, p, h], inter[p][:L, :] + akv[c, p, h]) for p, h in heads}
        y_h = {}
        for p, h in heads:
            ar = aas[c, p][(2 + h) * L:(3 + h) * L, :]
            y_h[p, h] = _dot(ar, jnp.concatenate([u_h[p, h].astype(BF16), prep[c, p]["v_bf"]], axis=0))
        new_states = []
        for p in pairs:
            pr = prep[c, p]
            u = jnp.where(head0, u_h[p, 0], u_h[p, 1])
            uv = jnp.concatenate([u.astype(BF16), pr["v_bf"]], axis=0)
            new_states.append(states[p] * pr["w_last"] + jnp.where(blockdiag, _dot_tn(uv, pr["bkhat"]), 0.0))
        states = new_states
        for p in pairs:
            sl = sls[p]
            pr = prep[c, p]
            y = inter[p][L:, :] + jnp.where(head0, y_h[p, 0], y_h[p, 1])
            mean = head_sum(y) * (1.0 / HEAD_DIM)
            d = y - mean
            var = head_sum(d * d) * (1.0 / HEAD_DIM)
            yn = d * lax.rsqrt(var + GN_EPS) * gng_ref[:, sl] + gnb_ref[:, sl]
            bonus = head_sum(pr["rk"] * rk_ref[:, sl]) * pr["v"]
            o_ref[rows[c], sl] = ((yn + bonus) * g_ref[rows[c], sl]).astype(o_ref.dtype)
    for p in pairs:
        s_ref[p] = states[p]


def _rwkv_scan(r, k, v, lw, a, g, k_k, k_a, r_k, gn_g, gn_b):
    T, D = r.shape
    rows = SCAN_CHUNK * SCAN_CHUNKS_PER_STEP
    row = pl.BlockSpec((rows, D), lambda c: (c, 0))
    par = _full((1, D))
    return pl.pallas_call(
        _scan_kernel,
        grid=(T // rows,),
        in_specs=[row] * 6 + [par] * 5,
        out_specs=row,
        out_shape=jax.ShapeDtypeStruct((T, D), BF16),
        scratch_shapes=[pltpu.VMEM((D // PAIR, PAIR, PAIR), F32)],
        compiler_params=_cparams("arbitrary"),
        name="rwkv_scan",
    )(r, k, v, lw, a, g, k_k, k_a, r_k, gn_g, gn_b)


def _mm_res_kernel(x_ref, w_ref, res_ref, o_ref):
    o_ref[...] = res_ref[...] + _dot(x_ref[...], w_ref[...])


def _mm_res(x, w, res, tm=512):
    T, K = x.shape
    N = w.shape[1]
    return pl.pallas_call(
        _mm_res_kernel,
        grid=(T // tm,),
        in_specs=[pl.BlockSpec((tm, K), lambda i: (i, 0)), _full((K, N)),
                  pl.BlockSpec((tm, N), lambda i: (i, 0))],
        out_specs=pl.BlockSpec((tm, N), lambda i: (i, 0)),
        out_shape=jax.ShapeDtypeStruct((T, N), F32),
        compiler_params=_cparams("parallel"),
        name="mm_res",
    )(x, w, res)


def _swiglu_step(u_ref, wg_ref, wu_ref, wd_ref, acc_ref):
    u = u_ref[...]
    gate = jnp.dot(u, wg_ref[...].astype(BF16), preferred_element_type=F32)
    up = jnp.dot(u, wu_ref[...].astype(BF16), preferred_element_type=F32)
    hid = (gate * _sigmoid(gate) * up).astype(BF16)
    acc_ref[...] += jnp.dot(hid, wd_ref[...].astype(BF16), preferred_element_type=F32)


def _ffn_kernel(h_ref, ng_ref, wg_ref, wu_ref, wd_ref, o_ref, u_ref, acc_ref):
    f = pl.program_id(1)

    @pl.when(f == 0)
    def _():
        u_ref[...] = _rms(h_ref[...], ng_ref[...]).astype(BF16)
        acc_ref[...] = jnp.zeros_like(acc_ref)

    _swiglu_step(u_ref, wg_ref, wu_ref, wd_ref, acc_ref)

    @pl.when(f == pl.num_programs(1) - 1)
    def _():
        o_ref[...] = h_ref[...] + acc_ref[...]


def _ffn(h, ng, w_gu, w_down, tm=1024, tf=512):
    T, D = h.shape
    F = w_down.shape[0]
    nf = F // tf
    row = pl.BlockSpec((tm, D), lambda i, f: (i, 0))
    return pl.pallas_call(
        _ffn_kernel,
        grid=(T // tm, nf),
        in_specs=[row, _full((1, D)),
                  pl.BlockSpec((D, tf), lambda i, f: (0, f)),
                  pl.BlockSpec((D, tf), lambda i, f: (0, f + nf)),
                  pl.BlockSpec((tf, D), lambda i, f: (f, 0))],
        out_specs=row,
        out_shape=jax.ShapeDtypeStruct((T, D), F32),
        scratch_shapes=[pltpu.VMEM((tm, D), BF16), pltpu.VMEM((tm, D), F32)],
        compiler_params=_cparams("parallel", "arbitrary"),
        name="ffn",
    )(h, ng, w_gu, w_gu, w_down)


def _moe_ffn_kernel(be_ref, nb_ref, clo_ref, chi_ref, dest_ref, tok_hbm, wg_ref, wu_ref, wd_ref, o_ref,
                    tok_ref, x_ref, acc_ref, sem):
    b = pl.program_id(0)
    f = pl.program_id(1)
    live = b < nb_ref[0]
    rows = o_ref.shape[0]
    ch = MOE_CHUNK

    @pl.when((b == 0) & (f == 0))
    def _():
        cp = pltpu.make_async_copy(tok_hbm, tok_ref, sem)
        cp.start()
        cp.wait()

    @pl.when(f == 0)
    def _():
        acc_ref[...] = jnp.zeros_like(acc_ref)

    @pl.when((f == 0) & live)
    def _():
        for s in range(rows // MOE_SUB):
            sub = slice(s * MOE_SUB, (s + 1) * MOE_SUB)
            row_id = b * rows + s * MOE_SUB + lax.broadcasted_iota(jnp.int32, (MOE_SUB, 1), 0)
            w = b * (rows // MOE_SUB) + s

            def chunk(c, _, sub=sub, row_id=row_id):
                start = pl.multiple_of(c * ch, ch)
                d = dest_ref[:, pl.ds(start, ch)]
                onehot = (jnp.where(row_id == d[0:1, :], 1.0, 0.0)
                          + jnp.where(row_id == d[1:2, :], 1.0, 0.0)).astype(BF16)
                acc_ref[sub, :] += jnp.dot(onehot, tok_ref[pl.ds(start, ch), :], preferred_element_type=F32)
                return 0

            lax.fori_loop(clo_ref[w], chi_ref[w] + 1, chunk, 0)
        x_ref[...] = acc_ref[...].astype(BF16)
        acc_ref[...] = jnp.zeros_like(acc_ref)

    @pl.when(live)
    def _():
        _swiglu_step(x_ref, wg_ref.at[0], wu_ref.at[0], wd_ref.at[0], acc_ref)

    @pl.when(f == pl.num_programs(1) - 1)
    def _():
        o_ref[...] = acc_ref[...]


def _moe_ffn(block_e, n_live, c_lo, c_hi, dest_t, tok, w_gu, w_down, n_blocks, tf=512):
    T, D = tok.shape
    F = w_down.shape[1]
    nf = F // tf
    tm = MOE_ROWS
    grid_spec = pltpu.PrefetchScalarGridSpec(
        num_scalar_prefetch=4,
        grid=(n_blocks, nf),
        in_specs=[pl.BlockSpec((2, T), lambda b, f, *_: (0, 0)),
                  pl.BlockSpec(memory_space=pl.ANY),
                  pl.BlockSpec((1, D, tf), lambda b, f, be, *_: (be[b], 0, f)),
                  pl.BlockSpec((1, D, tf), lambda b, f, be, *_: (be[b], 0, f + nf)),
                  pl.BlockSpec((1, tf, D), lambda b, f, be, *_: (be[b], f, 0))],
        out_specs=pl.BlockSpec((tm, D), lambda b, f, *_: (b, 0)),
        scratch_shapes=[pltpu.VMEM((T, D), BF16), pltpu.VMEM((tm, D), BF16), pltpu.VMEM((tm, D), F32),
                        pltpu.SemaphoreType.DMA(())],
    )
    return pl.pallas_call(
        _moe_ffn_kernel,
        grid_spec=grid_spec,
        out_shape=jax.ShapeDtypeStruct((n_blocks * tm, D), F32),
        compiler_params=_cparams("arbitrary", "arbitrary", vmem_limit=VMEM_LIMIT_MOE),
        name="moe_ffn",
    )(block_e, n_live, c_lo, c_hi, dest_t, tok, w_gu, w_gu, w_down)


def _fox_pre_kernel(h_ref, ng_ref, w_ref, wf_ref, bf_ref, qg_ref, kg_ref, sel_ref, selt_ref, pq_ref, pk_ref,
                    oneq_ref, onek_ref, q_ref, k_ref, v_ref, og_ref, c2_ref, augq_ref, augk_ref, carry_ref):
    i = pl.program_id(0)
    tm, D = h_ref.shape
    H = c2_ref.shape[1]

    @pl.when(i == 0)
    def _():
        carry_ref[...] = jnp.zeros_like(carry_ref)

    u = _rms(h_ref[...], ng_ref[...]).astype(BF16)

    def head_norm(x, gain):
        ms = _dot(x * x, sel_ref[...]) * (1.0 / HEAD_DIM)
        inv_hi, inv_lo = _split_hi_lo(lax.rsqrt(ms + RMS_EPS))
        inv = (jnp.dot(inv_hi, selt_ref[...], preferred_element_type=F32)
               + jnp.dot(inv_lo, selt_ref[...], preferred_element_type=F32))
        return x * inv * gain

    q = jnp.dot(u, w_ref[:, 0:D], preferred_element_type=F32)
    q_ref[...] = (head_norm(q, qg_ref[...]) * (HEAD_DIM ** -0.5 * LOG2E)).astype(BF16)
    k = jnp.dot(u, w_ref[:, D:2 * D], preferred_element_type=F32)
    k_ref[...] = head_norm(k, kg_ref[...]).astype(BF16)
    v_ref[...] = jnp.dot(u, w_ref[:, 2 * D:3 * D], preferred_element_type=F32).astype(BF16)
    og_ref[...] = _sigmoid(jnp.dot(u, w_ref[:, 3 * D:4 * D], preferred_element_type=F32)).astype(BF16)

    fl = jnp.dot(u, wf_ref[...], preferred_element_type=F32) + bf_ref[...]
    log_f = jnp.minimum(fl, 0.0) - jnp.log(1.0 + jnp.exp(-jnp.abs(fl)))
    row = lax.broadcasted_iota(jnp.int32, (tm, tm), 0)
    col = lax.broadcasted_iota(jnp.int32, (tm, tm), 1)
    tri = jnp.where(col <= row, 1.0, 0.0).astype(BF16)
    hi, lo = _split_hi_lo(log_f)
    lo2 = (log_f - hi.astype(F32) - lo.astype(F32)).astype(BF16)
    c = (jnp.dot(tri, hi, preferred_element_type=F32) + jnp.dot(tri, lo, preferred_element_type=F32)
         + jnp.dot(tri, lo2, preferred_element_type=F32)) + carry_ref[...]
    carry_ref[...] = c[tm - 1:tm, :]
    c2 = c * LOG2E
    c2_ref[...] = c2[:, 0:H]
    c_hi, c_lo = _split_hi_lo(c2)
    c_lo2 = (c2 - c_hi.astype(F32) - c_lo.astype(F32)).astype(BF16)
    group = lax.broadcasted_iota(jnp.int32, c2.shape, 1) // H
    pieces = jnp.where(group == 0, c_hi, jnp.where(group == 1, c_lo, c_lo2))
    augq_ref[...] = (jnp.dot(pieces, pq_ref[...], preferred_element_type=F32) + oneq_ref[...]).astype(BF16)
    augk_ref[...] = (jnp.dot(pieces, pk_ref[...], preferred_element_type=F32) + onek_ref[...]).astype(BF16)


def _fox_pre(h, ng, w_main, w_f, b_f, q_gain, k_gain, tm=256):
    T, D = h.shape
    H = D // HEAD_DIM
    sel = (jnp.arange(D)[:, None] // HEAD_DIM == jnp.arange(H)[None, :]).astype(BF16)
    piece = jnp.arange(3 * H)[:, None] // H
    head = jnp.arange(3 * H)[:, None] % H
    lane = jnp.arange(D)[None, :]
    at = lambda j: lane == head * HEAD_DIM + j
    p_q = at(piece).astype(BF16)
    p_k = -at(piece + 3).astype(BF16)
    off = jnp.arange(D) % HEAD_DIM
    one_q = ((off >= 3) & (off < 6)).astype(F32).reshape(1, D)
    one_k = (off < 3).astype(F32).reshape(1, D)
    row = pl.BlockSpec((tm, D), lambda i: (i, 0))
    ins = [ng, w_main, jnp.tile(w_f, (1, 3)), jnp.tile(b_f, (1, 3)), q_gain, k_gain, sel, sel.T, p_q, p_k,
           one_q, one_k]
    return pl.pallas_call(
        _fox_pre_kernel,
        grid=(T // tm,),
        in_specs=[row] + [_full(a.shape) for a in ins],
        out_specs=[row] * 4 + [pl.BlockSpec((tm, H), lambda i: (i, 0))] + [row] * 2,
        out_shape=([jax.ShapeDtypeStruct((T, D), BF16)] * 4 + [jax.ShapeDtypeStruct((T, H), F32)]
                   + [jax.ShapeDtypeStruct((T, D), BF16)] * 2),
        scratch_shapes=[pltpu.VMEM((1, 3 * H), F32)],
        compiler_params=_cparams("arbitrary"),
        name="fox_pre",
    )(h, *ins)


def _fox_attn_kernel(jlo_ref, q_ref, augq_ref, k_ref, augk_ref, vt_ref, og_ref, o_ref, s_ref, *, blk):
    p_idx = pl.program_id(0)
    i = pl.program_id(1)
    head0 = lax.broadcasted_iota(jnp.int32, (1, PAIR), 1) < HEAD_DIM
    q, augq = q_ref[...], augq_ref[...]
    zero = jnp.zeros_like(q)
    q_heads = [jnp.concatenate([jnp.where(m, q, zero), jnp.where(m, augq, zero)], axis=1)
               for m in (head0, jnp.logical_not(head0))]
    causal = (lax.broadcasted_iota(jnp.int32, (blk, blk), 0)
              <= lax.broadcasted_iota(jnp.int32, (blk, blk), 1))

    def scores_into(j, slot):
        start = pl.multiple_of(j * blk, blk)
        kb = jnp.concatenate([k_ref[pl.ds(start, blk), :], augk_ref[pl.ds(start, blk), :]], axis=1)
        for h in range(2):
            s_ref[slot, h] = lax.dot_general(kb, q_heads[h], (((1,), (1,)), ((), ())),
                                             preferred_element_type=F32)

    def softmax_pv(j, slot, carry, masked):
        start = pl.multiple_of(j * blk, blk)
        out = []
        for h in range(2):
            m_prev, l_prev, acc_prev = carry[h]
            vtb = vt_ref[h * HEAD_DIM:(h + 1) * HEAD_DIM, pl.ds(start, blk)]
            s = s_ref[slot, h]
            if masked:
                s = jnp.where(causal, s, -jnp.inf)
            m_new = jnp.maximum(m_prev, jnp.max(s, axis=0, keepdims=True))
            alpha = jnp.exp2(m_prev - m_new)
            p = jnp.exp2(s - m_new)
            l_new = alpha * l_prev + jnp.sum(p, axis=0, keepdims=True)
            acc_new = alpha * acc_prev + jnp.dot(vtb, p.astype(BF16), preferred_element_type=F32)
            out.append((m_new, l_new, acc_new))
        return tuple(out)

    init = tuple((jnp.full((1, blk), -jnp.inf, F32), jnp.zeros((1, blk), F32), jnp.zeros((HEAD_DIM, blk), F32))
                 for _ in range(2))
    jlo = jlo_ref[p_idx * pl.num_programs(1) + i]
    total = i - jlo + 1
    scores_into(jlo, 0)

    def two_blocks(t, carry):
        j = jlo + 2 * t
        scores_into(j + 1, 1)
        carry = softmax_pv(j, 0, carry, False)
        scores_into(j + 2, 0)
        return softmax_pv(j + 1, 1, carry, False)

    res = lax.fori_loop(0, (total - 1) // 2, two_blocks, init)

    def last_one(carry):
        return softmax_pv(i, 0, carry, True)

    def last_two(carry):
        scores_into(i, 1)
        carry = softmax_pv(i - 1, 0, carry, False)
        return softmax_pv(i, 1, carry, True)

    res = lax.cond(total % 2 == 1, last_one, last_two, res)
    o_t = jnp.concatenate([res[0][2] / res[0][1], res[1][2] / res[1][1]], axis=0)
    o_ref[...] = (o_t.T * og_ref[...].astype(F32)).astype(o_ref.dtype)


def _fox_attn(jlo, q, augq, k, augk, v_t, og, blk):
    D, T = v_t.shape
    n_pairs = D // PAIR
    qspec = pl.BlockSpec((blk, PAIR), lambda p, i, jlo: (i, p))
    kspec = pl.BlockSpec((T, PAIR), lambda p, i, jlo: (0, p))
    grid_spec = pltpu.PrefetchScalarGridSpec(
        num_scalar_prefetch=1,
        grid=(n_pairs, T // blk),
        in_specs=[qspec, qspec, kspec, kspec,
                  pl.BlockSpec((PAIR, T), lambda p, i, jlo: (p, 0)),
                  qspec],
        out_specs=qspec,
        scratch_shapes=[pltpu.VMEM((2, 2, blk, blk), F32)],
    )
    return pl.pallas_call(
        functools.partial(_fox_attn_kernel, blk=blk),
        grid_spec=grid_spec,
        out_shape=jax.ShapeDtypeStruct((T, D), BF16),
        compiler_params=_cparams("parallel", "arbitrary"),
        name="fox_attn",
    )(jlo, q, augq, k, augk, v_t, og)


def _first_live_block(c2, q_gain, k_gain, blk):
    T, H = c2.shape
    bound = 8.0 * LOG2E * 1.02 * jnp.max(jnp.abs(q_gain)) * jnp.max(jnp.abs(k_gain))
    c_first = c2[0::blk]
    c_last = c2[blk - 1::blk]
    dead = c_last[None, :, :] > c_first[:, None, :] + (2.0 * bound + 104.0 * LOG2E + 1.0)
    jlo = jnp.sum(dead, axis=1).astype(jnp.int32)
    jlo = jnp.min(jlo.reshape(-1, H // 2, 2), axis=-1)
    return jlo.T.reshape(-1)


def _router_kernel(h_ref, ng_ref, wr_ref, u_ref, info_ref, cnt_ref, csr_ref, carry_ref):
    i = pl.program_id(0)
    tm = h_ref.shape[0]
    E = N_EXPERTS

    @pl.when(i == 0)
    def _():
        carry_ref[...] = jnp.zeros_like(carry_ref)

    csr_ref[0] = carry_ref[...]
    u = _rms(h_ref[...], ng_ref[...])
    u_ref[...] = u.astype(BF16)
    u_hi, u_lo = _split_hi_lo(u)
    w_hi, w_lo = _split_hi_lo(wr_ref[...])
    logits = (jnp.dot(u_hi, w_hi, preferred_element_type=F32)
              + jnp.dot(u_hi, w_lo, preferred_element_type=F32)
              + jnp.dot(u_lo, w_hi, preferred_element_type=F32))
    lane = lax.broadcasted_iota(jnp.int32, (tm, E), 1)
    v1 = jnp.max(logits, axis=-1, keepdims=True)
    e1 = jnp.min(jnp.where(logits == v1, lane, E), axis=-1, keepdims=True)
    rest = jnp.where(lane == e1, -jnp.inf, logits)
    v2 = jnp.max(rest, axis=-1, keepdims=True)
    e2 = jnp.min(jnp.where(rest == v2, lane, E), axis=-1, keepdims=True)
    g1 = 1.0 / (1.0 + jnp.exp(v2 - v1))
    g2 = 1.0 - g1
    oh1 = lane == e1
    oh2 = lane == e2
    onehot = jnp.where(oh1 | oh2, 1.0, 0.0).astype(BF16)
    row = lax.broadcasted_iota(jnp.int32, (tm, tm), 0)
    col = lax.broadcasted_iota(jnp.int32, (tm, tm), 1)
    tri_strict = jnp.where(col < row, 1.0, 0.0).astype(BF16)
    before = jnp.dot(tri_strict, onehot, preferred_element_type=F32) + carry_ref[...]
    rank1 = jnp.sum(jnp.where(oh1, before, 0.0), axis=-1, keepdims=True)
    rank2 = jnp.sum(jnp.where(oh2, before, 0.0), axis=-1, keepdims=True)
    info = jnp.where(lane == 0, e1.astype(F32), 0.0)
    info = jnp.where(lane == 1, e2.astype(F32), info)
    info = jnp.where(lane == 2, g1, info)
    info = jnp.where(lane == 3, g2, info)
    info = jnp.where(lane == 4, rank1, info)
    info = jnp.where(lane == 5, rank2, info)
    info_ref[...] = info
    total = carry_ref[...] + jnp.sum(onehot.astype(F32), axis=0, keepdims=True)
    carry_ref[...] = total
    cnt_ref[...] = total


def _router(h, ng, w_router):
    T, D = h.shape
    E = N_EXPERTS
    tm = min(MOE_CHUNK, T)
    row = pl.BlockSpec((tm, D), lambda i: (i, 0))
    return pl.pallas_call(
        _router_kernel,
        grid=(T // tm,),
        in_specs=[row, _full((1, D)), _full((D, E))],
        out_specs=[row, pl.BlockSpec((tm, E), lambda i: (i, 0)), _full((1, E)),
                   pl.BlockSpec((1, 1, E), lambda i: (i, 0, 0))],
        out_shape=[jax.ShapeDtypeStruct((T, D), BF16), jax.ShapeDtypeStruct((T, E), F32),
                   jax.ShapeDtypeStruct((1, E), F32), jax.ShapeDtypeStruct((T // tm, 1, E), F32)],
        scratch_shapes=[pltpu.VMEM((1, E), F32)],
        compiler_params=_cparams("arbitrary"),
        name="router",
    )(h, ng, w_router)


def _combine_kernel(pos_ref, h_ref, gate_ref, fg_ref, y_ref, o_ref, ya_ref, yb_ref, sem):
    b = pl.program_id(0)
    nb = pl.num_programs(0)
    rows = h_ref.shape[0]

    def copies(blk, r):
        slot = blk % 2
        base = 2 * (blk * rows + r)
        return (pltpu.make_async_copy(y_ref.at[pl.ds(pos_ref[base], 1)], ya_ref.at[slot, pl.ds(r, 1)],
                                      sem.at[slot]),
                pltpu.make_async_copy(y_ref.at[pl.ds(pos_ref[base + 1], 1)], yb_ref.at[slot, pl.ds(r, 1)],
                                      sem.at[slot]))

    def issue(blk):
        def body(r, _):
            ca, cb = copies(blk, r)
            ca.start()
            cb.start()
            return 0
        lax.fori_loop(0, rows, body, 0, unroll=8)

    @pl.when(b == 0)
    def _():
        issue(0)

    @pl.when(b + 1 < nb)
    def _():
        issue(b + 1)

    def wait(r, _):
        ca, cb = copies(b, r)
        ca.wait()
        cb.wait()
        return 0

    lax.fori_loop(0, rows, wait, 0, unroll=8)
    slot = b % 2
    gate = gate_ref[...]
    moe = gate[:, 0:1] * ya_ref[slot] + gate[:, 1:2] * yb_ref[slot]
    o_ref[...] = _rms(h_ref[...] + moe, fg_ref[...])


def _combine(pos, h, gate, final_g, y, rows=256):
    T, D = h.shape
    row = pl.BlockSpec((rows, D), lambda b, pos: (b, 0))
    grid_spec = pltpu.PrefetchScalarGridSpec(
        num_scalar_prefetch=1,
        grid=(T // rows,),
        in_specs=[row, pl.BlockSpec((rows, 2), lambda b, pos: (b, 0)),
                  pl.BlockSpec((1, D), lambda b, pos: (0, 0)), pl.BlockSpec(memory_space=pl.ANY)],
        out_specs=row,
        scratch_shapes=[pltpu.VMEM((2, rows, D), F32), pltpu.VMEM((2, rows, D), F32),
                        pltpu.SemaphoreType.DMA((2,))],
    )
    return pl.pallas_call(
        _combine_kernel,
        grid_spec=grid_spec,
        out_shape=jax.ShapeDtypeStruct((T, D), F32),
        compiler_params=_cparams("arbitrary"),
        name="moe_combine",
    )(pos, h, gate, final_g, y)


def _rwkv_layer(h, ng, mu, w_rkv, w0, w1, w2, a0, a1, a2, g1, g2, k_k, k_a, r_k, gn_g, gn_b, w_o):
    D = h.shape[1]
    vec = lambda z: z.reshape(1, D)
    b16 = lambda z: z.astype(BF16)
    r, k, v, lw, a, g = _rwkv_pre(h, vec(ng), mu, b16(w_rkv), vec(w0), b16(w1), b16(w2), vec(a0),
                                  b16(a1), b16(a2), b16(g1), b16(g2))
    o = _rwkv_scan(r, k, v, lw, a, g, vec(k_k), vec(k_a), vec(r_k), vec(gn_g), vec(gn_b))
    return _mm_res(o, b16(w_o), h)


def _fox_layer(h, ng, w_in, b_f, q_gain, k_gain, w_o):
    T, D = h.shape
    H = D // HEAD_DIM
    order = jnp.argsort(b_f)
    w_main = w_in[:, :4 * D].reshape(D, 4, H, HEAD_DIM)[:, :, order, :].reshape(D, 4 * D)
    w_out = w_o.reshape(H, HEAD_DIM, D)[order].reshape(D, D)
    q, k, v, og, c2, augq, augk = _fox_pre(
        h, ng.reshape(1, D), w_main.astype(BF16), w_in[:, 4 * D:][:, order].astype(BF16),
        b_f[order].reshape(1, H), jnp.tile(q_gain, H).reshape(1, D), jnp.tile(k_gain, H).reshape(1, D))
    blk = min(ATTN_BLOCK, T)
    jlo = _first_live_block(c2, q_gain, k_gain, blk)
    o = _fox_attn(jlo, q, augq, k, augk, v.T, og, blk)
    return _mm_res(o, w_out.astype(BF16), h)


def _moe_layer(h, ng, final_g, w_router, w_gu, w_down):
    T, D = h.shape
    E, RB = N_EXPERTS, MOE_ROWS
    u, info, cnt, csr = _router(h, ng.reshape(1, D), w_router)
    e = info[:, 0:2].astype(jnp.int32)
    gate = info[:, 2:4]
    rank = info[:, 4:6].astype(jnp.int32)
    counts = cnt[0].astype(jnp.int32)
    padded = (counts + RB - 1) // RB * RB
    ends = jnp.cumsum(padded)
    starts = ends - padded
    n_blocks = (2 * T) // RB + E
    dest = starts[e] + rank
    blocks = jnp.arange(n_blocks, dtype=jnp.int32)
    block_e = jnp.minimum(jnp.sum(blocks[:, None] * RB >= ends[None, :], axis=1), E - 1).astype(jnp.int32)
    n_live = (ends[-1] // RB).astype(jnp.int32).reshape(1)
    groups = jnp.arange(n_blocks * (RB // MOE_SUB), dtype=jnp.int32)
    group_e = jnp.repeat(block_e, RB // MOE_SUB)
    first_rank = groups * MOE_SUB - starts[group_e]
    csr_e = csr[:, 0, :].astype(jnp.int32)[:, group_e]
    n_chunks = csr_e.shape[0]
    c_lo = jnp.clip(jnp.sum(csr_e <= first_rank[None, :], axis=0) - 1, 0, n_chunks - 1).astype(jnp.int32)
    c_hi = jnp.clip(jnp.sum(csr_e < first_rank[None, :] + MOE_SUB, axis=0) - 1, 0, n_chunks - 1).astype(jnp.int32)
    y = _moe_ffn(block_e, n_live, c_lo, c_hi, dest.T, u, w_gu, w_down, n_blocks)
    return _combine(dest.reshape(-1), h, gate, final_g.reshape(1, D), y)


def kernel(x, norm_g, final_g, rwkv_mu, rwkv_w_rkv, rwkv_w0, rwkv_w1, rwkv_w2, rwkv_a0, rwkv_a1,
           rwkv_a2, rwkv_g1, rwkv_g2, rwkv_k_k, rwkv_k_a, rwkv_r_k, rwkv_gn_g, rwkv_gn_b, rwkv_w_o,
           fox_w_in, fox_b_f, fox_q_gain, fox_k_gain, fox_w_o, ffn_w_gu, ffn_w_down,
           moe_w_router, moe_w_gu, moe_w_down):
    B, T, D = x.shape
    h = x.reshape(B * T, D)
    h = _rwkv_layer(h, norm_g[0, 0], rwkv_mu[0], rwkv_w_rkv[0], rwkv_w0[0], rwkv_w1[0], rwkv_w2[0],
                    rwkv_a0[0], rwkv_a1[0], rwkv_a2[0], rwkv_g1[0], rwkv_g2[0], rwkv_k_k[0],
                    rwkv_k_a[0], rwkv_r_k[0], rwkv_gn_g[0], rwkv_gn_b[0], rwkv_w_o[0])
    h = _ffn(h, norm_g[0, 1].reshape(1, D), ffn_w_gu[0].astype(BF16), ffn_w_down[0].astype(BF16))
    h = _fox_layer(h, norm_g[1, 0], fox_w_in[0], fox_b_f[0], fox_q_gain[0], fox_k_gain[0], fox_w_o[0])
    out = _moe_layer(h, norm_g[1, 1], final_g, moe_w_router[0], moe_w_gu[0], moe_w_down[0])
    return out.reshape(B, T, D)
```

```python
import functools

import jax
import jax.numpy as jnp
from jax import lax
from jax.experimental import pallas as pl
from jax.experimental.pallas import tpu as pltpu

F32 = jnp.float32
BF16 = jnp.bfloat16

HEAD_DIM = 64
PAIR = 2 * HEAD_DIM
GN_EPS = 64e-5
RMS_EPS = 1e-6
N_EXPERTS = 8
SCAN_CHUNK = 64
SCAN_CHUNKS_PER_STEP = 2
MOE_ROWS = 1024
MOE_HALF = 512
MOE_SUB = 128
MOE_CHUNK = 512
ATTN_BLOCK = 512
LOG2E = 1.4426950408889634
VMEM_LIMIT = 56 * 1024 * 1024
VMEM_LIMIT_MOE = 60 * 1024 * 1024


def _cparams(*sem, vmem_limit=VMEM_LIMIT):
    return pltpu.CompilerParams(dimension_semantics=sem, vmem_limit_bytes=vmem_limit)


def _dot(a, b):
    return jnp.dot(a.astype(BF16), b.astype(BF16), preferred_element_type=F32)


def _dot_nt(a, b):
    return lax.dot_general(a.astype(BF16), b.astype(BF16), (((1,), (1,)), ((), ())),
                           preferred_element_type=F32)


def _dot_tn(a, b):
    return lax.dot_general(a.astype(BF16), b.astype(BF16), (((0,), (0,)), ((), ())),
                           preferred_element_type=F32)


def _split_hi_lo(x):
    hi = x.astype(BF16)
    lo = (x - hi.astype(F32)).astype(BF16)
    return hi, lo


def _rms(x, g):
    ms = jnp.mean(x * x, axis=-1, keepdims=True)
    return x * lax.rsqrt(ms + RMS_EPS) * g


def _sigmoid(x):
    return 1.0 / (1.0 + jnp.exp(-x))


def _full(shape):
    return pl.BlockSpec(shape, lambda *_: (0,) * len(shape))


def _rwkv_pre_kernel(h_ref, hp_ref, ng_ref, mu_ref, wrkv_ref, w0_ref, w1_ref, w2_ref, a0_ref,
                     a1_ref, a2_ref, g1_ref, g2_ref, r_ref, k_ref, v_ref, lw_ref, a_ref, g_ref):
    i = pl.program_id(0)
    ng = ng_ref[...]
    u = _rms(h_ref[...], ng)
    up = _rms(hp_ref[7:8, :], ng)
    up = jnp.where(i == 0, 0.0, up)
    row = lax.broadcasted_iota(jnp.int32, u.shape, 0)
    u_prev = jnp.where(row == 0, up, pltpu.roll(u, 1, axis=0))
    dx = u_prev - u
    mix = lambda n: u + dx * mu_ref[n:n + 1, :]
    r_ref[...] = _dot(mix(0), wrkv_ref[0])
    k_ref[...] = _dot(mix(1), wrkv_ref[1])
    v_ref[...] = _dot(mix(2), wrkv_ref[2])
    z = w0_ref[...] + _dot(jnp.tanh(_dot(mix(3), w1_ref[...])), w2_ref[...])
    softplus_neg = jnp.maximum(-z, 0.0) + jnp.log(1.0 + jnp.exp(-jnp.abs(z)))
    lw_ref[...] = -jnp.exp(-softplus_neg - 0.5)
    a_ref[...] = _sigmoid(a0_ref[...] + _dot(_dot(mix(4), a1_ref[...]), a2_ref[...]))
    g_ref[...] = _dot(_sigmoid(_dot(mix(5), g1_ref[...])), g2_ref[...])


def _rwkv_pre(h, ng, mu, wrkv, w0, w1, w2, a0, a1, a2, g1, g2, tm=256):
    T, D = h.shape
    row = pl.BlockSpec((tm, D), lambda i: (i, 0))
    prev = pl.BlockSpec((8, D), lambda i: (jnp.maximum(i * (tm // 8) - 1, 0), 0))
    ins = [ng, mu, wrkv, w0, w1, w2, a0, a1, a2, g1, g2]
    return pl.pallas_call(
        _rwkv_pre_kernel,
        grid=(T // tm,),
        in_specs=[row, prev] + [_full(a.shape) for a in ins],
        out_specs=[row] * 6,
        out_shape=[jax.ShapeDtypeStruct((T, D), F32)] * 6,
        compiler_params=_cparams("parallel"),
        name="rwkv_pre",
    )(h, h, *ins)


def _scan_kernel(r_ref, k_ref, v_ref, lw_ref, a_ref, g_ref, kk_ref, ka_ref, rk_ref, gng_ref,
                 gnb_ref, o_ref, s_ref):
    L = SCAN_CHUNK
    n_pairs = r_ref.shape[1] // PAIR

    @pl.when(pl.program_id(0) == 0)
    def _():
        s_ref[...] = jnp.zeros_like(s_ref)

    row = lax.broadcasted_iota(jnp.int32, (L, L), 0)
    col = lax.broadcasted_iota(jnp.int32, (L, L), 1)
    tri_incl = jnp.where(col <= row, 1.0, 0.0).astype(BF16)
    eye = jnp.where(col == row, 1.0, 0.0).astype(F32)
    lane = lax.broadcasted_iota(jnp.int32, (1, PAIR), 1)
    head0 = lane < HEAD_DIM
    arow = lax.broadcasted_iota(jnp.int32, (4 * L, PAIR), 0)
    acol = lax.broadcasted_iota(jnp.int32, (4 * L, PAIR), 1)
    t_idx = arow % L + jnp.where(arow < 2 * L, 0, 1)
    j_idx = acol % L
    a_mask = j_idx < t_idx
    brow = lax.broadcasted_iota(jnp.int32, (PAIR, PAIR), 0)
    bcol = lax.broadcasted_iota(jnp.int32, (PAIR, PAIR), 1)
    blockdiag = (brow < HEAD_DIM) == (bcol < HEAD_DIM)

    def head_sum(x):
        s0 = jnp.sum(jnp.where(head0, x, 0.0), axis=-1, keepdims=True)
        s1 = jnp.sum(jnp.where(head0, 0.0, x), axis=-1, keepdims=True)
        return jnp.where(head0, s0, s1)

    pairs = range(n_pairs)
    heads = [(p, h) for p in pairs for h in range(2)]
    sls = [slice(p * PAIR, (p + 1) * PAIR) for p in pairs]
    n_chunks = r_ref.shape[0] // L
    chunks = range(n_chunks)
    rows = [slice(c * L, (c + 1) * L) for c in chunks]
    cp = [(c, p) for c in chunks for p in pairs]
    cph = [(c, p, h) for c in chunks for p, h in heads]

    lws = {(c, p): lw_ref[rows[c], sls[p]] for c, p in cp}
    splits = {k: _split_hi_lo(lw) for k, lw in lws.items()}
    css = {k: jnp.dot(tri_incl, hi, preferred_element_type=F32) + jnp.dot(tri_incl, lo, preferred_element_type=F32)
           for k, (hi, lo) in splits.items()}
    prep = {}
    for c, p in cp:
        rw, sl, cs, lw = rows[c], sls[p], css[c, p], lws[c, p]
        w_inv = jnp.exp(-cs)
        w_last = jnp.exp(cs[L - 1:L, :])
        r = r_ref[rw, sl]
        k = k_ref[rw, sl]
        a = a_ref[rw, sl]
        kk = k * kk_ref[:, sl]
        kk = kk / jnp.maximum(jnp.sqrt(head_sum(kk * kk)), 1e-12)
        kmod = k * (1.0 + (a - 1.0) * ka_ref[:, sl])
        at = -kk * jnp.exp(cs - lw)
        bt = kk * a * w_inv
        kt = kmod * w_inv
        rt = r * jnp.exp(cs)
        v = v_ref[rw, sl]
        prep[c, p] = dict(atrt=jnp.concatenate([at, rt], axis=0).astype(BF16),
                          x_stack=jnp.concatenate([jnp.where(head0, at, 0.0), jnp.where(head0, 0.0, at),
                                                   jnp.where(head0, rt, 0.0), jnp.where(head0, 0.0, rt)],
                                                  axis=0).astype(BF16),
                          bk=jnp.concatenate([bt, kt], axis=0).astype(BF16),
                          bkhat=jnp.concatenate([bt * w_last, kt * w_last], axis=0).astype(BF16),
                          w_last=w_last, v=v, v_bf=v.astype(BF16), rk=r * kmod)
    aas = {k: jnp.where(a_mask, _dot_nt(prep[k]["x_stack"], prep[k]["bk"]), 0.0) for k in cp}
    akv = {}
    pw = {}
    t_inv = {}
    for c, p, h in cph:
        a_rows = aas[c, p][h * L:(h + 1) * L, :]
        vv = jnp.concatenate([prep[c, p]["v_bf"], prep[c, p]["v_bf"]], axis=0)
        akv[c, p, h] = _dot(jnp.where(head0, 0.0, a_rows), vv)
        pw[c, p, h] = a_rows[:, :L]
        t_inv[c, p, h] = eye + pw[c, p, h]
    for _ in range(5):
        for k in cph:
            pw[k] = _dot(pw[k], pw[k])
        for k in cph:
            t_inv[k] = t_inv[k] + _dot(t_inv[k], pw[k])

    states = [s_ref[p] for p in pairs]
    for c in chunks:
        inter = [_dot_nt(prep[c, p]["atrt"], states[p]) for p in pairs]
        u_h = {(p, h): _dot(t_inv[c, p, h], inter[p][:L, :] + akv[c, p, h]) for p, h in heads}
        y_h = {}
        for p, h in heads:
            ar = aas[c, p][(2 + h) * L:(3 + h) * L, :]
            y_h[p, h] = _dot(ar, jnp.concatenate([u_h[p, h].astype(BF16), prep[c, p]["v_bf"]], axis=0))
        new_states = []
        for p in pairs:
            pr = prep[c, p]
            u = jnp.where(head0, u_h[p, 0], u_h[p, 1])
            uv = jnp.concatenate([u.astype(BF16), pr["v_bf"]], axis=0)
            new_states.append(states[p] * pr["w_last"] + jnp.where(blockdiag, _dot_tn(uv, pr["bkhat"]), 0.0))
        states = new_states
        for p in pairs:
            sl = sls[p]
            pr = prep[c, p]
            y = inter[p][L:, :] + jnp.where(head0, y_h[p, 0], y_h[p, 1])
            mean = head_sum(y) * (1.0 / HEAD_DIM)
            d = y - mean
            var = head_sum(d * d) * (1.0 / HEAD_DIM)
            yn = d * lax.rsqrt(var + GN_EPS) * gng_ref[:, sl] + gnb_ref[:, sl]
            bonus = head_sum(pr["rk"] * rk_ref[:, sl]) * pr["v"]
            o_ref[rows[c], sl] = ((yn + bonus) * g_ref[rows[c], sl]).astype(o_ref.dtype)
    for p in pairs:
        s_ref[p] = states[p]


def _rwkv_scan(r, k, v, lw, a, g, k_k, k_a, r_k, gn_g, gn_b):
    T, D = r.shape
    rows = SCAN_CHUNK * SCAN_CHUNKS_PER_STEP
    row = pl.BlockSpec((rows, D), lambda c: (c, 0))
    par = _full((1, D))
    return pl.pallas_call(
        _scan_kernel,
        grid=(T // rows,),
        in_specs=[row] * 6 + [par] * 5,
        out_specs=row,
        out_shape=jax.ShapeDtypeStruct((T, D), BF16),
        scratch_shapes=[pltpu.VMEM((D // PAIR, PAIR, PAIR), F32)],
        compiler_params=_cparams("arbitrary"),
        name="rwkv_scan",
    )(r, k, v, lw, a, g, k_k, k_a, r_k, gn_g, gn_b)


def _mm_res_kernel(x_ref, w_ref, res_ref, o_ref):
    o_ref[...] = res_ref[...] + _dot(x_ref[...], w_ref[...])


def _mm_res(x, w, res, tm=512):
    T, K = x.shape
    N = w.shape[1]
    return pl.pallas_call(
        _mm_res_kernel,
        grid=(T // tm,),
        in_specs=[pl.BlockSpec((tm, K), lambda i: (i, 0)), _full((K, N)),
                  pl.BlockSpec((tm, N), lambda i: (i, 0))],
        out_specs=pl.BlockSpec((tm, N), lambda i: (i, 0)),
        out_shape=jax.ShapeDtypeStruct((T, N), F32),
        compiler_params=_cparams("parallel"),
        name="mm_res",
    )(x, w, res)


def _swiglu_step(u_ref, wg_ref, wu_ref, wd_ref, acc_ref):
    u = u_ref[...]
    gate = jnp.dot(u, wg_ref[...].astype(BF16), preferred_element_type=F32)
    up = jnp.dot(u, wu_ref[...].astype(BF16), preferred_element_type=F32)
    hid = (gate * _sigmoid(gate) * up).astype(BF16)
    acc_ref[...] += jnp.dot(hid, wd_ref[...].astype(BF16), preferred_element_type=F32)


def _ffn_kernel(h_ref, ng_ref, wg_ref, wu_ref, wd_ref, o_ref, u_ref, acc_ref):
    f = pl.program_id(1)

    @pl.when(f == 0)
    def _():
        u_ref[...] = _rms(h_ref[...], ng_ref[...]).astype(BF16)
        acc_ref[...] = jnp.zeros_like(acc_ref)

    _swiglu_step(u_ref, wg_ref, wu_ref, wd_ref, acc_ref)

    @pl.when(f == pl.num_programs(1) - 1)
    def _():
        o_ref[...] = h_ref[...] + acc_ref[...]


def _ffn(h, ng, w_gu, w_down, tm=1024, tf=512):
    T, D = h.shape
    F = w_down.shape[0]
    nf = F // tf
    row = pl.BlockSpec((tm, D), lambda i, f: (i, 0))
    return pl.pallas_call(
        _ffn_kernel,
        grid=(T // tm, nf),
        in_specs=[row, _full((1, D)),
                  pl.BlockSpec((D, tf), lambda i, f: (0, f)),
                  pl.BlockSpec((D, tf), lambda i, f: (0, f + nf)),
                  pl.BlockSpec((tf, D), lambda i, f: (f, 0))],
        out_specs=row,
        out_shape=jax.ShapeDtypeStruct((T, D), F32),
        scratch_shapes=[pltpu.VMEM((tm, D), BF16), pltpu.VMEM((tm, D), F32)],
        compiler_params=_cparams("parallel", "arbitrary"),
        name="ffn",
    )(h, ng, w_gu, w_gu, w_down)


def _moe_ffn_kernel(be_ref, live_ref, clo_ref, chi_ref, dest_ref, tok_hbm, wg_ref, wu_ref, wd_ref, o_ref,
                    tok_ref, x_ref, sem):
    b = pl.program_id(0)
    f = pl.program_id(1)
    rows = o_ref.shape[0]
    ch = MOE_CHUNK
    halves = rows // MOE_HALF

    @pl.when((b == 0) & (f == 0))
    def _():
        cp = pltpu.make_async_copy(tok_hbm, tok_ref, sem)
        cp.start()
        cp.wait()

    @pl.when(f == 0)
    def _():
        o_ref[...] = jnp.zeros_like(o_ref)
        for s in range(rows // MOE_SUB):
            sub = slice(s * MOE_SUB, (s + 1) * MOE_SUB)
            row_id = b * rows + s * MOE_SUB + lax.broadcasted_iota(jnp.int32, (MOE_SUB, 1), 0)
            w = b * (rows // MOE_SUB) + s

            def chunk(c, _, sub=sub, row_id=row_id):
                start = pl.multiple_of(c * ch, ch)
                d = dest_ref[:, pl.ds(start, ch)]
                onehot = (jnp.where(row_id == d[0:1, :], 1.0, 0.0)
                          + jnp.where(row_id == d[1:2, :], 1.0, 0.0)).astype(BF16)
                o_ref[sub, :] += jnp.dot(onehot, tok_ref[pl.ds(start, ch), :], preferred_element_type=F32)
                return 0

            lax.fori_loop(clo_ref[w], chi_ref[w] + 1, chunk, 0)
        x_ref[...] = o_ref[...].astype(BF16)
        o_ref[...] = jnp.zeros_like(o_ref)

    for half in range(halves):
        @pl.when(live_ref[b * halves + half] > 0)
        def _(half=half):
            part = pl.ds(half * MOE_HALF, MOE_HALF)
            _swiglu_step(x_ref.at[part], wg_ref.at[0], wu_ref.at[0], wd_ref.at[0], o_ref.at[part])


def _moe_ffn(block_e, half_live, c_lo, c_hi, dest_t, tok, w_gu, w_down, n_blocks, tf=512):
    T, D = tok.shape
    F = w_down.shape[1]
    nf = F // tf
    tm = MOE_ROWS
    grid_spec = pltpu.PrefetchScalarGridSpec(
        num_scalar_prefetch=4,
        grid=(n_blocks, nf),
        in_specs=[pl.BlockSpec((2, T), lambda b, f, *_: (0, 0)),
                  pl.BlockSpec(memory_space=pl.ANY),
                  pl.BlockSpec((1, D, tf), lambda b, f, be, *_: (be[b], 0, f)),
                  pl.BlockSpec((1, D, tf), lambda b, f, be, *_: (be[b], 0, f + nf)),
                  pl.BlockSpec((1, tf, D), lambda b, f, be, *_: (be[b], f, 0))],
        out_specs=pl.BlockSpec((tm, D), lambda b, f, *_: (b, 0)),
        scratch_shapes=[pltpu.VMEM((T, D), BF16), pltpu.VMEM((tm, D), BF16), pltpu.SemaphoreType.DMA(())],
    )
    return pl.pallas_call(
        _moe_ffn_kernel,
        grid_spec=grid_spec,
        out_shape=jax.ShapeDtypeStruct((n_blocks * tm, D), F32),
        compiler_params=_cparams("arbitrary", "arbitrary", vmem_limit=VMEM_LIMIT_MOE),
        name="moe_ffn",
    )(block_e, half_live, c_lo, c_hi, dest_t, tok, w_gu, w_gu, w_down)


def _fox_pre_kernel(h_ref, ng_ref, w_ref, wf_ref, bf_ref, qg_ref, kg_ref, sel_ref, selt_ref, pq_ref, pk_ref,
                    oneq_ref, onek_ref, q_ref, k_ref, v_ref, og_ref, c2_ref, augq_ref, augk_ref, carry_ref):
    i = pl.program_id(0)
    tm, D = h_ref.shape
    H = c2_ref.shape[1]

    @pl.when(i == 0)
    def _():
        carry_ref[...] = jnp.zeros_like(carry_ref)

    u = _rms(h_ref[...], ng_ref[...]).astype(BF16)

    def head_norm(x, gain):
        ms = _dot(x * x, sel_ref[...]) * (1.0 / HEAD_DIM)
        inv = lax.rsqrt(ms + RMS_EPS)
        inv_hi, inv_lo = _split_hi_lo(inv)
        first = lax.broadcasted_iota(jnp.int32, inv.shape, 1) < H
        wide = jnp.dot(jnp.where(first, inv_hi, inv_lo), selt_ref[...], preferred_element_type=F32)
        return x * wide * gain

    q = jnp.dot(u, w_ref[:, 0:D], preferred_element_type=F32)
    q_ref[...] = (head_norm(q, qg_ref[...]) * (HEAD_DIM ** -0.5 * LOG2E)).astype(BF16)
    k = jnp.dot(u, w_ref[:, D:2 * D], preferred_element_type=F32)
    k_ref[...] = head_norm(k, kg_ref[...]).astype(BF16)
    v_ref[...] = jnp.dot(u, w_ref[:, 2 * D:3 * D], preferred_element_type=F32).astype(BF16)
    og_ref[...] = _sigmoid(jnp.dot(u, w_ref[:, 3 * D:4 * D], preferred_element_type=F32)).astype(BF16)

    fl = jnp.dot(u, wf_ref[...], preferred_element_type=F32) + bf_ref[...]
    log_f = jnp.minimum(fl, 0.0) - jnp.log(1.0 + jnp.exp(-jnp.abs(fl)))
    row = lax.broadcasted_iota(jnp.int32, (tm, tm), 0)
    col = lax.broadcasted_iota(jnp.int32, (tm, tm), 1)
    tri = jnp.where(col <= row, 1.0, 0.0).astype(BF16)
    hi, lo = _split_hi_lo(log_f)
    lo2 = (log_f - hi.astype(F32) - lo.astype(F32)).astype(BF16)
    c = (jnp.dot(tri, hi, preferred_element_type=F32) + jnp.dot(tri, lo, preferred_element_type=F32)
         + jnp.dot(tri, lo2, preferred_element_type=F32)) + carry_ref[...]
    carry_ref[...] = c[tm - 1:tm, :]
    c2 = c * LOG2E
    c2_ref[...] = c2[:, 0:H]
    c_hi, c_lo = _split_hi_lo(c2)
    c_lo2 = (c2 - c_hi.astype(F32) - c_lo.astype(F32)).astype(BF16)
    group = lax.broadcasted_iota(jnp.int32, c2.shape, 1) // H
    pieces = jnp.where(group == 0, c_hi, jnp.where(group == 1, c_lo, c_lo2))
    augq_ref[...] = (jnp.dot(pieces, pq_ref[...], preferred_element_type=F32) + oneq_ref[...]).astype(BF16)
    augk_ref[...] = (jnp.dot(pieces, pk_ref[...], preferred_element_type=F32) + onek_ref[...]).astype(BF16)


def _fox_pre(h, ng, w_main, w_f, b_f, q_gain, k_gain, tm=256):
    T, D = h.shape
    H = D // HEAD_DIM
    sel = (jnp.arange(D)[:, None] // HEAD_DIM == jnp.arange(2 * H)[None, :] % H).astype(BF16)
    piece = jnp.arange(3 * H)[:, None] // H
    head = jnp.arange(3 * H)[:, None] % H
    lane = jnp.arange(D)[None, :]
    at = lambda j: lane == head * HEAD_DIM + j
    p_q = at(piece).astype(BF16)
    p_k = -at(piece + 3).astype(BF16)
    off = jnp.arange(D) % HEAD_DIM
    one_q = ((off >= 3) & (off < 6)).astype(F32).reshape(1, D)
    one_k = (off < 3).astype(F32).reshape(1, D)
    row = pl.BlockSpec((tm, D), lambda i: (i, 0))
    ins = [ng, w_main, jnp.tile(w_f, (1, 3)), jnp.tile(b_f, (1, 3)), q_gain, k_gain, sel, sel.T, p_q, p_k,
           one_q, one_k]
    return pl.pallas_call(
        _fox_pre_kernel,
        grid=(T // tm,),
        in_specs=[row] + [_full(a.shape) for a in ins],
        out_specs=[row] * 4 + [pl.BlockSpec((tm, H), lambda i: (i, 0))] + [row] * 2,
        out_shape=([jax.ShapeDtypeStruct((T, D), BF16)] * 4 + [jax.ShapeDtypeStruct((T, H), F32)]
                   + [jax.ShapeDtypeStruct((T, D), BF16)] * 2),
        scratch_shapes=[pltpu.VMEM((1, 3 * H), F32)],
        compiler_params=_cparams("arbitrary"),
        name="fox_pre",
    )(h, *ins)


def _fox_attn_kernel(jlo_ref, q_ref, augq_ref, k_ref, augk_ref, vt_ref, og_ref, o_ref, s_ref, *, blk):
    p_idx = pl.program_id(0)
    i = pl.program_id(1)
    head0 = lax.broadcasted_iota(jnp.int32, (1, PAIR), 1) < HEAD_DIM
    q, augq = q_ref[...], augq_ref[...]
    zero = jnp.zeros_like(q)
    q_heads = [jnp.concatenate([jnp.where(m, q, zero), jnp.where(m, augq, zero)], axis=1)
               for m in (head0, jnp.logical_not(head0))]
    causal = (lax.broadcasted_iota(jnp.int32, (blk, blk), 0)
              <= lax.broadcasted_iota(jnp.int32, (blk, blk), 1))

    def scores_into(j, slot):
        start = pl.multiple_of(j * blk, blk)
        kb = jnp.concatenate([k_ref[pl.ds(start, blk), :], augk_ref[pl.ds(start, blk), :]], axis=1)
        for h in range(2):
            s_ref[slot, h] = lax.dot_general(kb, q_heads[h], (((1,), (1,)), ((), ())),
                                             preferred_element_type=F32)

    def softmax_pv(j, slot, carry, masked):
        start = pl.multiple_of(j * blk, blk)
        out = []
        for h in range(2):
            m_prev, l_prev, acc_prev = carry[h]
            vtb = vt_ref[h * HEAD_DIM:(h + 1) * HEAD_DIM, pl.ds(start, blk)]
            s = s_ref[slot, h]
            if masked:
                s = jnp.where(causal, s, -jnp.inf)
            m_new = jnp.maximum(m_prev, jnp.max(s, axis=0, keepdims=True))
            alpha = jnp.exp2(m_prev - m_new)
            p = jnp.exp2(s - m_new)
            l_new = alpha * l_prev + jnp.sum(p, axis=0, keepdims=True)
            acc_new = alpha * acc_prev + jnp.dot(vtb, p.astype(BF16), preferred_element_type=F32)
            out.append((m_new, l_new, acc_new))
        return tuple(out)

    init = tuple((jnp.full((1, blk), -jnp.inf, F32), jnp.zeros((1, blk), F32), jnp.zeros((HEAD_DIM, blk), F32))
                 for _ in range(2))
    jlo = jlo_ref[p_idx * pl.num_programs(1) + i]
    total = i - jlo + 1
    scores_into(jlo, 0)

    def two_blocks(t, carry):
        j = jlo + 2 * t
        scores_into(j + 1, 1)
        carry = softmax_pv(j, 0, carry, False)
        scores_into(j + 2, 0)
        return softmax_pv(j + 1, 1, carry, False)

    res = lax.fori_loop(0, (total - 1) // 2, two_blocks, init)

    def last_one(carry):
        return softmax_pv(i, 0, carry, True)

    def last_two(carry):
        scores_into(i, 1)
        carry = softmax_pv(i - 1, 0, carry, False)
        return softmax_pv(i, 1, carry, True)

    res = lax.cond(total % 2 == 1, last_one, last_two, res)
    o_t = jnp.concatenate([res[0][2] / res[0][1], res[1][2] / res[1][1]], axis=0)
    o_ref[...] = (o_t.T * og_ref[...].astype(F32)).astype(o_ref.dtype)


def _fox_attn(jlo, q, augq, k, augk, v_t, og, blk):
    D, T = v_t.shape
    n_pairs = D // PAIR
    qspec = pl.BlockSpec((blk, PAIR), lambda p, i, jlo: (i, p))
    kspec = pl.BlockSpec((T, PAIR), lambda p, i, jlo: (0, p))
    grid_spec = pltpu.PrefetchScalarGridSpec(
        num_scalar_prefetch=1,
        grid=(n_pairs, T // blk),
        in_specs=[qspec, qspec, kspec, kspec,
                  pl.BlockSpec((PAIR, T), lambda p, i, jlo: (p, 0)),
                  qspec],
        out_specs=qspec,
        scratch_shapes=[pltpu.VMEM((2, 2, blk, blk), F32)],
    )
    return pl.pallas_call(
        functools.partial(_fox_attn_kernel, blk=blk),
        grid_spec=grid_spec,
        out_shape=jax.ShapeDtypeStruct((T, D), BF16),
        compiler_params=_cparams("parallel", "arbitrary"),
        name="fox_attn",
    )(jlo, q, augq, k, augk, v_t, og)


def _first_live_block(c2, q_gain, k_gain, blk):
    T, H = c2.shape
    bound = 8.0 * LOG2E * 1.02 * jnp.max(jnp.abs(q_gain)) * jnp.max(jnp.abs(k_gain))
    c_first = c2[0::blk]
    c_last = c2[blk - 1::blk]
    dead = c_last[None, :, :] > c_first[:, None, :] + (2.0 * bound + 104.0 * LOG2E + 1.0)
    jlo = jnp.sum(dead, axis=1).astype(jnp.int32)
    jlo = jnp.min(jlo.reshape(-1, H // 2, 2), axis=-1)
    return jlo.T.reshape(-1)


def _router_kernel(h_ref, ng_ref, wr_ref, u_ref, info_ref, cnt_ref, csr_ref, carry_ref):
    i = pl.program_id(0)
    tm = h_ref.shape[0]
    E = N_EXPERTS

    @pl.when(i == 0)
    def _():
        carry_ref[...] = jnp.zeros_like(carry_ref)

    csr_ref[0] = carry_ref[...]
    u = _rms(h_ref[...], ng_ref[...])
    u_ref[...] = u.astype(BF16)
    u_hi, u_lo = _split_hi_lo(u)
    w_hi, w_lo = _split_hi_lo(wr_ref[...])
    logits = (jnp.dot(u_hi, w_hi, preferred_element_type=F32)
              + jnp.dot(u_hi, w_lo, preferred_element_type=F32)
              + jnp.dot(u_lo, w_hi, preferred_element_type=F32))
    lane = lax.broadcasted_iota(jnp.int32, (tm, E), 1)
    v1 = jnp.max(logits, axis=-1, keepdims=True)
    e1 = jnp.min(jnp.where(logits == v1, lane, E), axis=-1, keepdims=True)
    rest = jnp.where(lane == e1, -jnp.inf, logits)
    v2 = jnp.max(rest, axis=-1, keepdims=True)
    e2 = jnp.min(jnp.where(rest == v2, lane, E), axis=-1, keepdims=True)
    g1 = 1.0 / (1.0 + jnp.exp(v2 - v1))
    g2 = 1.0 - g1
    oh1 = lane == e1
    oh2 = lane == e2
    onehot = jnp.where(oh1 | oh2, 1.0, 0.0).astype(BF16)
    row = lax.broadcasted_iota(jnp.int32, (tm, tm), 0)
    col = lax.broadcasted_iota(jnp.int32, (tm, tm), 1)
    tri_strict = jnp.where(col < row, 1.0, 0.0).astype(BF16)
    before = jnp.dot(tri_strict, onehot, preferred_element_type=F32) + carry_ref[...]
    rank1 = jnp.sum(jnp.where(oh1, before, 0.0), axis=-1, keepdims=True)
    rank2 = jnp.sum(jnp.where(oh2, before, 0.0), axis=-1, keepdims=True)
    info = jnp.where(lane == 0, e1.astype(F32), 0.0)
    info = jnp.where(lane == 1, e2.astype(F32), info)
    info = jnp.where(lane == 2, g1, info)
    info = jnp.where(lane == 3, g2, info)
    info = jnp.where(lane == 4, rank1, info)
    info = jnp.where(lane == 5, rank2, info)
    info_ref[...] = info
    total = carry_ref[...] + jnp.sum(onehot.astype(F32), axis=0, keepdims=True)
    carry_ref[...] = total
    cnt_ref[...] = total


def _router(h, ng, w_router):
    T, D = h.shape
    E = N_EXPERTS
    tm = min(MOE_CHUNK, T)
    row = pl.BlockSpec((tm, D), lambda i: (i, 0))
    return pl.pallas_call(
        _router_kernel,
        grid=(T // tm,),
        in_specs=[row, _full((1, D)), _full((D, E))],
        out_specs=[row, pl.BlockSpec((tm, E), lambda i: (i, 0)), _full((1, E)),
                   pl.BlockSpec((1, 1, E), lambda i: (i, 0, 0))],
        out_shape=[jax.ShapeDtypeStruct((T, D), BF16), jax.ShapeDtypeStruct((T, E), F32),
                   jax.ShapeDtypeStruct((1, E), F32), jax.ShapeDtypeStruct((T // tm, 1, E), F32)],
        scratch_shapes=[pltpu.VMEM((1, E), F32)],
        compiler_params=_cparams("arbitrary"),
        name="router",
    )(h, ng, w_router)


def _combine_kernel(pos_ref, h_ref, gate_ref, fg_ref, y_ref, o_ref, ya_ref, yb_ref, sem):
    b = pl.program_id(0)
    nb = pl.num_programs(0)
    rows = h_ref.shape[0]

    def copies(blk, r):
        slot = blk % 2
        base = 2 * (blk * rows + r)
        return (pltpu.make_async_copy(y_ref.at[pl.ds(pos_ref[base], 1)], ya_ref.at[slot, pl.ds(r, 1)],
                                      sem.at[slot]),
                pltpu.make_async_copy(y_ref.at[pl.ds(pos_ref[base + 1], 1)], yb_ref.at[slot, pl.ds(r, 1)],
                                      sem.at[slot]))

    def issue(blk):
        def body(r, _):
            ca, cb = copies(blk, r)
            ca.start()
            cb.start()
            return 0
        lax.fori_loop(0, rows, body, 0, unroll=8)

    @pl.when(b == 0)
    def _():
        issue(0)

    @pl.when(b + 1 < nb)
    def _():
        issue(b + 1)

    def wait(r, _):
        ca, cb = copies(b, r)
        ca.wait()
        cb.wait()
        return 0

    lax.fori_loop(0, rows, wait, 0, unroll=8)
    slot = b % 2
    gate = gate_ref[...]
    moe = gate[:, 0:1] * ya_ref[slot] + gate[:, 1:2] * yb_ref[slot]
    o_ref[...] = _rms(h_ref[...] + moe, fg_ref[...])


def _combine(pos, h, gate, final_g, y, rows=256):
    T, D = h.shape
    row = pl.BlockSpec((rows, D), lambda b, pos: (b, 0))
    grid_spec = pltpu.PrefetchScalarGridSpec(
        num_scalar_prefetch=1,
        grid=(T // rows,),
        in_specs=[row, pl.BlockSpec((rows, 2), lambda b, pos: (b, 0)),
                  pl.BlockSpec((1, D), lambda b, pos: (0, 0)), pl.BlockSpec(memory_space=pl.ANY)],
        out_specs=row,
        scratch_shapes=[pltpu.VMEM((2, rows, D), F32), pltpu.VMEM((2, rows, D), F32),
                        pltpu.SemaphoreType.DMA((2,))],
    )
    return pl.pallas_call(
        _combine_kernel,
        grid_spec=grid_spec,
        out_shape=jax.ShapeDtypeStruct((T, D), F32),
        compiler_params=_cparams("arbitrary"),
        name="moe_combine",
    )(pos, h, gate, final_g, y)


def _rwkv_layer(h, ng, mu, w_rkv, w0, w1, w2, a0, a1, a2, g1, g2, k_k, k_a, r_k, gn_g, gn_b, w_o):
    D = h.shape[1]
    vec = lambda z: z.reshape(1, D)
    b16 = lambda z: z.astype(BF16)
    r, k, v, lw, a, g = _rwkv_pre(h, vec(ng), mu, b16(w_rkv), vec(w0), b16(w1), b16(w2), vec(a0),
                                  b16(a1), b16(a2), b16(g1), b16(g2))
    o = _rwkv_scan(r, k, v, lw, a, g, vec(k_k), vec(k_a), vec(r_k), vec(gn_g), vec(gn_b))
    return _mm_res(o, b16(w_o), h)


def _fox_layer(h, ng, w_in, b_f, q_gain, k_gain, w_o):
    T, D = h.shape
    H = D // HEAD_DIM
    order = jnp.argsort(b_f)
    w_main = w_in[:, :4 * D].reshape(D, 4, H, HEAD_DIM)[:, :, order, :].reshape(D, 4 * D)
    w_out = w_o.reshape(H, HEAD_DIM, D)[order].reshape(D, D)
    q, k, v, og, c2, augq, augk = _fox_pre(
        h, ng.reshape(1, D), w_main.astype(BF16), w_in[:, 4 * D:][:, order].astype(BF16),
        b_f[order].reshape(1, H), jnp.tile(q_gain, H).reshape(1, D), jnp.tile(k_gain, H).reshape(1, D))
    blk = min(ATTN_BLOCK, T)
    jlo = _first_live_block(c2, q_gain, k_gain, blk)
    o = _fox_attn(jlo, q, augq, k, augk, v.T, og, blk)
    return _mm_res(o, w_out.astype(BF16), h)


def _moe_layer(h, ng, final_g, w_router, w_gu, w_down):
    T, D = h.shape
    E, RB = N_EXPERTS, MOE_ROWS
    u, info, cnt, csr = _router(h, ng.reshape(1, D), w_router)
    e = info[:, 0:2].astype(jnp.int32)
    gate = info[:, 2:4]
    rank = info[:, 4:6].astype(jnp.int32)
    counts = cnt[0].astype(jnp.int32)
    padded = (counts + RB - 1) // RB * RB
    ends = jnp.cumsum(padded)
    starts = ends - padded
    n_blocks = (2 * T) // RB + E
    dest = starts[e] + rank
    blocks = jnp.arange(n_blocks, dtype=jnp.int32)
    block_e = jnp.minimum(jnp.sum(blocks[:, None] * RB >= ends[None, :], axis=1), E - 1).astype(jnp.int32)

    def first_rank(size):
        groups = jnp.arange(n_blocks * (RB // size), dtype=jnp.int32)
        group_e = jnp.repeat(block_e, RB // size)
        rank0 = groups * size - starts[group_e]
        return group_e, rank0, rank0 < counts[group_e]

    _, _, half_live = first_rank(MOE_HALF)
    group_e, rank0, group_live = first_rank(MOE_SUB)
    csr_e = csr[:, 0, :].astype(jnp.int32)[:, group_e]
    n_chunks = csr_e.shape[0]
    c_lo = jnp.clip(jnp.sum(csr_e <= rank0[None, :], axis=0) - 1, 0, n_chunks - 1)
    c_hi = jnp.clip(jnp.sum(csr_e < rank0[None, :] + MOE_SUB, axis=0) - 1, 0, n_chunks - 1)
    c_lo = jnp.where(group_live, c_lo, 1).astype(jnp.int32)
    c_hi = jnp.where(group_live, c_hi, 0).astype(jnp.int32)
    y = _moe_ffn(block_e, half_live.astype(jnp.int32), c_lo, c_hi, dest.T, u, w_gu, w_down, n_blocks)
    return _combine(dest.reshape(-1), h, gate, final_g.reshape(1, D), y)


def kernel(x, norm_g, final_g, rwkv_mu, rwkv_w_rkv, rwkv_w0, rwkv_w1, rwkv_w2, rwkv_a0, rwkv_a1,
           rwkv_a2, rwkv_g1, rwkv_g2, rwkv_k_k, rwkv_k_a, rwkv_r_k, rwkv_gn_g, rwkv_gn_b, rwkv_w_o,
           fox_w_in, fox_b_f, fox_q_gain, fox_k_gain, fox_w_o, ffn_w_gu, ffn_w_down,
           moe_w_router, moe_w_gu, moe_w_down):
    B, T, D = x.shape
    h = x.reshape(B * T, D)
    h = _rwkv_layer(h, norm_g[0, 0], rwkv_mu[0], rwkv_w_rkv[0], rwkv_w0[0], rwkv_w1[0], rwkv_w2[0],
                    rwkv_a0[0], rwkv_a1[0], rwkv_a2[0], rwkv_g1[0], rwkv_g2[0], rwkv_k_k[0],
                    rwkv_k_a[0], rwkv_r_k[0], rwkv_gn_g[0], rwkv_gn_b[0], rwkv_w_o[0])
    h = _ffn(h, norm_g[0, 1].reshape(1, D), ffn_w_gu[0].astype(BF16), ffn_w_down[0].astype(BF16))
    h = _fox_layer(h, norm_g[1, 0], fox_w_in[0], fox_b_f[0], fox_q_gain[0], fox_k_gain[0], fox_w_o[0])
    out = _moe_layer(h, norm_g[1, 1], final_g, moe_w_router[0], moe_w_gu[0], moe_w_down[0])
    return out.reshape(B, T, D)
```

```python
import functools

import jax
import jax.numpy as jnp
from jax import lax
from jax.experimental import pallas as pl
from jax.experimental.pallas import tpu as pltpu

F32 = jnp.float32
BF16 = jnp.bfloat16

HEAD_DIM = 64
PAIR = 2 * HEAD_DIM
GN_EPS = 64e-5
RMS_EPS = 1e-6
N_EXPERTS = 8
SCAN_CHUNK = 64
SCAN_CHUNKS_PER_STEP = 2
MOE_ROWS = 1024
MOE_HALF = 512
MOE_SUB = 128
MOE_CHUNK = 512
ATTN_BLOCK = 512
V_ROWS = 80
LOG2E = 1.4426950408889634
VMEM_LIMIT = 56 * 1024 * 1024
VMEM_LIMIT_MOE = 60 * 1024 * 1024


def _cparams(*sem, vmem_limit=VMEM_LIMIT):
    return pltpu.CompilerParams(dimension_semantics=sem, vmem_limit_bytes=vmem_limit)


def _dot(a, b):
    return jnp.dot(a.astype(BF16), b.astype(BF16), preferred_element_type=F32)


def _dot_nt(a, b):
    return lax.dot_general(a.astype(BF16), b.astype(BF16), (((1,), (1,)), ((), ())),
                           preferred_element_type=F32)


def _dot_tn(a, b):
    return lax.dot_general(a.astype(BF16), b.astype(BF16), (((0,), (0,)), ((), ())),
                           preferred_element_type=F32)


def _split_hi_lo(x):
    hi = x.astype(BF16)
    lo = (x - hi.astype(F32)).astype(BF16)
    return hi, lo


def _rms(x, g):
    ms = jnp.mean(x * x, axis=-1, keepdims=True)
    return x * lax.rsqrt(ms + RMS_EPS) * g


def _sigmoid(x):
    return 1.0 / (1.0 + jnp.exp(-x))


def _full(shape):
    return pl.BlockSpec(shape, lambda *_: (0,) * len(shape))


def _rwkv_pre_kernel(h_ref, hp_ref, ng_ref, mu_ref, wrkv_ref, w0_ref, w1_ref, w2_ref, a0_ref,
                     a1_ref, a2_ref, g1_ref, g2_ref, r_ref, k_ref, v_ref, lw_ref, a_ref, g_ref):
    i = pl.program_id(0)
    ng = ng_ref[...]
    u = _rms(h_ref[...], ng)
    up = _rms(hp_ref[7:8, :], ng)
    up = jnp.where(i == 0, 0.0, up)
    row = lax.broadcasted_iota(jnp.int32, u.shape, 0)
    u_prev = jnp.where(row == 0, up, pltpu.roll(u, 1, axis=0))
    dx = u_prev - u
    mix = lambda n: u + dx * mu_ref[n:n + 1, :]
    r_ref[...] = _dot(mix(0), wrkv_ref[0])
    k_ref[...] = _dot(mix(1), wrkv_ref[1])
    v_ref[...] = _dot(mix(2), wrkv_ref[2])
    z = w0_ref[...] + _dot(jnp.tanh(_dot(mix(3), w1_ref[...])), w2_ref[...])
    softplus_neg = jnp.maximum(-z, 0.0) + jnp.log(1.0 + jnp.exp(-jnp.abs(z)))
    lw_ref[...] = -jnp.exp(-softplus_neg - 0.5)
    a_ref[...] = _sigmoid(a0_ref[...] + _dot(_dot(mix(4), a1_ref[...]), a2_ref[...]))
    g_ref[...] = _dot(_sigmoid(_dot(mix(5), g1_ref[...])), g2_ref[...])


def _rwkv_pre(h, ng, mu, wrkv, w0, w1, w2, a0, a1, a2, g1, g2, tm=256):
    T, D = h.shape
    row = pl.BlockSpec((tm, D), lambda i: (i, 0))
    prev = pl.BlockSpec((8, D), lambda i: (jnp.maximum(i * (tm // 8) - 1, 0), 0))
    ins = [ng, mu, wrkv, w0, w1, w2, a0, a1, a2, g1, g2]
    return pl.pallas_call(
        _rwkv_pre_kernel,
        grid=(T // tm,),
        in_specs=[row, prev] + [_full(a.shape) for a in ins],
        out_specs=[row] * 6,
        out_shape=[jax.ShapeDtypeStruct((T, D), F32)] * 6,
        compiler_params=_cparams("parallel"),
        name="rwkv_pre",
    )(h, h, *ins)


def _scan_kernel(r_ref, k_ref, v_ref, lw_ref, a_ref, g_ref, kk_ref, ka_ref, rk_ref, gng_ref,
                 gnb_ref, o_ref, s_ref):
    L = SCAN_CHUNK
    n_pairs = r_ref.shape[1] // PAIR

    @pl.when(pl.program_id(0) == 0)
    def _():
        s_ref[...] = jnp.zeros_like(s_ref)

    row = lax.broadcasted_iota(jnp.int32, (L, L), 0)
    col = lax.broadcasted_iota(jnp.int32, (L, L), 1)
    tri_incl = jnp.where(col <= row, 1.0, 0.0).astype(BF16)
    eye = jnp.where(col == row, 1.0, 0.0).astype(F32)
    lane = lax.broadcasted_iota(jnp.int32, (1, PAIR), 1)
    head0 = lane < HEAD_DIM
    arow = lax.broadcasted_iota(jnp.int32, (4 * L, PAIR), 0)
    acol = lax.broadcasted_iota(jnp.int32, (4 * L, PAIR), 1)
    t_idx = arow % L + jnp.where(arow < 2 * L, 0, 1)
    j_idx = acol % L
    a_mask = j_idx < t_idx
    brow = lax.broadcasted_iota(jnp.int32, (PAIR, PAIR), 0)
    bcol = lax.broadcasted_iota(jnp.int32, (PAIR, PAIR), 1)
    blockdiag = (brow < HEAD_DIM) == (bcol < HEAD_DIM)

    def head_sum(x):
        s0 = jnp.sum(jnp.where(head0, x, 0.0), axis=-1, keepdims=True)
        s1 = jnp.sum(jnp.where(head0, 0.0, x), axis=-1, keepdims=True)
        return jnp.where(head0, s0, s1)

    pairs = range(n_pairs)
    heads = [(p, h) for p in pairs for h in range(2)]
    sls = [slice(p * PAIR, (p + 1) * PAIR) for p in pairs]
    n_chunks = r_ref.shape[0] // L
    chunks = range(n_chunks)
    rows = [slice(c * L, (c + 1) * L) for c in chunks]
    cp = [(c, p) for c in chunks for p in pairs]
    cph = [(c, p, h) for c in chunks for p, h in heads]

    lws = {(c, p): lw_ref[rows[c], sls[p]] for c, p in cp}
    splits = {k: _split_hi_lo(lw) for k, lw in lws.items()}
    css = {k: jnp.dot(tri_incl, hi, preferred_element_type=F32) + jnp.dot(tri_incl, lo, preferred_element_type=F32)
           for k, (hi, lo) in splits.items()}
    prep = {}
    for c, p in cp:
        rw, sl, cs, lw = rows[c], sls[p], css[c, p], lws[c, p]
        w_inv = jnp.exp(-cs)
        w_last = jnp.exp(cs[L - 1:L, :])
        r = r_ref[rw, sl]
        k = k_ref[rw, sl]
        a = a_ref[rw, sl]
        kk = k * kk_ref[:, sl]
        kk = kk / jnp.maximum(jnp.sqrt(head_sum(kk * kk)), 1e-12)
        kmod = k * (1.0 + (a - 1.0) * ka_ref[:, sl])
        at = -kk * jnp.exp(cs - lw)
        bt = kk * a * w_inv
        kt = kmod * w_inv
        rt = r * jnp.exp(cs)
        v = v_ref[rw, sl]
        prep[c, p] = dict(atrt=jnp.concatenate([at, rt], axis=0).astype(BF16),
                          x_stack=jnp.concatenate([jnp.where(head0, at, 0.0), jnp.where(head0, 0.0, at),
                                                   jnp.where(head0, rt, 0.0), jnp.where(head0, 0.0, rt)],
                                                  axis=0).astype(BF16),
                          bk=jnp.concatenate([bt, kt], axis=0).astype(BF16),
                          bkhat=jnp.concatenate([bt * w_last, kt * w_last], axis=0).astype(BF16),
                          w_last=w_last, v=v, v_bf=v.astype(BF16), rk=r * kmod)
    aas = {k: jnp.where(a_mask, _dot_nt(prep[k]["x_stack"], prep[k]["bk"]), 0.0) for k in cp}
    akv = {}
    pw = {}
    t_inv = {}
    for c, p, h in cph:
        a_rows = aas[c, p][h * L:(h + 1) * L, :]
        vv = jnp.concatenate([prep[c, p]["v_bf"], prep[c, p]["v_bf"]], axis=0)
        akv[c, p, h] = _dot(jnp.where(head0, 0.0, a_rows), vv)
        pw[c, p, h] = a_rows[:, :L]
        t_inv[c, p, h] = eye + pw[c, p, h]
    for _ in range(5):
        for k in cph:
            pw[k] = _dot(pw[k], pw[k])
        for k in cph:
            t_inv[k] = t_inv[k] + _dot(t_inv[k], pw[k])

    states = [s_ref[p] for p in pairs]
    for c in chunks:
        inter = [_dot_nt(prep[c, p]["atrt"], states[p]) for p in pairs]
        u_h = {(p, h): _dot(t_inv[c, p, h], inter[p][:L, :] + akv[c, p, h]) for p, h in heads}
        y_h = {}
        for p, h in heads:
            ar = aas[c, p][(2 + h) * L:(3 + h) * L, :]
            y_h[p, h] = _dot(ar, jnp.concatenate([u_h[p, h].astype(BF16), prep[c, p]["v_bf"]], axis=0))
        new_states = []
        for p in pairs:
            pr = prep[c, p]
            u = jnp.where(head0, u_h[p, 0], u_h[p, 1])
            uv = jnp.concatenate([u.astype(BF16), pr["v_bf"]], axis=0)
            new_states.append(states[p] * pr["w_last"] + jnp.where(blockdiag, _dot_tn(uv, pr["bkhat"]), 0.0))
        states = new_states
        for p in pairs:
            sl = sls[p]
            pr = prep[c, p]
            y = inter[p][L:, :] + jnp.where(head0, y_h[p, 0], y_h[p, 1])
            mean = head_sum(y) * (1.0 / HEAD_DIM)
            d = y - mean
            var = head_sum(d * d) * (1.0 / HEAD_DIM)
            yn = d * lax.rsqrt(var + GN_EPS) * gng_ref[:, sl] + gnb_ref[:, sl]
            bonus = head_sum(pr["rk"] * rk_ref[:, sl]) * pr["v"]
            o_ref[rows[c], sl] = ((yn + bonus) * g_ref[rows[c], sl]).astype(o_ref.dtype)
    for p in pairs:
        s_ref[p] = states[p]


def _rwkv_scan(r, k, v, lw, a, g, k_k, k_a, r_k, gn_g, gn_b):
    T, D = r.shape
    rows = SCAN_CHUNK * SCAN_CHUNKS_PER_STEP
    row = pl.BlockSpec((rows, D), lambda c: (c, 0))
    par = _full((1, D))
    return pl.pallas_call(
        _scan_kernel,
        grid=(T // rows,),
        in_specs=[row] * 6 + [par] * 5,
        out_specs=row,
        out_shape=jax.ShapeDtypeStruct((T, D), BF16),
        scratch_shapes=[pltpu.VMEM((D // PAIR, PAIR, PAIR), F32)],
        compiler_params=_cparams("arbitrary"),
        name="rwkv_scan",
    )(r, k, v, lw, a, g, k_k, k_a, r_k, gn_g, gn_b)


def _swiglu_step(u_ref, wg_ref, wu_ref, wd_ref, acc_ref):
    u = u_ref[...]
    gate = jnp.dot(u, wg_ref[...].astype(BF16), preferred_element_type=F32)
    up = jnp.dot(u, wu_ref[...].astype(BF16), preferred_element_type=F32)
    hid = (gate * _sigmoid(gate) * up).astype(BF16)
    acc_ref[...] += jnp.dot(hid, wd_ref[...].astype(BF16), preferred_element_type=F32)


def _ffn_kernel(h_ref, x_ref, wo_ref, ng_ref, wg_ref, wu_ref, wd_ref, o_ref, u_ref):
    @pl.when(pl.program_id(1) == 0)
    def _():
        h1 = h_ref[...] + jnp.dot(x_ref[...], wo_ref[...], preferred_element_type=F32)
        o_ref[...] = h1
        u_ref[...] = _rms(h1, ng_ref[...]).astype(BF16)

    _swiglu_step(u_ref, wg_ref, wu_ref, wd_ref, o_ref)


def _ffn(h, x, w_o, ng, w_gu, w_down, tm=1024, tf=512):
    T, D = h.shape
    F = w_down.shape[0]
    nf = F // tf
    row = pl.BlockSpec((tm, D), lambda i, f: (i, 0))
    return pl.pallas_call(
        _ffn_kernel,
        grid=(T // tm, nf),
        in_specs=[row, row, _full((D, D)), _full((1, D)),
                  pl.BlockSpec((D, tf), lambda i, f: (0, f)),
                  pl.BlockSpec((D, tf), lambda i, f: (0, f + nf)),
                  pl.BlockSpec((tf, D), lambda i, f: (f, 0))],
        out_specs=row,
        out_shape=jax.ShapeDtypeStruct((T, D), F32),
        scratch_shapes=[pltpu.VMEM((tm, D), BF16)],
        compiler_params=_cparams("parallel", "arbitrary"),
        name="ffn",
    )(h, x, w_o, ng, w_gu, w_gu, w_down)


def _moe_ffn_kernel(be_ref, live_ref, clo_ref, chi_ref, dest_ref, tok_hbm, wg_ref, wu_ref, wd_ref, o_ref,
                    tok_ref, x_ref, sem):
    b = pl.program_id(0)
    f = pl.program_id(1)
    rows = o_ref.shape[0]
    ch = MOE_CHUNK
    halves = rows // MOE_HALF

    @pl.when((b == 0) & (f == 0))
    def _():
        cp = pltpu.make_async_copy(tok_hbm, tok_ref, sem)
        cp.start()
        cp.wait()

    @pl.when(f == 0)
    def _():
        o_ref[...] = jnp.zeros_like(o_ref)
        for s in range(rows // MOE_SUB):
            sub = slice(s * MOE_SUB, (s + 1) * MOE_SUB)
            row_id = b * rows + s * MOE_SUB + lax.broadcasted_iota(jnp.int32, (MOE_SUB, 1), 0)
            w = b * (rows // MOE_SUB) + s

            def chunk(c, _, sub=sub, row_id=row_id):
                start = pl.multiple_of(c * ch, ch)
                d = dest_ref[:, pl.ds(start, ch)]
                onehot = (jnp.where(row_id == d[0:1, :], 1.0, 0.0)
                          + jnp.where(row_id == d[1:2, :], 1.0, 0.0)).astype(BF16)
                o_ref[sub, :] += jnp.dot(onehot, tok_ref[pl.ds(start, ch), :], preferred_element_type=F32)
                return 0

            lax.fori_loop(clo_ref[w], chi_ref[w] + 1, chunk, 0)
        x_ref[...] = o_ref[...].astype(BF16)
        o_ref[...] = jnp.zeros_like(o_ref)

    for half in range(halves):
        @pl.when(live_ref[b * halves + half] > 0)
        def _(half=half):
            part = pl.ds(half * MOE_HALF, MOE_HALF)
            _swiglu_step(x_ref.at[part], wg_ref.at[0], wu_ref.at[0], wd_ref.at[0], o_ref.at[part])


def _moe_ffn(block_e, half_live, c_lo, c_hi, dest_t, tok, w_gu, w_down, n_blocks, tf=512):
    T, D = tok.shape
    F = w_down.shape[1]
    nf = F // tf
    tm = MOE_ROWS

    def fstep(b, f, live):
        return jnp.where(live[b * (tm // MOE_HALF)] > 0, f, nf - 1)

    grid_spec = pltpu.PrefetchScalarGridSpec(
        num_scalar_prefetch=4,
        grid=(n_blocks, nf),
        in_specs=[pl.BlockSpec((2, T), lambda b, f, *_: (0, 0)),
                  pl.BlockSpec(memory_space=pl.ANY),
                  pl.BlockSpec((1, D, tf), lambda b, f, be, live, *_: (be[b], 0, fstep(b, f, live))),
                  pl.BlockSpec((1, D, tf), lambda b, f, be, live, *_: (be[b], 0, fstep(b, f, live) + nf)),
                  pl.BlockSpec((1, tf, D), lambda b, f, be, live, *_: (be[b], fstep(b, f, live), 0))],
        out_specs=pl.BlockSpec((tm, D), lambda b, f, *_: (b, 0)),
        scratch_shapes=[pltpu.VMEM((T, D), BF16), pltpu.VMEM((tm, D), BF16), pltpu.SemaphoreType.DMA(())],
    )
    return pl.pallas_call(
        _moe_ffn_kernel,
        grid_spec=grid_spec,
        out_shape=jax.ShapeDtypeStruct((n_blocks * tm, D), F32),
        compiler_params=_cparams("arbitrary", "arbitrary", vmem_limit=VMEM_LIMIT_MOE),
        name="moe_ffn",
    )(block_e, half_live, c_lo, c_hi, dest_t, tok, w_gu, w_gu, w_down)


def _fox_pre_kernel(h_ref, ng_ref, w_ref, wf_ref, bf_ref, qg_ref, kg_ref, sel_ref, selt_ref, pq_ref, pk_ref,
                    oneq_ref, onek_ref, q_ref, k_ref, v_ref, og_ref, c2_ref, augq_ref, augk_ref, carry_ref):
    i = pl.program_id(0)
    tm, D = h_ref.shape
    H = c2_ref.shape[1]

    @pl.when(i == 0)
    def _():
        carry_ref[...] = jnp.zeros_like(carry_ref)

    u = _rms(h_ref[...], ng_ref[...]).astype(BF16)

    def head_norm(x, gain):
        ms = _dot(x * x, sel_ref[...]) * (1.0 / HEAD_DIM)
        inv = lax.rsqrt(ms + RMS_EPS)
        inv_hi, inv_lo = _split_hi_lo(inv)
        first = lax.broadcasted_iota(jnp.int32, inv.shape, 1) < H
        wide = jnp.dot(jnp.where(first, inv_hi, inv_lo), selt_ref[...], preferred_element_type=F32)
        return x * wide * gain

    q = jnp.dot(u, w_ref[:, 0:D], preferred_element_type=F32)
    q_ref[...] = (head_norm(q, qg_ref[...]) * (HEAD_DIM ** -0.5 * LOG2E)).astype(BF16)
    k = jnp.dot(u, w_ref[:, D:2 * D], preferred_element_type=F32)
    k_ref[...] = head_norm(k, kg_ref[...]).astype(BF16)
    v_ref[...] = jnp.dot(u, w_ref[:, 2 * D:3 * D], preferred_element_type=F32).astype(BF16)
    og_ref[...] = _sigmoid(jnp.dot(u, w_ref[:, 3 * D:4 * D], preferred_element_type=F32)).astype(BF16)

    fl = jnp.dot(u, wf_ref[...], preferred_element_type=F32) + bf_ref[...]
    log_f = jnp.minimum(fl, 0.0) - jnp.log(1.0 + jnp.exp(-jnp.abs(fl)))
    row = lax.broadcasted_iota(jnp.int32, (tm, tm), 0)
    col = lax.broadcasted_iota(jnp.int32, (tm, tm), 1)
    tri = jnp.where(col <= row, 1.0, 0.0).astype(BF16)
    hi, lo = _split_hi_lo(log_f)
    lo2 = (log_f - hi.astype(F32) - lo.astype(F32)).astype(BF16)
    c = (jnp.dot(tri, hi, preferred_element_type=F32) + jnp.dot(tri, lo, preferred_element_type=F32)
         + jnp.dot(tri, lo2, preferred_element_type=F32)) + carry_ref[...]
    carry_ref[...] = c[tm - 1:tm, :]
    c2 = c * LOG2E
    c2_ref[...] = c2[:, 0:H]
    c_hi, c_lo = _split_hi_lo(c2)
    c_lo2 = (c2 - c_hi.astype(F32) - c_lo.astype(F32)).astype(BF16)
    group = lax.broadcasted_iota(jnp.int32, c2.shape, 1) // H
    pieces = jnp.where(group == 0, c_hi, jnp.where(group == 1, c_lo, c_lo2))
    augq_ref[...] = (jnp.dot(pieces, pq_ref[...], preferred_element_type=F32) + oneq_ref[...]).astype(BF16)
    augk_ref[...] = (jnp.dot(pieces, pk_ref[...], preferred_element_type=F32) + onek_ref[...]).astype(BF16)


def _fox_pre(h, ng, w_main, w_f, b_f, q_gain, k_gain, tm=256):
    T, D = h.shape
    H = D // HEAD_DIM
    sel = (jnp.arange(D)[:, None] // HEAD_DIM == jnp.arange(2 * H)[None, :] % H).astype(BF16)
    piece = jnp.arange(3 * H)[:, None] // H
    head = jnp.arange(3 * H)[:, None] % H
    lane = jnp.arange(D)[None, :]
    at = lambda j: lane == head * HEAD_DIM + j
    p_q = at(piece).astype(BF16)
    p_k = -at(piece + 3).astype(BF16)
    off = jnp.arange(D) % HEAD_DIM
    one_q = ((off >= 3) & (off < 6)).astype(F32).reshape(1, D)
    one_k = (off < 3).astype(F32).reshape(1, D)
    row = pl.BlockSpec((tm, D), lambda i: (i, 0))
    ins = [ng, w_main, jnp.tile(w_f, (1, 3)), jnp.tile(b_f, (1, 3)), q_gain, k_gain, sel, sel.T, p_q, p_k,
           one_q, one_k]
    return pl.pallas_call(
        _fox_pre_kernel,
        grid=(T // tm,),
        in_specs=[row] + [_full(a.shape) for a in ins],
        out_specs=[row] * 4 + [pl.BlockSpec((tm, H), lambda i: (i, 0))] + [row] * 2,
        out_shape=([jax.ShapeDtypeStruct((T, D), BF16)] * 4 + [jax.ShapeDtypeStruct((T, H), F32)]
                   + [jax.ShapeDtypeStruct((T, D), BF16)] * 2),
        scratch_shapes=[pltpu.VMEM((1, 3 * H), F32)],
        compiler_params=_cparams("arbitrary"),
        name="fox_pre",
    )(h, *ins)


def _fox_attn_kernel(jlo_ref, q_ref, augq_ref, k_ref, augk_ref, vt_ref, og_ref, o_ref, s_ref, *, blk):
    p_idx = pl.program_id(0)
    i = pl.program_id(1)
    head0 = lax.broadcasted_iota(jnp.int32, (1, PAIR), 1) < HEAD_DIM
    q, augq = q_ref[...], augq_ref[...]
    zero = jnp.zeros_like(q)
    q_heads = [jnp.concatenate([jnp.where(m, q, zero), jnp.where(m, augq, zero)], axis=1)
               for m in (head0, jnp.logical_not(head0))]
    causal = (lax.broadcasted_iota(jnp.int32, (blk, blk), 0)
              <= lax.broadcasted_iota(jnp.int32, (blk, blk), 1))

    def scores_into(j, slot):
        start = pl.multiple_of(j * blk, blk)
        kb = jnp.concatenate([k_ref[pl.ds(start, blk), :], augk_ref[pl.ds(start, blk), :]], axis=1)
        for h in range(2):
            s_ref[slot, h] = lax.dot_general(kb, q_heads[h], (((1,), (1,)), ((), ())),
                                             preferred_element_type=F32)

    def softmax_pv(j, slot, carry, masked):
        start = pl.multiple_of(j * blk, blk)
        out = []
        for h in range(2):
            m_prev, acc_prev = carry[h]
            vtb = vt_ref[h * V_ROWS:(h + 1) * V_ROWS, pl.ds(start, blk)]
            s = s_ref[slot, h]
            if masked:
                s = jnp.where(causal, s, -jnp.inf)
            m_new = jnp.maximum(m_prev, jnp.max(s, axis=0, keepdims=True))
            alpha = jnp.exp2(m_prev - m_new)
            p = jnp.exp2(s - m_new)
            acc_new = alpha * acc_prev + jnp.dot(vtb, p.astype(BF16), preferred_element_type=F32)
            out.append((m_new, acc_new))
        return tuple(out)

    init = tuple((jnp.full((1, blk), -jnp.inf, F32), jnp.zeros((V_ROWS, blk), F32)) for _ in range(2))
    jlo = jlo_ref[p_idx * pl.num_programs(1) + i]
    total = i - jlo + 1
    scores_into(jlo, 0)

    def two_blocks(t, carry):
        j = jlo + 2 * t
        scores_into(j + 1, 1)
        carry = softmax_pv(j, 0, carry, False)
        scores_into(j + 2, 0)
        return softmax_pv(j + 1, 1, carry, False)

    res = lax.fori_loop(0, (total - 1) // 2, two_blocks, init)

    def last_one(carry):
        return softmax_pv(i, 0, carry, True)

    def last_two(carry):
        scores_into(i, 1)
        carry = softmax_pv(i - 1, 0, carry, False)
        return softmax_pv(i, 1, carry, True)

    res = lax.cond(total % 2 == 1, last_one, last_two, res)
    o_t = jnp.concatenate([acc[:HEAD_DIM] / acc[HEAD_DIM:HEAD_DIM + 1] for _, acc in res], axis=0)
    o_ref[...] = (o_t.T * og_ref[...].astype(F32)).astype(o_ref.dtype)


def _fox_attn(jlo, q, augq, k, augk, v_t, og, blk):
    T, D = q.shape
    n_pairs = D // PAIR
    qspec = pl.BlockSpec((blk, PAIR), lambda p, i, jlo: (i, p))
    kspec = pl.BlockSpec((T, PAIR), lambda p, i, jlo: (0, p))
    grid_spec = pltpu.PrefetchScalarGridSpec(
        num_scalar_prefetch=1,
        grid=(n_pairs, T // blk),
        in_specs=[qspec, qspec, kspec, kspec,
                  pl.BlockSpec((2 * V_ROWS, T), lambda p, i, jlo: (p, 0)),
                  qspec],
        out_specs=qspec,
        scratch_shapes=[pltpu.VMEM((2, 2, blk, blk), F32)],
    )
    return pl.pallas_call(
        functools.partial(_fox_attn_kernel, blk=blk),
        grid_spec=grid_spec,
        out_shape=jax.ShapeDtypeStruct((T, D), BF16),
        compiler_params=_cparams("parallel", "arbitrary"),
        name="fox_attn",
    )(jlo, q, augq, k, augk, v_t, og)


def _first_live_block(c2, q_gain, k_gain, blk):
    T, H = c2.shape
    bound = 8.0 * LOG2E * 1.02 * jnp.max(jnp.abs(q_gain)) * jnp.max(jnp.abs(k_gain))
    c_first = c2[0::blk]
    c_last = c2[blk - 1::blk]
    dead = c_last[None, :, :] > c_first[:, None, :] + (2.0 * bound + 104.0 * LOG2E + 1.0)
    jlo = jnp.sum(dead, axis=1).astype(jnp.int32)
    jlo = jnp.min(jlo.reshape(-1, H // 2, 2), axis=-1)
    return jlo.T.reshape(-1)


def _router_kernel(h_ref, x_ref, wo_ref, ng_ref, wr_ref, h1_ref, u_ref, info_ref, cnt_ref, csr_ref, carry_ref):
    i = pl.program_id(0)
    tm = h_ref.shape[0]
    E = N_EXPERTS

    @pl.when(i == 0)
    def _():
        carry_ref[...] = jnp.zeros_like(carry_ref)

    csr_ref[0] = carry_ref[...]
    h1 = h_ref[...] + jnp.dot(x_ref[...], wo_ref[...], preferred_element_type=F32)
    h1_ref[...] = h1
    u = _rms(h1, ng_ref[...])
    u_ref[...] = u.astype(BF16)
    u_hi, u_lo = _split_hi_lo(u)
    w_hi, w_lo = _split_hi_lo(wr_ref[...])
    logits = (jnp.dot(u_hi, w_hi, preferred_element_type=F32)
              + jnp.dot(u_hi, w_lo, preferred_element_type=F32)
              + jnp.dot(u_lo, w_hi, preferred_element_type=F32))
    lane = lax.broadcasted_iota(jnp.int32, (tm, E), 1)
    v1 = jnp.max(logits, axis=-1, keepdims=True)
    e1 = jnp.min(jnp.where(logits == v1, lane, E), axis=-1, keepdims=True)
    rest = jnp.where(lane == e1, -jnp.inf, logits)
    v2 = jnp.max(rest, axis=-1, keepdims=True)
    e2 = jnp.min(jnp.where(rest == v2, lane, E), axis=-1, keepdims=True)
    g1 = 1.0 / (1.0 + jnp.exp(v2 - v1))
    g2 = 1.0 - g1
    oh1 = lane == e1
    oh2 = lane == e2
    onehot = jnp.where(oh1 | oh2, 1.0, 0.0).astype(BF16)
    row = lax.broadcasted_iota(jnp.int32, (tm, tm), 0)
    col = lax.broadcasted_iota(jnp.int32, (tm, tm), 1)
    tri_strict = jnp.where(col < row, 1.0, 0.0).astype(BF16)
    before = jnp.dot(tri_strict, onehot, preferred_element_type=F32) + carry_ref[...]
    rank1 = jnp.sum(jnp.where(oh1, before, 0.0), axis=-1, keepdims=True)
    rank2 = jnp.sum(jnp.where(oh2, before, 0.0), axis=-1, keepdims=True)
    info = jnp.where(lane == 0, e1.astype(F32), 0.0)
    info = jnp.where(lane == 1, e2.astype(F32), info)
    info = jnp.where(lane == 2, g1, info)
    info = jnp.where(lane == 3, g2, info)
    info = jnp.where(lane == 4, rank1, info)
    info = jnp.where(lane == 5, rank2, info)
    info_ref[...] = info
    total = carry_ref[...] + jnp.sum(onehot.astype(F32), axis=0, keepdims=True)
    carry_ref[...] = total
    cnt_ref[...] = total


def _router(h, x, w_o, ng, w_router):
    T, D = h.shape
    E = N_EXPERTS
    tm = min(MOE_CHUNK, T)
    row = pl.BlockSpec((tm, D), lambda i: (i, 0))
    return pl.pallas_call(
        _router_kernel,
        grid=(T // tm,),
        in_specs=[row, row, _full((D, D)), _full((1, D)), _full((D, E))],
        out_specs=[row, row, pl.BlockSpec((tm, E), lambda i: (i, 0)), _full((1, E)),
                   pl.BlockSpec((1, 1, E), lambda i: (i, 0, 0))],
        out_shape=[jax.ShapeDtypeStruct((T, D), F32), jax.ShapeDtypeStruct((T, D), BF16),
                   jax.ShapeDtypeStruct((T, E), F32), jax.ShapeDtypeStruct((1, E), F32),
                   jax.ShapeDtypeStruct((T // tm, 1, E), F32)],
        scratch_shapes=[pltpu.VMEM((1, E), F32)],
        compiler_params=_cparams("arbitrary"),
        name="router",
    )(h, x, w_o, ng, w_router)


def _combine_kernel(pos_ref, h_ref, gate_ref, fg_ref, y_ref, o_ref, ya_ref, yb_ref, sem):
    b = pl.program_id(0)
    nb = pl.num_programs(0)
    rows = h_ref.shape[0]

    def copies(blk, r):
        slot = blk % 2
        base = 2 * (blk * rows + r)
        return (pltpu.make_async_copy(y_ref.at[pl.ds(pos_ref[base], 1)], ya_ref.at[slot, pl.ds(r, 1)],
                                      sem.at[slot]),
                pltpu.make_async_copy(y_ref.at[pl.ds(pos_ref[base + 1], 1)], yb_ref.at[slot, pl.ds(r, 1)],
                                      sem.at[slot]))

    def issue(blk):
        def body(r, _):
            ca, cb = copies(blk, r)
            ca.start()
            cb.start()
            return 0
        lax.fori_loop(0, rows, body, 0, unroll=8)

    @pl.when(b == 0)
    def _():
        issue(0)

    @pl.when(b + 1 < nb)
    def _():
        issue(b + 1)

    def wait(r, _):
        ca, cb = copies(b, r)
        ca.wait()
        cb.wait()
        return 0

    lax.fori_loop(0, rows, wait, 0, unroll=8)
    slot = b % 2
    gate = gate_ref[...]
    moe = gate[:, 0:1] * ya_ref[slot] + gate[:, 1:2] * yb_ref[slot]
    o_ref[...] = _rms(h_ref[...] + moe, fg_ref[...])


def _combine(pos, h, gate, final_g, y, rows=256):
    T, D = h.shape
    row = pl.BlockSpec((rows, D), lambda b, pos: (b, 0))
    grid_spec = pltpu.PrefetchScalarGridSpec(
        num_scalar_prefetch=1,
        grid=(T // rows,),
        in_specs=[row, pl.BlockSpec((rows, 2), lambda b, pos: (b, 0)),
                  pl.BlockSpec((1, D), lambda b, pos: (0, 0)), pl.BlockSpec(memory_space=pl.ANY)],
        out_specs=row,
        scratch_shapes=[pltpu.VMEM((2, rows, D), F32), pltpu.VMEM((2, rows, D), F32),
                        pltpu.SemaphoreType.DMA((2,))],
    )
    return pl.pallas_call(
        _combine_kernel,
        grid_spec=grid_spec,
        out_shape=jax.ShapeDtypeStruct((T, D), F32),
        compiler_params=_cparams("arbitrary"),
        name="moe_combine",
    )(pos, h, gate, final_g, y)


def _rwkv_layer(h, ng, mu, w_rkv, w0, w1, w2, a0, a1, a2, g1, g2, k_k, k_a, r_k, gn_g, gn_b, w_o):
    D = h.shape[1]
    vec = lambda z: z.reshape(1, D)
    b16 = lambda z: z.astype(BF16)
    r, k, v, lw, a, g = _rwkv_pre(h, vec(ng), mu, b16(w_rkv), vec(w0), b16(w1), b16(w2), vec(a0),
                                  b16(a1), b16(a2), b16(g1), b16(g2))
    o = _rwkv_scan(r, k, v, lw, a, g, vec(k_k), vec(k_a), vec(r_k), vec(gn_g), vec(gn_b))
    return o, b16(w_o)


def _fox_layer(h, ng, w_in, b_f, q_gain, k_gain, w_o):
    T, D = h.shape
    H = D // HEAD_DIM
    order = jnp.argsort(b_f)
    w_main = w_in[:, :4 * D].reshape(D, 4, H, HEAD_DIM)[:, :, order, :].reshape(D, 4 * D)
    w_out = w_o.reshape(H, HEAD_DIM, D)[order].reshape(D, D)
    q, k, v, og, c2, augq, augk = _fox_pre(
        h, ng.reshape(1, D), w_main.astype(BF16), w_in[:, 4 * D:][:, order].astype(BF16),
        b_f[order].reshape(1, H), jnp.tile(q_gain, H).reshape(1, D), jnp.tile(k_gain, H).reshape(1, D))
    blk = min(ATTN_BLOCK, T)
    jlo = _first_live_block(c2, q_gain, k_gain, blk)
    extra = jnp.zeros((H, V_ROWS - HEAD_DIM, T), BF16).at[:, 0, :].set(1.0)
    v_t = jnp.concatenate([v.T.reshape(H, HEAD_DIM, T), extra], axis=1).reshape(H * V_ROWS, T)
    o = _fox_attn(jlo, q, augq, k, augk, v_t, og, blk)
    return o, w_out.astype(BF16)


def _moe_layer(h, x, w_o, ng, final_g, w_router, w_gu, w_down):
    T, D = h.shape
    E, RB = N_EXPERTS, MOE_ROWS
    h, u, info, cnt, csr = _router(h, x, w_o, ng.reshape(1, D), w_router)
    e = info[:, 0:2].astype(jnp.int32)
    gate = info[:, 2:4]
    rank = info[:, 4:6].astype(jnp.int32)
    counts = cnt[0].astype(jnp.int32)
    padded = (counts + RB - 1) // RB * RB
    ends = jnp.cumsum(padded)
    starts = ends - padded
    n_blocks = (2 * T) // RB + E
    dest = starts[e] + rank
    blocks = jnp.arange(n_blocks, dtype=jnp.int32)
    block_e = jnp.minimum(jnp.sum(blocks[:, None] * RB >= ends[None, :], axis=1), E - 1).astype(jnp.int32)

    def first_rank(size):
        groups = jnp.arange(n_blocks * (RB // size), dtype=jnp.int32)
        group_e = jnp.repeat(block_e, RB // size)
        rank0 = groups * size - starts[group_e]
        return group_e, rank0, rank0 < counts[group_e]

    _, _, half_live = first_rank(MOE_HALF)
    group_e, rank0, group_live = first_rank(MOE_SUB)
    csr_e = csr[:, 0, :].astype(jnp.int32)[:, group_e]
    n_chunks = csr_e.shape[0]
    c_lo = jnp.clip(jnp.sum(csr_e <= rank0[None, :], axis=0) - 1, 0, n_chunks - 1)
    c_hi = jnp.clip(jnp.sum(csr_e < rank0[None, :] + MOE_SUB, axis=0) - 1, 0, n_chunks - 1)
    c_lo = jnp.where(group_live, c_lo, 1).astype(jnp.int32)
    c_hi = jnp.where(group_live, c_hi, 0).astype(jnp.int32)
    y = _moe_ffn(block_e, half_live.astype(jnp.int32), c_lo, c_hi, dest.T, u, w_gu, w_down, n_blocks)
    return _combine(dest.reshape(-1), h, gate, final_g.reshape(1, D), y)


def kernel(x, norm_g, final_g, rwkv_mu, rwkv_w_rkv, rwkv_w0, rwkv_w1, rwkv_w2, rwkv_a0, rwkv_a1,
           rwkv_a2, rwkv_g1, rwkv_g2, rwkv_k_k, rwkv_k_a, rwkv_r_k, rwkv_gn_g, rwkv_gn_b, rwkv_w_o,
           fox_w_in, fox_b_f, fox_q_gain, fox_k_gain, fox_w_o, ffn_w_gu, ffn_w_down,
           moe_w_router, moe_w_gu, moe_w_down):
    B, T, D = x.shape
    h = x.reshape(B * T, D)
    o, w_o = _rwkv_layer(h, norm_g[0, 0], rwkv_mu[0], rwkv_w_rkv[0], rwkv_w0[0], rwkv_w1[0], rwkv_w2[0],
                    rwkv_a0[0], rwkv_a1[0], rwkv_a2[0], rwkv_g1[0], rwkv_g2[0], rwkv_k_k[0],
                    rwkv_k_a[0], rwkv_r_k[0], rwkv_gn_g[0], rwkv_gn_b[0], rwkv_w_o[0])
    h = _ffn(h, o, w_o, norm_g[0, 1].reshape(1, D), ffn_w_gu[0].astype(BF16), ffn_w_down[0].astype(BF16))
    o, w_o = _fox_layer(h, norm_g[1, 0], fox_w_in[0], fox_b_f[0], fox_q_gain[0], fox_k_gain[0], fox_w_o[0])
    out = _moe_layer(h, o, w_o, norm_g[1, 1], final_g, moe_w_router[0], moe_w_gu[0], moe_w_down[0])
    return out.reshape(B, T, D)
```

```python
import functools

import jax
import jax.numpy as jnp
from jax import lax
from jax.experimental import pallas as pl
from jax.experimental.pallas import tpu as pltpu

F32 = jnp.float32
BF16 = jnp.bfloat16

HEAD_DIM = 64
PAIR = 2 * HEAD_DIM
GN_EPS = 64e-5
RMS_EPS = 1e-6
N_EXPERTS = 8
SCAN_CHUNK = 64
SCAN_CHUNKS_PER_STEP = 2
MOE_ROWS = 1024
MOE_HALF = 512
MOE_SUB = 128
MOE_CHUNK = 512
ATTN_BLOCK = 512
LOG2E = 1.4426950408889634
VMEM_LIMIT = 56 * 1024 * 1024
VMEM_LIMIT_MOE = 60 * 1024 * 1024


def _cparams(*sem, vmem_limit=VMEM_LIMIT):
    return pltpu.CompilerParams(dimension_semantics=sem, vmem_limit_bytes=vmem_limit)


def _dot(a, b):
    return jnp.dot(a.astype(BF16), b.astype(BF16), preferred_element_type=F32)


def _dot_nt(a, b):
    return lax.dot_general(a.astype(BF16), b.astype(BF16), (((1,), (1,)), ((), ())),
                           preferred_element_type=F32)


def _dot_tn(a, b):
    return lax.dot_general(a.astype(BF16), b.astype(BF16), (((0,), (0,)), ((), ())),
                           preferred_element_type=F32)


def _split_hi_lo(x):
    hi = x.astype(BF16)
    lo = (x - hi.astype(F32)).astype(BF16)
    return hi, lo


def _rms(x, g):
    ms = jnp.mean(x * x, axis=-1, keepdims=True)
    return x * lax.rsqrt(ms + RMS_EPS) * g


def _sigmoid(x):
    return 1.0 / (1.0 + jnp.exp(-x))


def _full(shape):
    return pl.BlockSpec(shape, lambda *_: (0,) * len(shape))


def _rwkv_pre_kernel(h_ref, hp_ref, ng_ref, mu_ref, wrkv_ref, w0_ref, w1_ref, w2_ref, a0_ref,
                     a1_ref, a2_ref, g1_ref, g2_ref, r_ref, k_ref, v_ref, lw_ref, a_ref, g_ref):
    i = pl.program_id(0)
    ng = ng_ref[...]
    u = _rms(h_ref[...], ng)
    up = _rms(hp_ref[7:8, :], ng)
    up = jnp.where(i == 0, 0.0, up)
    row = lax.broadcasted_iota(jnp.int32, u.shape, 0)
    u_prev = jnp.where(row == 0, up, pltpu.roll(u, 1, axis=0))
    dx = u_prev - u
    mix = lambda n: u + dx * mu_ref[n:n + 1, :]
    r_ref[...] = _dot(mix(0), wrkv_ref[0])
    k_ref[...] = _dot(mix(1), wrkv_ref[1])
    v_ref[...] = _dot(mix(2), wrkv_ref[2])
    z = w0_ref[...] + _dot(jnp.tanh(_dot(mix(3), w1_ref[...])), w2_ref[...])
    softplus_neg = jnp.maximum(-z, 0.0) + jnp.log(1.0 + jnp.exp(-jnp.abs(z)))
    lw_ref[...] = -jnp.exp(-softplus_neg - 0.5)
    a_ref[...] = _sigmoid(a0_ref[...] + _dot(_dot(mix(4), a1_ref[...]), a2_ref[...]))
    g_ref[...] = _dot(_sigmoid(_dot(mix(5), g1_ref[...])), g2_ref[...])


def _rwkv_pre(h, ng, mu, wrkv, w0, w1, w2, a0, a1, a2, g1, g2, tm=512):
    T, D = h.shape
    row = pl.BlockSpec((tm, D), lambda i: (i, 0))
    prev = pl.BlockSpec((8, D), lambda i: (jnp.maximum(i * (tm // 8) - 1, 0), 0))
    ins = [ng, mu, wrkv, w0, w1, w2, a0, a1, a2, g1, g2]
    return pl.pallas_call(
        _rwkv_pre_kernel,
        grid=(T // tm,),
        in_specs=[row, prev] + [_full(a.shape) for a in ins],
        out_specs=[row] * 6,
        out_shape=[jax.ShapeDtypeStruct((T, D), F32)] * 6,
        compiler_params=_cparams("parallel"),
        name="rwkv_pre",
    )(h, h, *ins)


def _scan_kernel(r_ref, k_ref, v_ref, lw_ref, a_ref, g_ref, kk_ref, ka_ref, rk_ref, gng_ref,
                 gnb_ref, o_ref, s_ref):
    L = SCAN_CHUNK
    n_pairs = r_ref.shape[1] // PAIR

    @pl.when(pl.program_id(0) == 0)
    def _():
        s_ref[...] = jnp.zeros_like(s_ref)

    row = lax.broadcasted_iota(jnp.int32, (L, L), 0)
    col = lax.broadcasted_iota(jnp.int32, (L, L), 1)
    tri_incl = jnp.where(col <= row, 1.0, 0.0).astype(BF16)
    eye = jnp.where(col == row, 1.0, 0.0).astype(F32)
    lane = lax.broadcasted_iota(jnp.int32, (1, PAIR), 1)
    head0 = lane < HEAD_DIM
    arow = lax.broadcasted_iota(jnp.int32, (4 * L, PAIR), 0)
    acol = lax.broadcasted_iota(jnp.int32, (4 * L, PAIR), 1)
    t_idx = arow % L + jnp.where(arow < 2 * L, 0, 1)
    j_idx = acol % L
    a_mask = j_idx < t_idx
    brow = lax.broadcasted_iota(jnp.int32, (PAIR, PAIR), 0)
    bcol = lax.broadcasted_iota(jnp.int32, (PAIR, PAIR), 1)
    blockdiag = (brow < HEAD_DIM) == (bcol < HEAD_DIM)

    def head_sum(x):
        s0 = jnp.sum(jnp.where(head0, x, 0.0), axis=-1, keepdims=True)
        s1 = jnp.sum(jnp.where(head0, 0.0, x), axis=-1, keepdims=True)
        return jnp.where(head0, s0, s1)

    pairs = range(n_pairs)
    heads = [(p, h) for p in pairs for h in range(2)]
    sls = [slice(p * PAIR, (p + 1) * PAIR) for p in pairs]
    n_chunks = r_ref.shape[0] // L
    chunks = range(n_chunks)
    rows = [slice(c * L, (c + 1) * L) for c in chunks]
    cp = [(c, p) for c in chunks for p in pairs]
    cph = [(c, p, h) for c in chunks for p, h in heads]

    lws = {(c, p): lw_ref[rows[c], sls[p]] for c, p in cp}
    splits = {k: _split_hi_lo(lw) for k, lw in lws.items()}
    css = {k: jnp.dot(tri_incl, hi, preferred_element_type=F32) + jnp.dot(tri_incl, lo, preferred_element_type=F32)
           for k, (hi, lo) in splits.items()}
    prep = {}
    for c, p in cp:
        rw, sl, cs, lw = rows[c], sls[p], css[c, p], lws[c, p]
        w_inv = jnp.exp(-cs)
        w_last = jnp.exp(cs[L - 1:L, :])
        r = r_ref[rw, sl]
        k = k_ref[rw, sl]
        a = a_ref[rw, sl]
        kk = k * kk_ref[:, sl]
        kk = kk / jnp.maximum(jnp.sqrt(head_sum(kk * kk)), 1e-12)
        kmod = k * (1.0 + (a - 1.0) * ka_ref[:, sl])
        at = -kk * jnp.exp(cs - lw)
        bt = kk * a * w_inv
        kt = kmod * w_inv
        rt = r * jnp.exp(cs)
        v = v_ref[rw, sl]
        prep[c, p] = dict(atrt=jnp.concatenate([at, rt], axis=0).astype(BF16),
                          x_stack=jnp.concatenate([jnp.where(head0, at, 0.0), jnp.where(head0, 0.0, at),
                                                   jnp.where(head0, rt, 0.0), jnp.where(head0, 0.0, rt)],
                                                  axis=0).astype(BF16),
                          bk=jnp.concatenate([bt, kt], axis=0).astype(BF16),
                          bkhat=jnp.concatenate([bt * w_last, kt * w_last], axis=0).astype(BF16),
                          w_last=w_last, v=v, v_bf=v.astype(BF16), rk=r * kmod)
    aas = {k: jnp.where(a_mask, _dot_nt(prep[k]["x_stack"], prep[k]["bk"]), 0.0) for k in cp}
    akv = {}
    pw = {}
    t_inv = {}
    for c, p, h in cph:
        a_rows = aas[c, p][h * L:(h + 1) * L, :]
        vv = jnp.concatenate([prep[c, p]["v_bf"], prep[c, p]["v_bf"]], axis=0)
        akv[c, p, h] = _dot(jnp.where(head0, 0.0, a_rows), vv)
        pw[c, p, h] = a_rows[:, :L]
        t_inv[c, p, h] = eye + pw[c, p, h]
    for _ in range(5):
        for k in cph:
            pw[k] = _dot(pw[k], pw[k])
        for k in cph:
            t_inv[k] = t_inv[k] + _dot(t_inv[k], pw[k])

    states = [s_ref[p] for p in pairs]
    for c in chunks:
        inter = [_dot_nt(prep[c, p]["atrt"], states[p]) for p in pairs]
        u_h = {(p, h): _dot(t_inv[c, p, h], inter[p][:L, :] + akv[c, p, h]) for p, h in heads}
        y_h = {}
        for p, h in heads:
            ar = aas[c, p][(2 + h) * L:(3 + h) * L, :]
            y_h[p, h] = _dot(ar, jnp.concatenate([u_h[p, h].astype(BF16), prep[c, p]["v_bf"]], axis=0))
        new_states = []
        for p in pairs:
            pr = prep[c, p]
            u = jnp.where(head0, u_h[p, 0], u_h[p, 1])
            uv = jnp.concatenate([u.astype(BF16), pr["v_bf"]], axis=0)
            new_states.append(states[p] * pr["w_last"] + jnp.where(blockdiag, _dot_tn(uv, pr["bkhat"]), 0.0))
        states = new_states
        for p in pairs:
            sl = sls[p]
            pr = prep[c, p]
            y = inter[p][L:, :] + jnp.where(head0, y_h[p, 0], y_h[p, 1])
            mean = head_sum(y) * (1.0 / HEAD_DIM)
            d = y - mean
            var = head_sum(d * d) * (1.0 / HEAD_DIM)
            yn = d * lax.rsqrt(var + GN_EPS) * gng_ref[:, sl] + gnb_ref[:, sl]
            bonus = head_sum(pr["rk"] * rk_ref[:, sl]) * pr["v"]
            o_ref[rows[c], sl] = ((yn + bonus) * g_ref[rows[c], sl]).astype(o_ref.dtype)
    for p in pairs:
        s_ref[p] = states[p]


def _rwkv_scan(r, k, v, lw, a, g, k_k, k_a, r_k, gn_g, gn_b):
    T, D = r.shape
    rows = SCAN_CHUNK * SCAN_CHUNKS_PER_STEP
    row = pl.BlockSpec((rows, D), lambda c: (c, 0))
    par = _full((1, D))
    return pl.pallas_call(
        _scan_kernel,
        grid=(T // rows,),
        in_specs=[row] * 6 + [par] * 5,
        out_specs=row,
        out_shape=jax.ShapeDtypeStruct((T, D), BF16),
        scratch_shapes=[pltpu.VMEM((D // PAIR, PAIR, PAIR), F32)],
        compiler_params=_cparams("arbitrary"),
        name="rwkv_scan",
    )(r, k, v, lw, a, g, k_k, k_a, r_k, gn_g, gn_b)


def _swiglu_step(u_ref, wg_ref, wu_ref, wd_ref, acc_ref):
    u = u_ref[...]
    gate = jnp.dot(u, wg_ref[...].astype(BF16), preferred_element_type=F32)
    up = jnp.dot(u, wu_ref[...].astype(BF16), preferred_element_type=F32)
    hid = (gate * _sigmoid(gate) * up).astype(BF16)
    acc_ref[...] += jnp.dot(hid, wd_ref[...].astype(BF16), preferred_element_type=F32)


def _ffn_kernel(h_ref, x_ref, wo_ref, ng_ref, wg_ref, wu_ref, wd_ref, o_ref, u_ref):
    @pl.when(pl.program_id(1) == 0)
    def _():
        h1 = h_ref[...] + jnp.dot(x_ref[...], wo_ref[...], preferred_element_type=F32)
        o_ref[...] = h1
        u_ref[...] = _rms(h1, ng_ref[...]).astype(BF16)

    _swiglu_step(u_ref, wg_ref, wu_ref, wd_ref, o_ref)


def _ffn(h, x, w_o, ng, w_gu, w_down, tm=1024, tf=512):
    T, D = h.shape
    F = w_down.shape[0]
    nf = F // tf
    row = pl.BlockSpec((tm, D), lambda i, f: (i, 0))
    return pl.pallas_call(
        _ffn_kernel,
        grid=(T // tm, nf),
        in_specs=[row, row, _full((D, D)), _full((1, D)),
                  pl.BlockSpec((D, tf), lambda i, f: (0, f)),
                  pl.BlockSpec((D, tf), lambda i, f: (0, f + nf)),
                  pl.BlockSpec((tf, D), lambda i, f: (f, 0))],
        out_specs=row,
        out_shape=jax.ShapeDtypeStruct((T, D), F32),
        scratch_shapes=[pltpu.VMEM((tm, D), BF16)],
        compiler_params=_cparams("parallel", "arbitrary"),
        name="ffn",
    )(h, x, w_o, ng, w_gu, w_gu, w_down)


def _moe_ffn_kernel(be_ref, live_ref, clo_ref, chi_ref, dest_ref, tok_hbm, wg_ref, wu_ref, wd_ref, o_ref,
                    tok_ref, x_ref, sem):
    b = pl.program_id(0)
    f = pl.program_id(1)
    rows = o_ref.shape[0]
    ch = MOE_CHUNK
    halves = rows // MOE_HALF

    @pl.when((b == 0) & (f == 0))
    def _():
        cp = pltpu.make_async_copy(tok_hbm, tok_ref, sem)
        cp.start()
        cp.wait()

    @pl.when(f == 0)
    def _():
        o_ref[...] = jnp.zeros_like(o_ref)
        for s in range(rows // MOE_SUB):
            sub = slice(s * MOE_SUB, (s + 1) * MOE_SUB)
            row_id = b * rows + s * MOE_SUB + lax.broadcasted_iota(jnp.int32, (MOE_SUB, 1), 0)
            w = b * (rows // MOE_SUB) + s

            def chunk(c, _, sub=sub, row_id=row_id):
                start = pl.multiple_of(c * ch, ch)
                d = dest_ref[:, pl.ds(start, ch)]
                onehot = (jnp.where(row_id == d[0:1, :], 1.0, 0.0)
                          + jnp.where(row_id == d[1:2, :], 1.0, 0.0)).astype(BF16)
                o_ref[sub, :] += jnp.dot(onehot, tok_ref[pl.ds(start, ch), :], preferred_element_type=F32)
                return 0

            lax.fori_loop(clo_ref[w], chi_ref[w] + 1, chunk, 0)
        x_ref[...] = o_ref[...].astype(BF16)
        o_ref[...] = jnp.zeros_like(o_ref)

    for half in range(halves):
        @pl.when(live_ref[b * halves + half] > 0)
        def _(half=half):
            part = pl.ds(half * MOE_HALF, MOE_HALF)
            _swiglu_step(x_ref.at[part], wg_ref.at[0], wu_ref.at[0], wd_ref.at[0], o_ref.at[part])


def _moe_ffn(block_e, half_live, c_lo, c_hi, dest_t, tok, w_gu, w_down, n_blocks, tf=512):
    T, D = tok.shape
    F = w_down.shape[1]
    nf = F // tf
    tm = MOE_ROWS

    def fstep(b, f, live):
        return jnp.where(live[b * (tm // MOE_HALF)] > 0, f, nf - 1)

    grid_spec = pltpu.PrefetchScalarGridSpec(
        num_scalar_prefetch=4,
        grid=(n_blocks, nf),
        in_specs=[pl.BlockSpec((2, T), lambda b, f, *_: (0, 0)),
                  pl.BlockSpec(memory_space=pl.ANY),
                  pl.BlockSpec((1, D, tf), lambda b, f, be, live, *_: (be[b], 0, fstep(b, f, live))),
                  pl.BlockSpec((1, D, tf), lambda b, f, be, live, *_: (be[b], 0, fstep(b, f, live) + nf)),
                  pl.BlockSpec((1, tf, D), lambda b, f, be, live, *_: (be[b], fstep(b, f, live), 0))],
        out_specs=pl.BlockSpec((tm, D), lambda b, f, *_: (b, 0)),
        scratch_shapes=[pltpu.VMEM((T, D), BF16), pltpu.VMEM((tm, D), BF16), pltpu.SemaphoreType.DMA(())],
    )
    return pl.pallas_call(
        _moe_ffn_kernel,
        grid_spec=grid_spec,
        out_shape=jax.ShapeDtypeStruct((n_blocks * tm, D), F32),
        compiler_params=_cparams("arbitrary", "arbitrary", vmem_limit=VMEM_LIMIT_MOE),
        name="moe_ffn",
    )(block_e, half_live, c_lo, c_hi, dest_t, tok, w_gu, w_gu, w_down)


def _fox_pre_kernel(h_ref, ng_ref, w_ref, wf_ref, bf_ref, qg_ref, kg_ref, sel_ref, selt_ref, pq_ref, pk_ref,
                    oneq_ref, onek_ref, q_ref, k_ref, v_ref, og_ref, c2_ref, augq_ref, augk_ref, carry_ref):
    i = pl.program_id(0)
    tm, D = h_ref.shape
    H = c2_ref.shape[1]

    @pl.when(i == 0)
    def _():
        carry_ref[...] = jnp.zeros_like(carry_ref)

    u = _rms(h_ref[...], ng_ref[...]).astype(BF16)

    def head_norm(x, gain):
        ms = _dot(x * x, sel_ref[...]) * (1.0 / HEAD_DIM)
        inv = lax.rsqrt(ms + RMS_EPS)
        inv_hi, inv_lo = _split_hi_lo(inv)
        first = lax.broadcasted_iota(jnp.int32, inv.shape, 1) < H
        wide = jnp.dot(jnp.where(first, inv_hi, inv_lo), selt_ref[...], preferred_element_type=F32)
        return x * wide * gain

    q = jnp.dot(u, w_ref[:, 0:D], preferred_element_type=F32)
    q_ref[...] = (head_norm(q, qg_ref[...]) * (HEAD_DIM ** -0.5 * LOG2E)).astype(BF16)
    k = jnp.dot(u, w_ref[:, D:2 * D], preferred_element_type=F32)
    k_ref[...] = head_norm(k, kg_ref[...]).astype(BF16)
    v_ref[...] = jnp.dot(u, w_ref[:, 2 * D:3 * D], preferred_element_type=F32).astype(BF16)
    og_ref[...] = _sigmoid(jnp.dot(u, w_ref[:, 3 * D:4 * D], preferred_element_type=F32)).astype(BF16)

    fl = jnp.dot(u, wf_ref[...], preferred_element_type=F32) + bf_ref[...]
    log_f = jnp.minimum(fl, 0.0) - jnp.log(1.0 + jnp.exp(-jnp.abs(fl)))
    row = lax.broadcasted_iota(jnp.int32, (tm, tm), 0)
    col = lax.broadcasted_iota(jnp.int32, (tm, tm), 1)
    tri = jnp.where(col <= row, 1.0, 0.0).astype(BF16)
    hi, lo = _split_hi_lo(log_f)
    lo2 = (log_f - hi.astype(F32) - lo.astype(F32)).astype(BF16)
    c = (jnp.dot(tri, hi, preferred_element_type=F32) + jnp.dot(tri, lo, preferred_element_type=F32)
         + jnp.dot(tri, lo2, preferred_element_type=F32)) + carry_ref[...]
    carry_ref[...] = c[tm - 1:tm, :]
    c2 = c * LOG2E
    c2_ref[...] = c2[:, 0:H]
    c_hi, c_lo = _split_hi_lo(c2)
    c_lo2 = (c2 - c_hi.astype(F32) - c_lo.astype(F32)).astype(BF16)
    group = lax.broadcasted_iota(jnp.int32, c2.shape, 1) // H
    pieces = jnp.where(group == 0, c_hi, jnp.where(group == 1, c_lo, c_lo2))
    augq_ref[...] = (jnp.dot(pieces, pq_ref[...], preferred_element_type=F32) + oneq_ref[...]).astype(BF16)
    augk_ref[...] = (jnp.dot(pieces, pk_ref[...], preferred_element_type=F32) + onek_ref[...]).astype(BF16)


def _fox_pre(h, ng, w_main, w_f, b_f, q_gain, k_gain, tm=256):
    T, D = h.shape
    H = D // HEAD_DIM
    sel = (jnp.arange(D)[:, None] // HEAD_DIM == jnp.arange(2 * H)[None, :] % H).astype(BF16)
    piece = jnp.arange(3 * H)[:, None] // H
    head = jnp.arange(3 * H)[:, None] % H
    lane = jnp.arange(D)[None, :]
    at = lambda j: lane == head * HEAD_DIM + j
    p_q = at(piece).astype(BF16)
    p_k = -at(piece + 3).astype(BF16)
    off = jnp.arange(D) % HEAD_DIM
    one_q = ((off >= 3) & (off < 6)).astype(F32).reshape(1, D)
    one_k = (off < 3).astype(F32).reshape(1, D)
    row = pl.BlockSpec((tm, D), lambda i: (i, 0))
    ins = [ng, w_main, jnp.tile(w_f, (1, 3)), jnp.tile(b_f, (1, 3)), q_gain, k_gain, sel, sel.T, p_q, p_k,
           one_q, one_k]
    return pl.pallas_call(
        _fox_pre_kernel,
        grid=(T // tm,),
        in_specs=[row] + [_full(a.shape) for a in ins],
        out_specs=[row] * 4 + [pl.BlockSpec((tm, H), lambda i: (i, 0))] + [row] * 2,
        out_shape=([jax.ShapeDtypeStruct((T, D), BF16)] * 4 + [jax.ShapeDtypeStruct((T, H), F32)]
                   + [jax.ShapeDtypeStruct((T, D), BF16)] * 2),
        scratch_shapes=[pltpu.VMEM((1, 3 * H), F32)],
        compiler_params=_cparams("arbitrary"),
        name="fox_pre",
    )(h, *ins)


def _fox_attn_kernel(jlo_ref, q_ref, augq_ref, k_ref, augk_ref, vt_ref, og_ref, o_ref, s_ref, *, blk):
    p_idx = pl.program_id(0)
    i = pl.program_id(1)
    head0 = lax.broadcasted_iota(jnp.int32, (1, PAIR), 1) < HEAD_DIM
    q, augq = q_ref[...], augq_ref[...]
    zero = jnp.zeros_like(q)
    q_heads = [jnp.concatenate([jnp.where(m, q, zero), jnp.where(m, augq, zero)], axis=1)
               for m in (head0, jnp.logical_not(head0))]
    causal = (lax.broadcasted_iota(jnp.int32, (blk, blk), 0)
              <= lax.broadcasted_iota(jnp.int32, (blk, blk), 1))

    def scores_into(j, slot):
        start = pl.multiple_of(j * blk, blk)
        kb = jnp.concatenate([k_ref[pl.ds(start, blk), :], augk_ref[pl.ds(start, blk), :]], axis=1)
        for h in range(2):
            s_ref[slot, h] = lax.dot_general(kb, q_heads[h], (((1,), (1,)), ((), ())),
                                             preferred_element_type=F32)

    def softmax_pv(j, slot, carry, masked):
        start = pl.multiple_of(j * blk, blk)
        out = []
        for h in range(2):
            m_prev, l_prev, acc_prev = carry[h]
            vtb = vt_ref[h * HEAD_DIM:(h + 1) * HEAD_DIM, pl.ds(start, blk)]
            s = s_ref[slot, h]
            if masked:
                s = jnp.where(causal, s, -jnp.inf)
            m_new = jnp.maximum(m_prev, jnp.max(s, axis=0, keepdims=True))
            alpha = jnp.exp2(m_prev - m_new)
            p = jnp.exp2(s - m_new)
            l_new = alpha * l_prev + jnp.sum(p, axis=0, keepdims=True)
            acc_new = alpha * acc_prev + jnp.dot(vtb, p.astype(BF16), preferred_element_type=F32)
            out.append((m_new, l_new, acc_new))
        return tuple(out)

    init = tuple((jnp.full((1, blk), -jnp.inf, F32), jnp.zeros((1, blk), F32), jnp.zeros((HEAD_DIM, blk), F32))
                 for _ in range(2))
    jlo = jlo_ref[p_idx * pl.num_programs(1) + i]
    total = i - jlo + 1
    scores_into(jlo, 0)

    def two_blocks(t, carry):
        j = jlo + 2 * t
        scores_into(j + 1, 1)
        carry = softmax_pv(j, 0, carry, False)
        scores_into(j + 2, 0)
        return softmax_pv(j + 1, 1, carry, False)

    res = lax.fori_loop(0, (total - 1) // 2, two_blocks, init)

    def last_one(carry):
        return softmax_pv(i, 0, carry, True)

    def last_two(carry):
        scores_into(i, 1)
        carry = softmax_pv(i - 1, 0, carry, False)
        return softmax_pv(i, 1, carry, True)

    res = lax.cond(total % 2 == 1, last_one, last_two, res)
    o_t = jnp.concatenate([res[0][2] / res[0][1], res[1][2] / res[1][1]], axis=0)
    o_ref[...] = (o_t.T * og_ref[...].astype(F32)).astype(o_ref.dtype)


def _fox_attn(jlo, q, augq, k, augk, v_t, og, blk):
    T, D = q.shape
    n_pairs = D // PAIR
    qspec = pl.BlockSpec((blk, PAIR), lambda p, i, jlo: (i, p))
    kspec = pl.BlockSpec((T, PAIR), lambda p, i, jlo: (0, p))
    grid_spec = pltpu.PrefetchScalarGridSpec(
        num_scalar_prefetch=1,
        grid=(n_pairs, T // blk),
        in_specs=[qspec, qspec, kspec, kspec,
                  pl.BlockSpec((PAIR, T), lambda p, i, jlo: (p, 0)),
                  qspec],
        out_specs=qspec,
        scratch_shapes=[pltpu.VMEM((2, 2, blk, blk), F32)],
    )
    return pl.pallas_call(
        functools.partial(_fox_attn_kernel, blk=blk),
        grid_spec=grid_spec,
        out_shape=jax.ShapeDtypeStruct((T, D), BF16),
        compiler_params=_cparams("parallel", "arbitrary"),
        name="fox_attn",
    )(jlo, q, augq, k, augk, v_t, og)


def _first_live_block(c2, q_gain, k_gain, blk):
    T, H = c2.shape
    bound = 8.0 * LOG2E * 1.02 * jnp.max(jnp.abs(q_gain)) * jnp.max(jnp.abs(k_gain))
    c_first = c2[0::blk]
    c_last = c2[blk - 1::blk]
    dead = c_last[None, :, :] > c_first[:, None, :] + (2.0 * bound + 104.0 * LOG2E + 1.0)
    jlo = jnp.sum(dead, axis=1).astype(jnp.int32)
    jlo = jnp.min(jlo.reshape(-1, H // 2, 2), axis=-1)
    return jlo.T.reshape(-1)


def _router_kernel(h_ref, x_ref, wo_ref, ng_ref, wr_ref, h1_ref, u_ref, info_ref, cnt_ref, csr_ref, carry_ref):
    i = pl.program_id(0)
    tm = h_ref.shape[0]
    E = N_EXPERTS

    @pl.when(i == 0)
    def _():
        carry_ref[...] = jnp.zeros_like(carry_ref)

    csr_ref[0] = carry_ref[...]
    h1 = h_ref[...] + jnp.dot(x_ref[...], wo_ref[...], preferred_element_type=F32)
    h1_ref[...] = h1
    u = _rms(h1, ng_ref[...])
    u_ref[...] = u.astype(BF16)
    u_hi, u_lo = _split_hi_lo(u)
    w_hi, w_lo = _split_hi_lo(wr_ref[...])
    logits = (jnp.dot(u_hi, w_hi, preferred_element_type=F32)
              + jnp.dot(u_hi, w_lo, preferred_element_type=F32)
              + jnp.dot(u_lo, w_hi, preferred_element_type=F32))
    lane = lax.broadcasted_iota(jnp.int32, (tm, E), 1)
    v1 = jnp.max(logits, axis=-1, keepdims=True)
    e1 = jnp.min(jnp.where(logits == v1, lane, E), axis=-1, keepdims=True)
    rest = jnp.where(lane == e1, -jnp.inf, logits)
    v2 = jnp.max(rest, axis=-1, keepdims=True)
    e2 = jnp.min(jnp.where(rest == v2, lane, E), axis=-1, keepdims=True)
    g1 = 1.0 / (1.0 + jnp.exp(v2 - v1))
    g2 = 1.0 - g1
    oh1 = lane == e1
    oh2 = lane == e2
    onehot = jnp.where(oh1 | oh2, 1.0, 0.0).astype(BF16)
    row = lax.broadcasted_iota(jnp.int32, (tm, tm), 0)
    col = lax.broadcasted_iota(jnp.int32, (tm, tm), 1)
    tri_strict = jnp.where(col < row, 1.0, 0.0).astype(BF16)
    before = jnp.dot(tri_strict, onehot, preferred_element_type=F32) + carry_ref[...]
    rank1 = jnp.sum(jnp.where(oh1, before, 0.0), axis=-1, keepdims=True)
    rank2 = jnp.sum(jnp.where(oh2, before, 0.0), axis=-1, keepdims=True)
    info = jnp.where(lane == 0, e1.astype(F32), 0.0)
    info = jnp.where(lane == 1, e2.astype(F32), info)
    info = jnp.where(lane == 2, g1, info)
    info = jnp.where(lane == 3, g2, info)
    info = jnp.where(lane == 4, rank1, info)
    info = jnp.where(lane == 5, rank2, info)
    info_ref[...] = info
    total = carry_ref[...] + jnp.sum(onehot.astype(F32), axis=0, keepdims=True)
    carry_ref[...] = total
    cnt_ref[...] = total


def _router(h, x, w_o, ng, w_router):
    T, D = h.shape
    E = N_EXPERTS
    tm = min(MOE_CHUNK, T)
    row = pl.BlockSpec((tm, D), lambda i: (i, 0))
    return pl.pallas_call(
        _router_kernel,
        grid=(T // tm,),
        in_specs=[row, row, _full((D, D)), _full((1, D)), _full((D, E))],
        out_specs=[row, row, pl.BlockSpec((tm, E), lambda i: (i, 0)), _full((1, E)),
                   pl.BlockSpec((1, 1, E), lambda i: (i, 0, 0))],
        out_shape=[jax.ShapeDtypeStruct((T, D), F32), jax.ShapeDtypeStruct((T, D), BF16),
                   jax.ShapeDtypeStruct((T, E), F32), jax.ShapeDtypeStruct((1, E), F32),
                   jax.ShapeDtypeStruct((T // tm, 1, E), F32)],
        scratch_shapes=[pltpu.VMEM((1, E), F32)],
        compiler_params=_cparams("arbitrary"),
        name="router",
    )(h, x, w_o, ng, w_router)


def _combine_kernel(pos_ref, h_ref, gate_ref, fg_ref, y_ref, o_ref, ya_ref, yb_ref, sem):
    b = pl.program_id(0)
    nb = pl.num_programs(0)
    rows = h_ref.shape[0]

    def copies(blk, r):
        slot = blk % 2
        base = 2 * (blk * rows + r)
        return (pltpu.make_async_copy(y_ref.at[pl.ds(pos_ref[base], 1)], ya_ref.at[slot, pl.ds(r, 1)],
                                      sem.at[slot]),
                pltpu.make_async_copy(y_ref.at[pl.ds(pos_ref[base + 1], 1)], yb_ref.at[slot, pl.ds(r, 1)],
                                      sem.at[slot]))

    def issue(blk):
        def body(r, _):
            ca, cb = copies(blk, r)
            ca.start()
            cb.start()
            return 0
        lax.fori_loop(0, rows, body, 0, unroll=8)

    @pl.when(b == 0)
    def _():
        issue(0)

    @pl.when(b + 1 < nb)
    def _():
        issue(b + 1)

    def wait(r, _):
        ca, cb = copies(b, r)
        ca.wait()
        cb.wait()
        return 0

    lax.fori_loop(0, rows, wait, 0, unroll=8)
    slot = b % 2
    gate = gate_ref[...]
    moe = gate[:, 0:1] * ya_ref[slot] + gate[:, 1:2] * yb_ref[slot]
    o_ref[...] = _rms(h_ref[...] + moe, fg_ref[...])


def _combine(pos, h, gate, final_g, y, rows=512):
    T, D = h.shape
    row = pl.BlockSpec((rows, D), lambda b, pos: (b, 0))
    grid_spec = pltpu.PrefetchScalarGridSpec(
        num_scalar_prefetch=1,
        grid=(T // rows,),
        in_specs=[row, pl.BlockSpec((rows, 2), lambda b, pos: (b, 0)),
                  pl.BlockSpec((1, D), lambda b, pos: (0, 0)), pl.BlockSpec(memory_space=pl.ANY)],
        out_specs=row,
        scratch_shapes=[pltpu.VMEM((2, rows, D), F32), pltpu.VMEM((2, rows, D), F32),
                        pltpu.SemaphoreType.DMA((2,))],
    )
    return pl.pallas_call(
        _combine_kernel,
        grid_spec=grid_spec,
        out_shape=jax.ShapeDtypeStruct((T, D), F32),
        compiler_params=_cparams("arbitrary"),
        name="moe_combine",
    )(pos, h, gate, final_g, y)


def _rwkv_layer(h, ng, mu, w_rkv, w0, w1, w2, a0, a1, a2, g1, g2, k_k, k_a, r_k, gn_g, gn_b, w_o):
    D = h.shape[1]
    vec = lambda z: z.reshape(1, D)
    b16 = lambda z: z.astype(BF16)
    r, k, v, lw, a, g = _rwkv_pre(h, vec(ng), mu, b16(w_rkv), vec(w0), b16(w1), b16(w2), vec(a0),
                                  b16(a1), b16(a2), b16(g1), b16(g2))
    o = _rwkv_scan(r, k, v, lw, a, g, vec(k_k), vec(k_a), vec(r_k), vec(gn_g), vec(gn_b))
    return o, b16(w_o)


def _fox_layer(h, ng, w_in, b_f, q_gain, k_gain, w_o):
    T, D = h.shape
    H = D // HEAD_DIM
    order = jnp.argsort(b_f)
    w_main = w_in[:, :4 * D].reshape(D, 4, H, HEAD_DIM)[:, :, order, :].reshape(D, 4 * D)
    w_out = w_o.reshape(H, HEAD_DIM, D)[order].reshape(D, D)
    q, k, v, og, c2, augq, augk = _fox_pre(
        h, ng.reshape(1, D), w_main.astype(BF16), w_in[:, 4 * D:][:, order].astype(BF16),
        b_f[order].reshape(1, H), jnp.tile(q_gain, H).reshape(1, D), jnp.tile(k_gain, H).reshape(1, D))
    blk = min(ATTN_BLOCK, T)
    jlo = _first_live_block(c2, q_gain, k_gain, blk)
    o = _fox_attn(jlo, q, augq, k, augk, v.T, og, blk)
    return o, w_out.astype(BF16)


def _moe_layer(h, x, w_o, ng, final_g, w_router, w_gu, w_down):
    T, D = h.shape
    E, RB = N_EXPERTS, MOE_ROWS
    h, u, info, cnt, csr = _router(h, x, w_o, ng.reshape(1, D), w_router)
    e = info[:, 0:2].astype(jnp.int32)
    gate = info[:, 2:4]
    rank = info[:, 4:6].astype(jnp.int32)
    counts = cnt[0].astype(jnp.int32)
    padded = (counts + RB - 1) // RB * RB
    ends = jnp.cumsum(padded)
    starts = ends - padded
    n_blocks = (2 * T) // RB + E
    dest = starts[e] + rank
    blocks = jnp.arange(n_blocks, dtype=jnp.int32)
    block_e = jnp.minimum(jnp.sum(blocks[:, None] * RB >= ends[None, :], axis=1), E - 1).astype(jnp.int32)

    def first_rank(size):
        groups = jnp.arange(n_blocks * (RB // size), dtype=jnp.int32)
        group_e = jnp.repeat(block_e, RB // size)
        rank0 = groups * size - starts[group_e]
        return group_e, rank0, rank0 < counts[group_e]

    _, _, half_live = first_rank(MOE_HALF)
    group_e, rank0, group_live = first_rank(MOE_SUB)
    csr_e = csr[:, 0, :].astype(jnp.int32)[:, group_e]
    n_chunks = csr_e.shape[0]
    c_lo = jnp.clip(jnp.sum(csr_e <= rank0[None, :], axis=0) - 1, 0, n_chunks - 1)
    c_hi = jnp.clip(jnp.sum(csr_e < rank0[None, :] + MOE_SUB, axis=0) - 1, 0, n_chunks - 1)
    c_lo = jnp.where(group_live, c_lo, 1).astype(jnp.int32)
    c_hi = jnp.where(group_live, c_hi, 0).astype(jnp.int32)
    y = _moe_ffn(block_e, half_live.astype(jnp.int32), c_lo, c_hi, dest.T, u, w_gu, w_down, n_blocks)
    return _combine(dest.reshape(-1), h, gate, final_g.reshape(1, D), y)


def kernel(x, norm_g, final_g, rwkv_mu, rwkv_w_rkv, rwkv_w0, rwkv_w1, rwkv_w2, rwkv_a0, rwkv_a1,
           rwkv_a2, rwkv_g1, rwkv_g2, rwkv_k_k, rwkv_k_a, rwkv_r_k, rwkv_gn_g, rwkv_gn_b, rwkv_w_o,
           fox_w_in, fox_b_f, fox_q_gain, fox_k_gain, fox_w_o, ffn_w_gu, ffn_w_down,
           moe_w_router, moe_w_gu, moe_w_down):
    B, T, D = x.shape
    h = x.reshape(B * T, D)
    o, w_o = _rwkv_layer(h, norm_g[0, 0], rwkv_mu[0], rwkv_w_rkv[0], rwkv_w0[0], rwkv_w1[0], rwkv_w2[0],
                    rwkv_a0[0], rwkv_a1[0], rwkv_a2[0], rwkv_g1[0], rwkv_g2[0], rwkv_k_k[0],
                    rwkv_k_a[0], rwkv_r_k[0], rwkv_gn_g[0], rwkv_gn_b[0], rwkv_w_o[0])
    h = _ffn(h, o, w_o, norm_g[0, 1].reshape(1, D), ffn_w_gu[0].astype(BF16), ffn_w_down[0].astype(BF16))
    o, w_o = _fox_layer(h, norm_g[1, 0], fox_w_in[0], fox_b_f[0], fox_q_gain[0], fox_k_gain[0], fox_w_o[0])
    out = _moe_layer(h, o, w_o, norm_g[1, 1], final_g, moe_w_router[0], moe_w_gu[0], moe_w_down[0])
    return out.reshape(B, T, D)
```

```python
import functools

import jax
import jax.numpy as jnp
from jax import lax
from jax.experimental import pallas as pl
from jax.experimental.pallas import tpu as pltpu

F32 = jnp.float32
BF16 = jnp.bfloat16

HEAD_DIM = 64
PAIR = 2 * HEAD_DIM
GN_EPS = 64e-5
RMS_EPS = 1e-6
N_EXPERTS = 8
SCAN_CHUNK = 64
SCAN_CHUNKS_PER_STEP = 2
MOE_ROWS = 1024
MOE_HALF = 512
MOE_SUB = 128
MOE_CHUNK = 512
ATTN_BLOCK = 512
LOG2E = 1.4426950408889634
VMEM_LIMIT = 56 * 1024 * 1024
VMEM_LIMIT_MOE = 60 * 1024 * 1024


def _cparams(*sem, vmem_limit=VMEM_LIMIT):
    return pltpu.CompilerParams(dimension_semantics=sem, vmem_limit_bytes=vmem_limit)


def _dot(a, b):
    return jnp.dot(a.astype(BF16), b.astype(BF16), preferred_element_type=F32)


def _dot_nt(a, b):
    return lax.dot_general(a.astype(BF16), b.astype(BF16), (((1,), (1,)), ((), ())),
                           preferred_element_type=F32)


def _dot_tn(a, b):
    return lax.dot_general(a.astype(BF16), b.astype(BF16), (((0,), (0,)), ((), ())),
                           preferred_element_type=F32)


def _split_hi_lo(x):
    hi = x.astype(BF16)
    lo = (x - hi.astype(F32)).astype(BF16)
    return hi, lo


def _rms(x, g):
    ms = jnp.mean(x * x, axis=-1, keepdims=True)
    return x * lax.rsqrt(ms + RMS_EPS) * g


def _sigmoid(x):
    return 1.0 / (1.0 + jnp.exp(-x))


def _full(shape):
    return pl.BlockSpec(shape, lambda *_: (0,) * len(shape))


def _rwkv_pre_kernel(h_ref, hp_ref, ng_ref, mu_ref, wrkv_ref, w0_ref, w1_ref, w2_ref, a0_ref,
                     a1_ref, a2_ref, g1_ref, g2_ref, r_ref, k_ref, v_ref, lw_ref, a_ref, g_ref):
    i = pl.program_id(0)
    ng = ng_ref[...]
    u = _rms(h_ref[...], ng)
    up = _rms(hp_ref[7:8, :], ng)
    up = jnp.where(i == 0, 0.0, up)
    row = lax.broadcasted_iota(jnp.int32, u.shape, 0)
    u_prev = jnp.where(row == 0, up, pltpu.roll(u, 1, axis=0))
    dx = u_prev - u
    mix = lambda n: u + dx * mu_ref[n:n + 1, :]
    r_ref[...] = _dot(mix(0), wrkv_ref[0])
    k_ref[...] = _dot(mix(1), wrkv_ref[1])
    v_ref[...] = _dot(mix(2), wrkv_ref[2])
    z = w0_ref[...] + _dot(jnp.tanh(_dot(mix(3), w1_ref[...])), w2_ref[...])
    softplus_neg = jnp.maximum(-z, 0.0) + jnp.log(1.0 + jnp.exp(-jnp.abs(z)))
    lw_ref[...] = -jnp.exp(-softplus_neg - 0.5)
    a_ref[...] = _sigmoid(a0_ref[...] + _dot(_dot(mix(4), a1_ref[...]), a2_ref[...]))
    g_ref[...] = _dot(_sigmoid(_dot(mix(5), g1_ref[...])), g2_ref[...])


def _rwkv_pre(h, ng, mu, wrkv, w0, w1, w2, a0, a1, a2, g1, g2, tm=512):
    T, D = h.shape
    row = pl.BlockSpec((tm, D), lambda i: (i, 0))
    prev = pl.BlockSpec((8, D), lambda i: (jnp.maximum(i * (tm // 8) - 1, 0), 0))
    ins = [ng, mu, wrkv, w0, w1, w2, a0, a1, a2, g1, g2]
    return pl.pallas_call(
        _rwkv_pre_kernel,
        grid=(T // tm,),
        in_specs=[row, prev] + [_full(a.shape) for a in ins],
        out_specs=[row] * 6,
        out_shape=[jax.ShapeDtypeStruct((T, D), F32)] * 6,
        compiler_params=_cparams("parallel"),
        name="rwkv_pre",
    )(h, h, *ins)


def _scan_kernel(r_ref, k_ref, v_ref, lw_ref, a_ref, g_ref, kk_ref, ka_ref, rk_ref, gng_ref,
                 gnb_ref, o_ref, s_ref):
    L = SCAN_CHUNK
    n_pairs = r_ref.shape[1] // PAIR

    @pl.when(pl.program_id(0) == 0)
    def _():
        s_ref[...] = jnp.zeros_like(s_ref)

    row = lax.broadcasted_iota(jnp.int32, (L, L), 0)
    col = lax.broadcasted_iota(jnp.int32, (L, L), 1)
    tri_incl = jnp.where(col <= row, 1.0, 0.0).astype(BF16)
    eye = jnp.where(col == row, 1.0, 0.0).astype(F32)
    lane = lax.broadcasted_iota(jnp.int32, (1, PAIR), 1)
    head0 = lane < HEAD_DIM
    arow = lax.broadcasted_iota(jnp.int32, (4 * L, PAIR), 0)
    acol = lax.broadcasted_iota(jnp.int32, (4 * L, PAIR), 1)
    t_idx = arow % L + jnp.where(arow < 2 * L, 0, 1)
    j_idx = acol % L
    a_mask = j_idx < t_idx
    brow = lax.broadcasted_iota(jnp.int32, (PAIR, PAIR), 0)
    bcol = lax.broadcasted_iota(jnp.int32, (PAIR, PAIR), 1)
    blockdiag = (brow < HEAD_DIM) == (bcol < HEAD_DIM)

    def head_sum(x):
        s0 = jnp.sum(jnp.where(head0, x, 0.0), axis=-1, keepdims=True)
        s1 = jnp.sum(jnp.where(head0, 0.0, x), axis=-1, keepdims=True)
        return jnp.where(head0, s0, s1)

    pairs = range(n_pairs)
    heads = [(p, h) for p in pairs for h in range(2)]
    sls = [slice(p * PAIR, (p + 1) * PAIR) for p in pairs]
    n_chunks = r_ref.shape[0] // L
    chunks = range(n_chunks)
    rows = [slice(c * L, (c + 1) * L) for c in chunks]
    cp = [(c, p) for c in chunks for p in pairs]
    cph = [(c, p, h) for c in chunks for p, h in heads]

    lws = {(c, p): lw_ref[rows[c], sls[p]] for c, p in cp}
    splits = {k: _split_hi_lo(lw) for k, lw in lws.items()}
    css = {k: jnp.dot(tri_incl, hi, preferred_element_type=F32) + jnp.dot(tri_incl, lo, preferred_element_type=F32)
           for k, (hi, lo) in splits.items()}
    prep = {}
    for c, p in cp:
        rw, sl, cs, lw = rows[c], sls[p], css[c, p], lws[c, p]
        w_inv = jnp.exp(-cs)
        w_last = jnp.exp(cs[L - 1:L, :])
        r = r_ref[rw, sl]
        k = k_ref[rw, sl]
        a = a_ref[rw, sl]
        kk = k * kk_ref[:, sl]
        kk = kk / jnp.maximum(jnp.sqrt(head_sum(kk * kk)), 1e-12)
        kmod = k * (1.0 + (a - 1.0) * ka_ref[:, sl])
        at = -kk * jnp.exp(cs - lw)
        bt = kk * a * w_inv
        kt = kmod * w_inv
        rt = r * jnp.exp(cs)
        v = v_ref[rw, sl]
        prep[c, p] = dict(atrt=jnp.concatenate([at, rt], axis=0).astype(BF16),
                          x_stack=jnp.concatenate([jnp.where(head0, at, 0.0), jnp.where(head0, 0.0, at),
                                                   jnp.where(head0, rt, 0.0), jnp.where(head0, 0.0, rt)],
                                                  axis=0).astype(BF16),
                          bk=jnp.concatenate([bt, kt], axis=0).astype(BF16),
                          bkhat=jnp.concatenate([bt * w_last, kt * w_last], axis=0).astype(BF16),
                          w_last=w_last, v=v, v_bf=v.astype(BF16), rk=r * kmod)
    aas = {k: jnp.where(a_mask, _dot_nt(prep[k]["x_stack"], prep[k]["bk"]), 0.0) for k in cp}
    akv = {}
    pw = {}
    t_inv = {}
    for c, p, h in cph:
        a_rows = aas[c, p][h * L:(h + 1) * L, :]
        vv = jnp.concatenate([prep[c, p]["v_bf"], prep[c, p]["v_bf"]], axis=0)
        akv[c, p, h] = _dot(jnp.where(head0, 0.0, a_rows), vv)
        pw[c, p, h] = a_rows[:, :L]
        t_inv[c, p, h] = eye + pw[c, p, h]
    for _ in range(5):
        for k in cph:
            pw[k] = _dot(pw[k], pw[k])
        for k in cph:
            t_inv[k] = t_inv[k] + _dot(t_inv[k], pw[k])

    states = [s_ref[p] for p in pairs]
    for c in chunks:
        inter = [_dot_nt(prep[c, p]["atrt"], states[p]) for p in pairs]
        u_h = {(p, h): _dot(t_inv[c, p, h], inter[p][:L, :] + akv[c, p, h]) for p, h in heads}
        y_h = {}
        for p, h in heads:
            ar = aas[c, p][(2 + h) * L:(3 + h) * L, :]
            y_h[p, h] = _dot(ar, jnp.concatenate([u_h[p, h].astype(BF16), prep[c, p]["v_bf"]], axis=0))
        new_states = []
        for p in pairs:
            pr = prep[c, p]
            u = jnp.where(head0, u_h[p, 0], u_h[p, 1])
            uv = jnp.concatenate([u.astype(BF16), pr["v_bf"]], axis=0)
            new_states.append(states[p] * pr["w_last"] + jnp.where(blockdiag, _dot_tn(uv, pr["bkhat"]), 0.0))
        states = new_states
        for p in pairs:
            sl = sls[p]
            pr = prep[c, p]
            y = inter[p][L:, :] + jnp.where(head0, y_h[p, 0], y_h[p, 1])
            mean = head_sum(y) * (1.0 / HEAD_DIM)
            d = y - mean
            var = head_sum(d * d) * (1.0 / HEAD_DIM)
            yn = d * lax.rsqrt(var + GN_EPS) * gng_ref[:, sl] + gnb_ref[:, sl]
            bonus = head_sum(pr["rk"] * rk_ref[:, sl]) * pr["v"]
            o_ref[rows[c], sl] = ((yn + bonus) * g_ref[rows[c], sl]).astype(o_ref.dtype)
    for p in pairs:
        s_ref[p] = states[p]


def _rwkv_scan(r, k, v, lw, a, g, k_k, k_a, r_k, gn_g, gn_b):
    T, D = r.shape
    rows = SCAN_CHUNK * SCAN_CHUNKS_PER_STEP
    row = pl.BlockSpec((rows, D), lambda c: (c, 0))
    par = _full((1, D))
    return pl.pallas_call(
        _scan_kernel,
        grid=(T // rows,),
        in_specs=[row] * 6 + [par] * 5,
        out_specs=row,
        out_shape=jax.ShapeDtypeStruct((T, D), BF16),
        scratch_shapes=[pltpu.VMEM((D // PAIR, PAIR, PAIR), F32)],
        compiler_params=_cparams("arbitrary"),
        name="rwkv_scan",
    )(r, k, v, lw, a, g, k_k, k_a, r_k, gn_g, gn_b)


def _swiglu_step(u_ref, wg_ref, wu_ref, wd_ref, acc_ref):
    u = u_ref[...]
    gate = jnp.dot(u, wg_ref[...].astype(BF16), preferred_element_type=F32)
    up = jnp.dot(u, wu_ref[...].astype(BF16), preferred_element_type=F32)
    hid = (gate * _sigmoid(gate) * up).astype(BF16)
    acc_ref[...] += jnp.dot(hid, wd_ref[...].astype(BF16), preferred_element_type=F32)


def _ffn_kernel(h_ref, x_ref, wo_ref, ng_ref, wg_ref, wu_ref, wd_ref, o_ref, u_ref):
    @pl.when(pl.program_id(1) == 0)
    def _():
        h1 = h_ref[...] + jnp.dot(x_ref[...], wo_ref[...], preferred_element_type=F32)
        o_ref[...] = h1
        u_ref[...] = _rms(h1, ng_ref[...]).astype(BF16)

    _swiglu_step(u_ref, wg_ref, wu_ref, wd_ref, o_ref)


def _ffn(h, x, w_o, ng, w_gu, w_down, tm=1024, tf=512):
    T, D = h.shape
    F = w_down.shape[0]
    nf = F // tf
    row = pl.BlockSpec((tm, D), lambda i, f: (i, 0))
    return pl.pallas_call(
        _ffn_kernel,
        grid=(T // tm, nf),
        in_specs=[row, row, _full((D, D)), _full((1, D)),
                  pl.BlockSpec((D, tf), lambda i, f: (0, f)),
                  pl.BlockSpec((D, tf), lambda i, f: (0, f + nf)),
                  pl.BlockSpec((tf, D), lambda i, f: (f, 0))],
        out_specs=row,
        out_shape=jax.ShapeDtypeStruct((T, D), F32),
        scratch_shapes=[pltpu.VMEM((tm, D), BF16)],
        compiler_params=_cparams("parallel", "arbitrary"),
        name="ffn",
    )(h, x, w_o, ng, w_gu, w_gu, w_down)


def _moe_ffn_kernel(be_ref, live_ref, clo_ref, chi_ref, dest_ref, tok_hbm, wg_ref, wu_ref, wd_ref, o_ref,
                    tok_ref, x_ref, sem):
    b = pl.program_id(0)
    f = pl.program_id(1)
    rows = o_ref.shape[0]
    ch = MOE_CHUNK
    halves = rows // MOE_HALF

    @pl.when((b == 0) & (f == 0))
    def _():
        cp = pltpu.make_async_copy(tok_hbm, tok_ref, sem)
        cp.start()
        cp.wait()

    @pl.when(f == 0)
    def _():
        o_ref[...] = jnp.zeros_like(o_ref)
        for s in range(rows // MOE_SUB):
            sub = slice(s * MOE_SUB, (s + 1) * MOE_SUB)
            row_id = b * rows + s * MOE_SUB + lax.broadcasted_iota(jnp.int32, (MOE_SUB, 1), 0)
            w = b * (rows // MOE_SUB) + s

            def chunk(c, _, sub=sub, row_id=row_id):
                start = pl.multiple_of(c * ch, ch)
                d = dest_ref[:, pl.ds(start, ch)]
                onehot = (jnp.where(row_id == d[0:1, :], 1.0, 0.0)
                          + jnp.where(row_id == d[1:2, :], 1.0, 0.0)).astype(BF16)
                o_ref[sub, :] += jnp.dot(onehot, tok_ref[pl.ds(start, ch), :], preferred_element_type=F32)
                return 0

            lax.fori_loop(clo_ref[w], chi_ref[w] + 1, chunk, 0)
        x_ref[...] = o_ref[...].astype(BF16)
        o_ref[...] = jnp.zeros_like(o_ref)

    for half in range(halves):
        @pl.when(live_ref[b * halves + half] > 0)
        def _(half=half):
            part = pl.ds(half * MOE_HALF, MOE_HALF)
            _swiglu_step(x_ref.at[part], wg_ref.at[0], wu_ref.at[0], wd_ref.at[0], o_ref.at[part])


def _moe_ffn(block_e, half_live, c_lo, c_hi, dest_t, tok, w_gu, w_down, n_blocks, tf=512):
    T, D = tok.shape
    F = w_down.shape[1]
    nf = F // tf
    tm = MOE_ROWS

    def fstep(b, f, live):
        return jnp.where(live[b * (tm // MOE_HALF)] > 0, f, nf - 1)

    grid_spec = pltpu.PrefetchScalarGridSpec(
        num_scalar_prefetch=4,
        grid=(n_blocks, nf),
        in_specs=[pl.BlockSpec((2, T), lambda b, f, *_: (0, 0)),
                  pl.BlockSpec(memory_space=pl.ANY),
                  pl.BlockSpec((1, D, tf), lambda b, f, be, live, *_: (be[b], 0, fstep(b, f, live))),
                  pl.BlockSpec((1, D, tf), lambda b, f, be, live, *_: (be[b], 0, fstep(b, f, live) + nf)),
                  pl.BlockSpec((1, tf, D), lambda b, f, be, live, *_: (be[b], fstep(b, f, live), 0))],
        out_specs=pl.BlockSpec((tm, D), lambda b, f, *_: (b, 0)),
        scratch_shapes=[pltpu.VMEM((T, D), BF16), pltpu.VMEM((tm, D), BF16), pltpu.SemaphoreType.DMA(())],
    )
    return pl.pallas_call(
        _moe_ffn_kernel,
        grid_spec=grid_spec,
        out_shape=jax.ShapeDtypeStruct((n_blocks * tm, D), F32),
        compiler_params=_cparams("arbitrary", "arbitrary", vmem_limit=VMEM_LIMIT_MOE),
        name="moe_ffn",
    )(block_e, half_live, c_lo, c_hi, dest_t, tok, w_gu, w_gu, w_down)


def _fox_pre_kernel(h_ref, ng_ref, w_ref, wf_ref, bf_ref, qg_ref, kg_ref, sel_ref, selt_ref, pq_ref, pk_ref,
                    oneq_ref, onek_ref, q_ref, k_ref, v_ref, og_ref, c2_ref, augq_ref, augk_ref, carry_ref):
    i = pl.program_id(0)
    tm, D = h_ref.shape
    H = c2_ref.shape[1]

    @pl.when(i == 0)
    def _():
        carry_ref[...] = jnp.zeros_like(carry_ref)

    u = _rms(h_ref[...], ng_ref[...]).astype(BF16)

    def head_norm(x, gain):
        ms = _dot(x * x, sel_ref[...]) * (1.0 / HEAD_DIM)
        inv = lax.rsqrt(ms + RMS_EPS)
        inv_hi, inv_lo = _split_hi_lo(inv)
        first = lax.broadcasted_iota(jnp.int32, inv.shape, 1) < H
        wide = jnp.dot(jnp.where(first, inv_hi, inv_lo), selt_ref[...], preferred_element_type=F32)
        return x * wide * gain

    q = jnp.dot(u, w_ref[:, 0:D], preferred_element_type=F32)
    q_ref[...] = (head_norm(q, qg_ref[...]) * (HEAD_DIM ** -0.5 * LOG2E)).astype(BF16)
    k = jnp.dot(u, w_ref[:, D:2 * D], preferred_element_type=F32)
    k_ref[...] = head_norm(k, kg_ref[...]).astype(BF16)
    v_ref[...] = jnp.dot(u, w_ref[:, 2 * D:3 * D], preferred_element_type=F32).astype(BF16)
    og_ref[...] = _sigmoid(jnp.dot(u, w_ref[:, 3 * D:4 * D], preferred_element_type=F32)).astype(BF16)

    fl = jnp.dot(u, wf_ref[...], preferred_element_type=F32) + bf_ref[...]
    log_f = jnp.minimum(fl, 0.0) - jnp.log(1.0 + jnp.exp(-jnp.abs(fl)))
    row = lax.broadcasted_iota(jnp.int32, (tm, tm), 0)
    col = lax.broadcasted_iota(jnp.int32, (tm, tm), 1)
    tri = jnp.where(col <= row, 1.0, 0.0).astype(BF16)
    hi, lo = _split_hi_lo(log_f)
    lo2 = (log_f - hi.astype(F32) - lo.astype(F32)).astype(BF16)
    c = (jnp.dot(tri, hi, preferred_element_type=F32) + jnp.dot(tri, lo, preferred_element_type=F32)
         + jnp.dot(tri, lo2, preferred_element_type=F32)) + carry_ref[...]
    carry_ref[...] = c[tm - 1:tm, :]
    c2 = c * LOG2E
    c2_ref[...] = c2[:, 0:H]
    c_hi, c_lo = _split_hi_lo(c2)
    c_lo2 = (c2 - c_hi.astype(F32) - c_lo.astype(F32)).astype(BF16)
    group = lax.broadcasted_iota(jnp.int32, c2.shape, 1) // H
    pieces = jnp.where(group == 0, c_hi, jnp.where(group == 1, c_lo, c_lo2))
    augq_ref[...] = (jnp.dot(pieces, pq_ref[...], preferred_element_type=F32) + oneq_ref[...]).astype(BF16)
    augk_ref[...] = (jnp.dot(pieces, pk_ref[...], preferred_element_type=F32) + onek_ref[...]).astype(BF16)


def _fox_pre(h, ng, w_main, w_f, b_f, q_gain, k_gain, tm=256):
    T, D = h.shape
    H = D // HEAD_DIM
    sel = (jnp.arange(D)[:, None] // HEAD_DIM == jnp.arange(2 * H)[None, :] % H).astype(BF16)
    piece = jnp.arange(3 * H)[:, None] // H
    head = jnp.arange(3 * H)[:, None] % H
    lane = jnp.arange(D)[None, :]
    at = lambda j: lane == head * HEAD_DIM + j
    p_q = at(piece).astype(BF16)
    p_k = -at(piece + 3).astype(BF16)
    off = jnp.arange(D) % HEAD_DIM
    one_q = ((off >= 3) & (off < 6)).astype(F32).reshape(1, D)
    one_k = (off < 3).astype(F32).reshape(1, D)
    row = pl.BlockSpec((tm, D), lambda i: (i, 0))
    ins = [ng, w_main, jnp.tile(w_f, (1, 3)), jnp.tile(b_f, (1, 3)), q_gain, k_gain, sel, sel.T, p_q, p_k,
           one_q, one_k]
    return pl.pallas_call(
        _fox_pre_kernel,
        grid=(T // tm,),
        in_specs=[row] + [_full(a.shape) for a in ins],
        out_specs=[row] * 4 + [pl.BlockSpec((tm, H), lambda i: (i, 0))] + [row] * 2,
        out_shape=([jax.ShapeDtypeStruct((T, D), BF16)] * 4 + [jax.ShapeDtypeStruct((T, H), F32)]
                   + [jax.ShapeDtypeStruct((T, D), BF16)] * 2),
        scratch_shapes=[pltpu.VMEM((1, 3 * H), F32)],
        compiler_params=_cparams("arbitrary"),
        name="fox_pre",
    )(h, *ins)


def _fox_attn_kernel(jlo_ref, q_ref, augq_ref, k_ref, augk_ref, vt_ref, og_ref, o_ref, s_ref, *, blk):
    p_idx = pl.program_id(0)
    i = pl.program_id(1)
    head0 = lax.broadcasted_iota(jnp.int32, (1, PAIR), 1) < HEAD_DIM
    q, augq = q_ref[...], augq_ref[...]
    zero = jnp.zeros_like(q)
    q_heads = [jnp.concatenate([jnp.where(m, q, zero), jnp.where(m, augq, zero)], axis=1)
               for m in (head0, jnp.logical_not(head0))]
    causal = (lax.broadcasted_iota(jnp.int32, (blk, blk), 0)
              <= lax.broadcasted_iota(jnp.int32, (blk, blk), 1))

    def scores_into(j, slot):
        start = pl.multiple_of(j * blk, blk)
        kb = jnp.concatenate([k_ref[pl.ds(start, blk), :], augk_ref[pl.ds(start, blk), :]], axis=1)
        for h in range(2):
            s_ref[slot, h] = lax.dot_general(kb, q_heads[h], (((1,), (1,)), ((), ())),
                                             preferred_element_type=F32)

    def softmax_pv(j, slot, carry, masked):
        start = pl.multiple_of(j * blk, blk)
        out = []
        for h in range(2):
            m_prev, l_prev, acc_prev = carry[h]
            vtb = vt_ref[h * HEAD_DIM:(h + 1) * HEAD_DIM, pl.ds(start, blk)]
            s = s_ref[slot, h]
            if masked:
                s = jnp.where(causal, s, -jnp.inf)
            m_new = jnp.maximum(m_prev, jnp.max(s, axis=0, keepdims=True))
            alpha = jnp.exp2(m_prev - m_new)
            p = jnp.exp2(s - m_new)
            l_new = alpha * l_prev + jnp.sum(p, axis=0, keepdims=True)
            acc_new = alpha * acc_prev + jnp.dot(vtb, p.astype(BF16), preferred_element_type=F32)
            out.append((m_new, l_new, acc_new))
        return tuple(out)

    init = tuple((jnp.full((1, blk), -jnp.inf, F32), jnp.zeros((1, blk), F32), jnp.zeros((HEAD_DIM, blk), F32))
                 for _ in range(2))
    jlo = jlo_ref[p_idx * pl.num_programs(1) + i]
    total = i - jlo + 1
    scores_into(jlo, 0)

    def two_blocks(t, carry):
        j = jlo + 2 * t
        scores_into(j + 1, 1)
        carry = softmax_pv(j, 0, carry, False)
        scores_into(j + 2, 0)
        return softmax_pv(j + 1, 1, carry, False)

    res = lax.fori_loop(0, (total - 1) // 2, two_blocks, init)

    def last_one(carry):
        return softmax_pv(i, 0, carry, True)

    def last_two(carry):
        scores_into(i, 1)
        carry = softmax_pv(i - 1, 0, carry, False)
        return softmax_pv(i, 1, carry, True)

    res = lax.cond(total % 2 == 1, last_one, last_two, res)
    o_t = jnp.concatenate([res[0][2] / res[0][1], res[1][2] / res[1][1]], axis=0)
    o_ref[...] = (o_t.T * og_ref[...].astype(F32)).astype(o_ref.dtype)


def _fox_attn(jlo, q, augq, k, augk, v_t, og, blk):
    T, D = q.shape
    n_pairs = D // PAIR
    qspec = pl.BlockSpec((blk, PAIR), lambda p, i, jlo: (i, p))
    kspec = pl.BlockSpec((T, PAIR), lambda p, i, jlo: (0, p))
    grid_spec = pltpu.PrefetchScalarGridSpec(
        num_scalar_prefetch=1,
        grid=(n_pairs, T // blk),
        in_specs=[qspec, qspec, kspec, kspec,
                  pl.BlockSpec((PAIR, T), lambda p, i, jlo: (p, 0)),
                  qspec],
        out_specs=qspec,
        scratch_shapes=[pltpu.VMEM((2, 2, blk, blk), F32)],
    )
    return pl.pallas_call(
        functools.partial(_fox_attn_kernel, blk=blk),
        grid_spec=grid_spec,
        out_shape=jax.ShapeDtypeStruct((T, D), BF16),
        compiler_params=_cparams("parallel", "arbitrary"),
        name="fox_attn",
    )(jlo, q, augq, k, augk, v_t, og)


def _first_live_block(c2, q_gain, k_gain, blk):
    T, H = c2.shape
    bound = 8.0 * LOG2E * 1.02 * jnp.max(jnp.abs(q_gain)) * jnp.max(jnp.abs(k_gain))
    c_first = c2[0::blk]
    c_last = c2[blk - 1::blk]
    dead = c_last[None, :, :] > c_first[:, None, :] + (2.0 * bound + 104.0 * LOG2E + 1.0)
    jlo = jnp.sum(dead, axis=1).astype(jnp.int32)
    jlo = jnp.min(jlo.reshape(-1, H // 2, 2), axis=-1)
    return jlo.T.reshape(-1)


def _router_kernel(h_ref, x_ref, wo_ref, ng_ref, wr_ref, h1_ref, u_ref, info_ref, cnt_ref, csr_ref, carry_ref):
    i = pl.program_id(0)
    tm = h_ref.shape[0]
    E = N_EXPERTS

    @pl.when(i == 0)
    def _():
        carry_ref[...] = jnp.zeros_like(carry_ref)

    csr_ref[0] = carry_ref[...]
    h1 = h_ref[...] + jnp.dot(x_ref[...], wo_ref[...], preferred_element_type=F32)
    h1_ref[...] = h1
    u = _rms(h1, ng_ref[...])
    u_ref[...] = u.astype(BF16)
    u_hi, u_lo = _split_hi_lo(u)
    w_hi, w_lo = _split_hi_lo(wr_ref[...])
    logits = (jnp.dot(u_hi, w_hi, preferred_element_type=F32)
              + jnp.dot(u_hi, w_lo, preferred_element_type=F32)
              + jnp.dot(u_lo, w_hi, preferred_element_type=F32))
    lane = lax.broadcasted_iota(jnp.int32, (tm, E), 1)
    v1 = jnp.max(logits, axis=-1, keepdims=True)
    e1 = jnp.min(jnp.where(logits == v1, lane, E), axis=-1, keepdims=True)
    rest = jnp.where(lane == e1, -jnp.inf, logits)
    v2 = jnp.max(rest, axis=-1, keepdims=True)
    e2 = jnp.min(jnp.where(rest == v2, lane, E), axis=-1, keepdims=True)
    g1 = 1.0 / (1.0 + jnp.exp(v2 - v1))
    g2 = 1.0 - g1
    oh1 = lane == e1
    oh2 = lane == e2
    onehot = jnp.where(oh1 | oh2, 1.0, 0.0).astype(BF16)
    row = lax.broadcasted_iota(jnp.int32, (tm, tm), 0)
    col = lax.broadcasted_iota(jnp.int32, (tm, tm), 1)
    tri_strict = jnp.where(col < row, 1.0, 0.0).astype(BF16)
    before = jnp.dot(tri_strict, onehot, preferred_element_type=F32) + carry_ref[...]
    rank1 = jnp.sum(jnp.where(oh1, before, 0.0), axis=-1, keepdims=True)
    rank2 = jnp.sum(jnp.where(oh2, before, 0.0), axis=-1, keepdims=True)
    info = jnp.where(lane == 0, e1.astype(F32), 0.0)
    info = jnp.where(lane == 1, e2.astype(F32), info)
    info = jnp.where(lane == 2, g1, info)
    info = jnp.where(lane == 3, g2, info)
    info = jnp.where(lane == 4, rank1, info)
    info = jnp.where(lane == 5, rank2, info)
    info_ref[...] = info
    total = carry_ref[...] + jnp.sum(onehot.astype(F32), axis=0, keepdims=True)
    carry_ref[...] = total
    cnt_ref[...] = total


def _router(h, x, w_o, ng, w_router):
    T, D = h.shape
    E = N_EXPERTS
    tm = min(MOE_CHUNK, T)
    row = pl.BlockSpec((tm, D), lambda i: (i, 0))
    return pl.pallas_call(
        _router_kernel,
        grid=(T // tm,),
        in_specs=[row, row, _full((D, D)), _full((1, D)), _full((D, E))],
        out_specs=[row, row, pl.BlockSpec((tm, E), lambda i: (i, 0)), _full((1, E)),
                   pl.BlockSpec((1, 1, E), lambda i: (i, 0, 0))],
        out_shape=[jax.ShapeDtypeStruct((T, D), F32), jax.ShapeDtypeStruct((T, D), BF16),
                   jax.ShapeDtypeStruct((T, E), F32), jax.ShapeDtypeStruct((1, E), F32),
                   jax.ShapeDtypeStruct((T // tm, 1, E), F32)],
        scratch_shapes=[pltpu.VMEM((1, E), F32)],
        compiler_params=_cparams("arbitrary"),
        name="router",
    )(h, x, w_o, ng, w_router)


def _combine_kernel(pos_ref, h_ref, gate_ref, fg_ref, y_ref, o_ref, ya_ref, yb_ref, sem):
    b = pl.program_id(0)
    nb = pl.num_programs(0)
    rows = h_ref.shape[0]

    def copies(blk, r):
        slot = blk % 2
        base = 2 * (blk * rows + r)
        return (pltpu.make_async_copy(y_ref.at[pl.ds(pos_ref[base], 1)], ya_ref.at[slot, pl.ds(r, 1)],
                                      sem.at[slot]),
                pltpu.make_async_copy(y_ref.at[pl.ds(pos_ref[base + 1], 1)], yb_ref.at[slot, pl.ds(r, 1)],
                                      sem.at[slot]))

    def issue(blk):
        def body(r, _):
            ca, cb = copies(blk, r)
            ca.start()
            cb.start()
            return 0
        lax.fori_loop(0, rows, body, 0, unroll=8)

    @pl.when(b == 0)
    def _():
        issue(0)

    @pl.when(b + 1 < nb)
    def _():
        issue(b + 1)

    def wait(r, _):
        ca, cb = copies(b, r)
        ca.wait()
        cb.wait()
        return 0

    lax.fori_loop(0, rows, wait, 0, unroll=8)
    slot = b % 2
    gate = gate_ref[...]
    moe = gate[:, 0:1] * ya_ref[slot] + gate[:, 1:2] * yb_ref[slot]
    o_ref[...] = _rms(h_ref[...] + moe, fg_ref[...])


def _combine(pos, h, gate, final_g, y, rows=512):
    T, D = h.shape
    row = pl.BlockSpec((rows, D), lambda b, pos: (b, 0))
    grid_spec = pltpu.PrefetchScalarGridSpec(
        num_scalar_prefetch=1,
        grid=(T // rows,),
        in_specs=[row, pl.BlockSpec((rows, 2), lambda b, pos: (b, 0)),
                  pl.BlockSpec((1, D), lambda b, pos: (0, 0)), pl.BlockSpec(memory_space=pl.ANY)],
        out_specs=row,
        scratch_shapes=[pltpu.VMEM((2, rows, D), F32), pltpu.VMEM((2, rows, D), F32),
                        pltpu.SemaphoreType.DMA((2,))],
    )
    return pl.pallas_call(
        _combine_kernel,
        grid_spec=grid_spec,
        out_shape=jax.ShapeDtypeStruct((T, D), F32),
        compiler_params=_cparams("arbitrary"),
        name="moe_combine",
    )(pos, h, gate, final_g, y)


def _rwkv_layer(h, ng, mu, w_rkv, w0, w1, w2, a0, a1, a2, g1, g2, k_k, k_a, r_k, gn_g, gn_b, w_o):
    D = h.shape[1]
    vec = lambda z: z.reshape(1, D)
    b16 = lambda z: z.astype(BF16)
    r, k, v, lw, a, g = _rwkv_pre(h, vec(ng), mu, b16(w_rkv), vec(w0), b16(w1), b16(w2), vec(a0),
                                  b16(a1), b16(a2), b16(g1), b16(g2))
    o = _rwkv_scan(r, k, v, lw, a, g, vec(k_k), vec(k_a), vec(r_k), vec(gn_g), vec(gn_b))
    return o, b16(w_o)


def _fox_layer(h, ng, w_in, b_f, q_gain, k_gain, w_o):
    T, D = h.shape
    H = D // HEAD_DIM
    order = jnp.argsort(b_f)
    w_main = w_in[:, :4 * D].reshape(D, 4, H, HEAD_DIM)[:, :, order, :].reshape(D, 4 * D)
    w_out = w_o.reshape(H, HEAD_DIM, D)[order].reshape(D, D)
    q, k, v, og, c2, augq, augk = _fox_pre(
        h, ng.reshape(1, D), w_main.astype(BF16), w_in[:, 4 * D:][:, order].astype(BF16),
        b_f[order].reshape(1, H), jnp.tile(q_gain, H).reshape(1, D), jnp.tile(k_gain, H).reshape(1, D))
    blk = min(ATTN_BLOCK, T)
    jlo = _first_live_block(c2, q_gain, k_gain, blk)
    o = _fox_attn(jlo, q, augq, k, augk, v.T, og, blk)
    return o, w_out.astype(BF16)


def _moe_layer(h, x, w_o, ng, final_g, w_router, w_gu, w_down):
    T, D = h.shape
    E, RB = N_EXPERTS, MOE_ROWS
    h, u, info, cnt, csr = _router(h, x, w_o, ng.reshape(1, D), w_router)
    e = info[:, 0:2].astype(jnp.int32)
    gate = info[:, 2:4]
    rank = info[:, 4:6].astype(jnp.int32)
    counts = cnt[0].astype(jnp.int32)
    padded = (counts + RB - 1) // RB * RB
    ends = jnp.cumsum(padded)
    starts = ends - padded
    n_blocks = (2 * T) // RB + E
    dest = starts[e] + rank
    blocks = jnp.arange(n_blocks, dtype=jnp.int32)
    block_e = jnp.minimum(jnp.sum(blocks[:, None] * RB >= ends[None, :], axis=1), E - 1).astype(jnp.int32)

    def first_rank(size):
        groups = jnp.arange(n_blocks * (RB // size), dtype=jnp.int32)
        group_e = jnp.repeat(block_e, RB // size)
        rank0 = groups * size - starts[group_e]
        return group_e, rank0, rank0 < counts[group_e]

    _, _, half_live = first_rank(MOE_HALF)
    group_e, rank0, group_live = first_rank(MOE_SUB)
    csr_e = csr[:, 0, :].astype(jnp.int32)[:, group_e]
    n_chunks = csr_e.shape[0]
    c_lo = jnp.clip(jnp.sum(csr_e <= rank0[None, :], axis=0) - 1, 0, n_chunks - 1)
    c_hi = jnp.clip(jnp.sum(csr_e < rank0[None, :] + MOE_SUB, axis=0) - 1, 0, n_chunks - 1)
    c_lo = jnp.where(group_live, c_lo, 1).astype(jnp.int32)
    c_hi = jnp.where(group_live, c_hi, 0).astype(jnp.int32)
    y = _moe_ffn(block_e, half_live.astype(jnp.int32), c_lo, c_hi, dest.T, u, w_gu, w_down, n_blocks)
    return _combine(dest.reshape(-1), h, gate, final_g.reshape(1, D), y)


def kernel(x, norm_g, final_g, rwkv_mu, rwkv_w_rkv, rwkv_w0, rwkv_w1, rwkv_w2, rwkv_a0, rwkv_a1,
           rwkv_a2, rwkv_g1, rwkv_g2, rwkv_k_k, rwkv_k_a, rwkv_r_k, rwkv_gn_g, rwkv_gn_b, rwkv_w_o,
           fox_w_in, fox_b_f, fox_q_gain, fox_k_gain, fox_w_o, ffn_w_gu, ffn_w_down,
           moe_w_router, moe_w_gu, moe_w_down):
    B, T, D = x.shape
    assert B == 1 and norm_g.shape[0] == 2 and rwkv_mu.shape[0] == 1 and fox_w_in.shape[0] == 1, \
        "one sequence; one RWKV-7 layer followed by one forgetting-attention layer"
    h = x.reshape(B * T, D)
    first = lambda z: z.reshape(z.shape[1:])
    o, w_o = _rwkv_layer(h, norm_g[0, 0], *map(first, (
        rwkv_mu, rwkv_w_rkv, rwkv_w0, rwkv_w1, rwkv_w2, rwkv_a0, rwkv_a1, rwkv_a2, rwkv_g1, rwkv_g2, rwkv_k_k,
        rwkv_k_a, rwkv_r_k, rwkv_gn_g, rwkv_gn_b, rwkv_w_o)))
    h = _ffn(h, o, w_o, norm_g[0, 1].reshape(1, D), first(ffn_w_gu).astype(BF16), first(ffn_w_down).astype(BF16))
    o, w_o = _fox_layer(h, norm_g[1, 0], *map(first, (fox_w_in, fox_b_f, fox_q_gain, fox_k_gain, fox_w_o)))
    out = _moe_layer(h, o, w_o, norm_g[1, 1], final_g, first(moe_w_router), first(moe_w_gu), first(moe_w_down))
    return out.reshape(B, T, D)
```

```python
import functools

import jax
import jax.numpy as jnp
from jax import lax
from jax.experimental import pallas as pl
from jax.experimental.pallas import tpu as pltpu

F32 = jnp.float32
BF16 = jnp.bfloat16

HEAD_DIM = 64
PAIR = 2 * HEAD_DIM
GN_EPS = 64e-5
RMS_EPS = 1e-6
N_EXPERTS = 8
SCAN_CHUNK = 64
SCAN_CHUNKS_PER_STEP = 2
MOE_ROWS = 1024
MOE_HALF = 512
MOE_SUB = 128
MOE_CHUNK = 512
ROUTER_PARTS = 2
ATTN_BLOCK = 512
LOG2E = 1.4426950408889634
VMEM_LIMIT = 56 * 1024 * 1024
VMEM_LIMIT_MOE = 60 * 1024 * 1024


def _cparams(*sem, vmem_limit=VMEM_LIMIT):
    return pltpu.CompilerParams(dimension_semantics=sem, vmem_limit_bytes=vmem_limit)


def _dot(a, b):
    return jnp.dot(a.astype(BF16), b.astype(BF16), preferred_element_type=F32)


def _dot_nt(a, b):
    return lax.dot_general(a.astype(BF16), b.astype(BF16), (((1,), (1,)), ((), ())),
                           preferred_element_type=F32)


def _dot_tn(a, b):
    return lax.dot_general(a.astype(BF16), b.astype(BF16), (((0,), (0,)), ((), ())),
                           preferred_element_type=F32)


def _split_hi_lo(x):
    hi = x.astype(BF16)
    lo = (x - hi.astype(F32)).astype(BF16)
    return hi, lo


def _rms(x, g):
    ms = jnp.mean(x * x, axis=-1, keepdims=True)
    return x * lax.rsqrt(ms + RMS_EPS) * g


def _sigmoid(x):
    return 1.0 / (1.0 + jnp.exp(-x))


def _full(shape):
    return pl.BlockSpec(shape, lambda *_: (0,) * len(shape))


def _rwkv_pre_kernel(h_ref, hp_ref, ng_ref, mu_ref, wrkv_ref, w0_ref, w1_ref, w2_ref, a0_ref,
                     a1_ref, a2_ref, g1_ref, g2_ref, r_ref, k_ref, v_ref, lw_ref, a_ref, g_ref):
    i = pl.program_id(0)
    ng = ng_ref[...]
    u = _rms(h_ref[...], ng)
    up = _rms(hp_ref[7:8, :], ng)
    up = jnp.where(i == 0, 0.0, up)
    row = lax.broadcasted_iota(jnp.int32, u.shape, 0)
    u_prev = jnp.where(row == 0, up, pltpu.roll(u, 1, axis=0))
    dx = u_prev - u
    mix = lambda n: u + dx * mu_ref[n:n + 1, :]
    w_mid = jnp.tanh(_dot(mix(3), w1_ref[...]))
    a_mid = _dot(mix(4), a1_ref[...])
    g_mid = _sigmoid(_dot(mix(5), g1_ref[...]))
    r_ref[...] = _dot(mix(0), wrkv_ref[0])
    k_ref[...] = _dot(mix(1), wrkv_ref[1])
    v_ref[...] = _dot(mix(2), wrkv_ref[2])
    z = w0_ref[...] + _dot(w_mid, w2_ref[...])
    softplus_neg = jnp.maximum(-z, 0.0) + jnp.log(1.0 + jnp.exp(-jnp.abs(z)))
    lw_ref[...] = -jnp.exp(-softplus_neg - 0.5)
    a_ref[...] = _sigmoid(a0_ref[...] + _dot(a_mid, a2_ref[...]))
    g_ref[...] = _dot(g_mid, g2_ref[...])


def _rwkv_pre(h, ng, mu, wrkv, w0, w1, w2, a0, a1, a2, g1, g2, tm=512):
    T, D = h.shape
    row = pl.BlockSpec((tm, D), lambda i: (i, 0))
    prev = pl.BlockSpec((8, D), lambda i: (jnp.maximum(i * (tm // 8) - 1, 0), 0))
    ins = [ng, mu, wrkv, w0, w1, w2, a0, a1, a2, g1, g2]
    return pl.pallas_call(
        _rwkv_pre_kernel,
        grid=(T // tm,),
        in_specs=[row, prev] + [_full(a.shape) for a in ins],
        out_specs=[row] * 6,
        out_shape=[jax.ShapeDtypeStruct((T, D), F32)] * 6,
        compiler_params=_cparams("parallel"),
        name="rwkv_pre",
    )(h, h, *ins)


def _scan_kernel(r_ref, k_ref, v_ref, lw_ref, a_ref, g_ref, kk_ref, ka_ref, rk_ref, gng_ref,
                 gnb_ref, o_ref, s_ref):
    L = SCAN_CHUNK
    n_pairs = r_ref.shape[1] // PAIR

    @pl.when(pl.program_id(0) == 0)
    def _():
        s_ref[...] = jnp.zeros_like(s_ref)

    row = lax.broadcasted_iota(jnp.int32, (L, L), 0)
    col = lax.broadcasted_iota(jnp.int32, (L, L), 1)
    tri_incl = jnp.where(col <= row, 1.0, 0.0).astype(BF16)
    eye = jnp.where(col == row, 1.0, 0.0).astype(F32)
    lane = lax.broadcasted_iota(jnp.int32, (1, PAIR), 1)
    head0 = lane < HEAD_DIM
    arow = lax.broadcasted_iota(jnp.int32, (4 * L, PAIR), 0)
    acol = lax.broadcasted_iota(jnp.int32, (4 * L, PAIR), 1)
    t_idx = arow % L + jnp.where(arow < 2 * L, 0, 1)
    j_idx = acol % L
    a_mask = j_idx < t_idx
    brow = lax.broadcasted_iota(jnp.int32, (PAIR, PAIR), 0)
    bcol = lax.broadcasted_iota(jnp.int32, (PAIR, PAIR), 1)
    blockdiag = (brow < HEAD_DIM) == (bcol < HEAD_DIM)

    def head_sum(x):
        s0 = jnp.sum(jnp.where(head0, x, 0.0), axis=-1, keepdims=True)
        s1 = jnp.sum(jnp.where(head0, 0.0, x), axis=-1, keepdims=True)
        return jnp.where(head0, s0, s1)

    pairs = range(n_pairs)
    heads = [(p, h) for p in pairs for h in range(2)]
    sls = [slice(p * PAIR, (p + 1) * PAIR) for p in pairs]
    n_chunks = r_ref.shape[0] // L
    chunks = range(n_chunks)
    rows = [slice(c * L, (c + 1) * L) for c in chunks]
    cp = [(c, p) for c in chunks for p in pairs]
    cph = [(c, p, h) for c in chunks for p, h in heads]

    lws = {(c, p): lw_ref[rows[c], sls[p]] for c, p in cp}
    splits = {k: _split_hi_lo(lw) for k, lw in lws.items()}
    css = {k: jnp.dot(tri_incl, hi, preferred_element_type=F32) + jnp.dot(tri_incl, lo, preferred_element_type=F32)
           for k, (hi, lo) in splits.items()}
    prep = {}
    for c, p in cp:
        rw, sl, cs, lw = rows[c], sls[p], css[c, p], lws[c, p]
        w_inv = jnp.exp(-cs)
        w_last = jnp.exp(cs[L - 1:L, :])
        r = r_ref[rw, sl]
        k = k_ref[rw, sl]
        a = a_ref[rw, sl]
        kk = k * kk_ref[:, sl]
        kk = kk / jnp.maximum(jnp.sqrt(head_sum(kk * kk)), 1e-12)
        kmod = k * (1.0 + (a - 1.0) * ka_ref[:, sl])
        at = -kk * jnp.exp(cs - lw)
        bt = kk * a * w_inv
        kt = kmod * w_inv
        rt = r * jnp.exp(cs)
        v = v_ref[rw, sl]
        prep[c, p] = dict(atrt=jnp.concatenate([at, rt], axis=0).astype(BF16),
                          x_stack=jnp.concatenate([jnp.where(head0, at, 0.0), jnp.where(head0, 0.0, at),
                                                   jnp.where(head0, rt, 0.0), jnp.where(head0, 0.0, rt)],
                                                  axis=0).astype(BF16),
                          bk=jnp.concatenate([bt, kt], axis=0).astype(BF16),
                          bkhat=jnp.concatenate([bt * w_last, kt * w_last], axis=0).astype(BF16),
                          w_last=w_last, v=v, v_bf=v.astype(BF16), rk=r * kmod)
    aas = {k: jnp.where(a_mask, _dot_nt(prep[k]["x_stack"], prep[k]["bk"]), 0.0) for k in cp}
    akv = {}
    pw = {}
    t_inv = {}
    for c, p, h in cph:
        a_rows = aas[c, p][h * L:(h + 1) * L, :]
        vv = jnp.concatenate([prep[c, p]["v_bf"], prep[c, p]["v_bf"]], axis=0)
        akv[c, p, h] = _dot(jnp.where(head0, 0.0, a_rows), vv)
        pw[c, p, h] = a_rows[:, :L]
        t_inv[c, p, h] = eye + pw[c, p, h]
    for _ in range(5):
        for k in cph:
            pw[k] = _dot(pw[k], pw[k])
        for k in cph:
            t_inv[k] = t_inv[k] + _dot(t_inv[k], pw[k])

    states = [s_ref[p] for p in pairs]
    for c in chunks:
        inter = [_dot_nt(prep[c, p]["atrt"], states[p]) for p in pairs]
        u_h = {(p, h): _dot(t_inv[c, p, h], inter[p][:L, :] + akv[c, p, h]) for p, h in heads}
        y_h = {}
        for p, h in heads:
            ar = aas[c, p][(2 + h) * L:(3 + h) * L, :]
            y_h[p, h] = _dot(ar, jnp.concatenate([u_h[p, h].astype(BF16), prep[c, p]["v_bf"]], axis=0))
        new_states = []
        for p in pairs:
            pr = prep[c, p]
            u = jnp.where(head0, u_h[p, 0], u_h[p, 1])
            uv = jnp.concatenate([u.astype(BF16), pr["v_bf"]], axis=0)
            new_states.append(states[p] * pr["w_last"] + jnp.where(blockdiag, _dot_tn(uv, pr["bkhat"]), 0.0))
        states = new_states
        for p in pairs:
            sl = sls[p]
            pr = prep[c, p]
            y = inter[p][L:, :] + jnp.where(head0, y_h[p, 0], y_h[p, 1])
            mean = head_sum(y) * (1.0 / HEAD_DIM)
            d = y - mean
            var = head_sum(d * d) * (1.0 / HEAD_DIM)
            yn = d * lax.rsqrt(var + GN_EPS) * gng_ref[:, sl] + gnb_ref[:, sl]
            bonus = head_sum(pr["rk"] * rk_ref[:, sl]) * pr["v"]
            o_ref[rows[c], sl] = ((yn + bonus) * g_ref[rows[c], sl]).astype(o_ref.dtype)
    for p in pairs:
        s_ref[p] = states[p]


def _rwkv_scan(r, k, v, lw, a, g, k_k, k_a, r_k, gn_g, gn_b):
    T, D = r.shape
    rows = SCAN_CHUNK * SCAN_CHUNKS_PER_STEP
    row = pl.BlockSpec((rows, D), lambda c: (c, 0))
    par = _full((1, D))
    return pl.pallas_call(
        _scan_kernel,
        grid=(T // rows,),
        in_specs=[row] * 6 + [par] * 5,
        out_specs=row,
        out_shape=jax.ShapeDtypeStruct((T, D), BF16),
        scratch_shapes=[pltpu.VMEM((D // PAIR, PAIR, PAIR), F32)],
        compiler_params=_cparams("arbitrary"),
        name="rwkv_scan",
    )(r, k, v, lw, a, g, k_k, k_a, r_k, gn_g, gn_b)


def _swiglu_step(u_ref, wg_ref, wu_ref, wd_ref, acc_ref):
    u = u_ref[...]
    gate = jnp.dot(u, wg_ref[...].astype(BF16), preferred_element_type=F32)
    up = jnp.dot(u, wu_ref[...].astype(BF16), preferred_element_type=F32)
    hid = (gate * _sigmoid(gate) * up).astype(BF16)
    acc_ref[...] += jnp.dot(hid, wd_ref[...].astype(BF16), preferred_element_type=F32)


def _ffn_kernel(h_ref, x_ref, wo_ref, ng_ref, wg_ref, wu_ref, wd_ref, o_ref, u_ref):
    @pl.when(pl.program_id(1) == 0)
    def _():
        h1 = h_ref[...] + jnp.dot(x_ref[...], wo_ref[...], preferred_element_type=F32)
        o_ref[...] = h1
        u_ref[...] = _rms(h1, ng_ref[...]).astype(BF16)

    _swiglu_step(u_ref, wg_ref, wu_ref, wd_ref, o_ref)


def _ffn(h, x, w_o, ng, w_gu, w_down, tm=1024, tf=512):
    T, D = h.shape
    F = w_down.shape[0]
    nf = F // tf
    row = pl.BlockSpec((tm, D), lambda i, f: (i, 0))
    return pl.pallas_call(
        _ffn_kernel,
        grid=(T // tm, nf),
        in_specs=[row, row, _full((D, D)), _full((1, D)),
                  pl.BlockSpec((D, tf), lambda i, f: (0, f)),
                  pl.BlockSpec((D, tf), lambda i, f: (0, f + nf)),
                  pl.BlockSpec((tf, D), lambda i, f: (f, 0))],
        out_specs=row,
        out_shape=jax.ShapeDtypeStruct((T, D), F32),
        scratch_shapes=[pltpu.VMEM((tm, D), BF16)],
        compiler_params=_cparams("parallel", "arbitrary"),
        name="ffn",
    )(h, x, w_o, ng, w_gu, w_gu, w_down)


def _moe_ffn_kernel(be_ref, live_ref, clo_ref, chi_ref, dest_ref, tok_hbm, wg_ref, wu_ref, wd_ref, o_ref,
                    tok_ref, x_ref, sem):
    b = pl.program_id(0)
    f = pl.program_id(1)
    rows = o_ref.shape[0]
    ch = MOE_CHUNK
    halves = rows // MOE_HALF

    @pl.when((b == 0) & (f == 0))
    def _():
        cp = pltpu.make_async_copy(tok_hbm, tok_ref, sem)
        cp.start()
        cp.wait()

    @pl.when(f == 0)
    def _():
        o_ref[...] = jnp.zeros_like(o_ref)
        for s in range(rows // MOE_SUB):
            sub = slice(s * MOE_SUB, (s + 1) * MOE_SUB)
            row_id = b * rows + s * MOE_SUB + lax.broadcasted_iota(jnp.int32, (MOE_SUB, 1), 0)
            w = b * (rows // MOE_SUB) + s

            def chunk(c, _, sub=sub, row_id=row_id):
                start = pl.multiple_of(c * ch, ch)
                d = dest_ref[:, pl.ds(start, ch)]
                onehot = (jnp.where(row_id == d[0:1, :], 1.0, 0.0)
                          + jnp.where(row_id == d[1:2, :], 1.0, 0.0)).astype(BF16)
                o_ref[sub, :] += jnp.dot(onehot, tok_ref[pl.ds(start, ch), :], preferred_element_type=F32)
                return 0

            lax.fori_loop(clo_ref[w], chi_ref[w] + 1, chunk, 0)
        x_ref[...] = o_ref[...].astype(BF16)
        o_ref[...] = jnp.zeros_like(o_ref)

    for half in range(halves):
        @pl.when(live_ref[b * halves + half] > 0)
        def _(half=half):
            part = pl.ds(half * MOE_HALF, MOE_HALF)
            _swiglu_step(x_ref.at[part], wg_ref.at[0], wu_ref.at[0], wd_ref.at[0], o_ref.at[part])


def _moe_ffn(block_e, half_live, c_lo, c_hi, dest_t, tok, w_gu, w_down, n_blocks, tf=512):
    T, D = tok.shape
    F = w_down.shape[1]
    nf = F // tf
    tm = MOE_ROWS

    def fstep(b, f, live):
        return jnp.where(live[b * (tm // MOE_HALF)] > 0, f, nf - 1)

    grid_spec = pltpu.PrefetchScalarGridSpec(
        num_scalar_prefetch=4,
        grid=(n_blocks, nf),
        in_specs=[pl.BlockSpec((2, T), lambda b, f, *_: (0, 0)),
                  pl.BlockSpec(memory_space=pl.ANY),
                  pl.BlockSpec((1, D, tf), lambda b, f, be, live, *_: (be[b], 0, fstep(b, f, live))),
                  pl.BlockSpec((1, D, tf), lambda b, f, be, live, *_: (be[b], 0, fstep(b, f, live) + nf)),
                  pl.BlockSpec((1, tf, D), lambda b, f, be, live, *_: (be[b], fstep(b, f, live), 0))],
        out_specs=pl.BlockSpec((tm, D), lambda b, f, *_: (b, 0)),
        scratch_shapes=[pltpu.VMEM((T, D), BF16), pltpu.VMEM((tm, D), BF16), pltpu.SemaphoreType.DMA(())],
    )
    return pl.pallas_call(
        _moe_ffn_kernel,
        grid_spec=grid_spec,
        out_shape=jax.ShapeDtypeStruct((n_blocks * tm, D), F32),
        compiler_params=_cparams("arbitrary", "arbitrary", vmem_limit=VMEM_LIMIT_MOE),
        name="moe_ffn",
    )(block_e, half_live, c_lo, c_hi, dest_t, tok, w_gu, w_gu, w_down)


def _fox_pre_kernel(h_ref, ng_ref, w_ref, wf_ref, bf_ref, qg_ref, kg_ref, sel_ref, selt_ref, pq_ref, pk_ref,
                    oneq_ref, onek_ref, q_ref, k_ref, v_ref, og_ref, c2_ref, augq_ref, augk_ref, carry_ref):
    i = pl.program_id(0)
    tm, D = h_ref.shape
    H = c2_ref.shape[1]

    @pl.when(i == 0)
    def _():
        carry_ref[...] = jnp.zeros_like(carry_ref)

    u = _rms(h_ref[...], ng_ref[...]).astype(BF16)

    q = jnp.dot(u, w_ref[:, 0:D], preferred_element_type=F32)
    k = jnp.dot(u, w_ref[:, D:2 * D], preferred_element_type=F32)
    fl = jnp.dot(u, wf_ref[...], preferred_element_type=F32) + bf_ref[...]
    v_ref[...] = jnp.dot(u, w_ref[:, 2 * D:3 * D], preferred_element_type=F32).astype(BF16)
    og_ref[...] = _sigmoid(jnp.dot(u, w_ref[:, 3 * D:4 * D], preferred_element_type=F32)).astype(BF16)

    mean_sq = [_dot(x * x, sel_ref[...]) * (1.0 / HEAD_DIM) for x in (q, k)]

    log_f = jnp.minimum(fl, 0.0) - jnp.log(1.0 + jnp.exp(-jnp.abs(fl)))
    row = lax.broadcasted_iota(jnp.int32, (tm, tm), 0)
    col = lax.broadcasted_iota(jnp.int32, (tm, tm), 1)
    tri = jnp.where(col <= row, 1.0, 0.0).astype(BF16)
    hi, lo = _split_hi_lo(log_f)
    lo2 = (log_f - hi.astype(F32) - lo.astype(F32)).astype(BF16)
    c = (jnp.dot(tri, hi, preferred_element_type=F32) + jnp.dot(tri, lo, preferred_element_type=F32)
         + jnp.dot(tri, lo2, preferred_element_type=F32)) + carry_ref[...]
    carry_ref[...] = c[tm - 1:tm, :]

    def spread(ms):
        inv = lax.rsqrt(ms + RMS_EPS)
        inv_hi, inv_lo = _split_hi_lo(inv)
        first = lax.broadcasted_iota(jnp.int32, inv.shape, 1) < H
        return jnp.dot(jnp.where(first, inv_hi, inv_lo), selt_ref[...], preferred_element_type=F32)

    q_inv, k_inv = [spread(ms) for ms in mean_sq]
    q_ref[...] = (q * q_inv * qg_ref[...] * (HEAD_DIM ** -0.5 * LOG2E)).astype(BF16)
    k_ref[...] = (k * k_inv * kg_ref[...]).astype(BF16)

    c2 = c * LOG2E
    c2_ref[...] = c2[:, 0:H]
    c_hi, c_lo = _split_hi_lo(c2)
    c_lo2 = (c2 - c_hi.astype(F32) - c_lo.astype(F32)).astype(BF16)
    group = lax.broadcasted_iota(jnp.int32, c2.shape, 1) // H
    pieces = jnp.where(group == 0, c_hi, jnp.where(group == 1, c_lo, c_lo2))
    augq_ref[...] = (jnp.dot(pieces, pq_ref[...], preferred_element_type=F32) + oneq_ref[...]).astype(BF16)
    augk_ref[...] = (jnp.dot(pieces, pk_ref[...], preferred_element_type=F32) + onek_ref[...]).astype(BF16)


def _fox_pre(h, ng, w_main, w_f, b_f, q_gain, k_gain, tm=256):
    T, D = h.shape
    H = D // HEAD_DIM
    sel = (jnp.arange(D)[:, None] // HEAD_DIM == jnp.arange(2 * H)[None, :] % H).astype(BF16)
    piece = jnp.arange(3 * H)[:, None] // H
    head = jnp.arange(3 * H)[:, None] % H
    lane = jnp.arange(D)[None, :]
    at = lambda j: lane == head * HEAD_DIM + j
    p_q = at(piece).astype(BF16)
    p_k = -at(piece + 3).astype(BF16)
    off = jnp.arange(D) % HEAD_DIM
    one_q = ((off >= 3) & (off < 6)).astype(F32).reshape(1, D)
    one_k = (off < 3).astype(F32).reshape(1, D)
    row = pl.BlockSpec((tm, D), lambda i: (i, 0))
    ins = [ng, w_main, jnp.tile(w_f, (1, 3)), jnp.tile(b_f, (1, 3)), q_gain, k_gain, sel, sel.T, p_q, p_k,
           one_q, one_k]
    return pl.pallas_call(
        _fox_pre_kernel,
        grid=(T // tm,),
        in_specs=[row] + [_full(a.shape) for a in ins],
        out_specs=[row] * 4 + [pl.BlockSpec((tm, H), lambda i: (i, 0))] + [row] * 2,
        out_shape=([jax.ShapeDtypeStruct((T, D), BF16)] * 4 + [jax.ShapeDtypeStruct((T, H), F32)]
                   + [jax.ShapeDtypeStruct((T, D), BF16)] * 2),
        scratch_shapes=[pltpu.VMEM((1, 3 * H), F32)],
        compiler_params=_cparams("arbitrary"),
        name="fox_pre",
    )(h, *ins)


def _fox_attn_kernel(jlo_ref, q_ref, augq_ref, k_ref, augk_ref, vt_ref, og_ref, o_ref, s_ref, *, blk):
    p_idx = pl.program_id(0)
    i = pl.program_id(1)
    head0 = lax.broadcasted_iota(jnp.int32, (1, PAIR), 1) < HEAD_DIM
    q, augq = q_ref[...], augq_ref[...]
    zero = jnp.zeros_like(q)
    q_heads = [jnp.concatenate([jnp.where(m, q, zero), jnp.where(m, augq, zero)], axis=1)
               for m in (head0, jnp.logical_not(head0))]
    causal = (lax.broadcasted_iota(jnp.int32, (blk, blk), 0)
              <= lax.broadcasted_iota(jnp.int32, (blk, blk), 1))

    def scores_into(j, slot):
        start = pl.multiple_of(j * blk, blk)
        kb = jnp.concatenate([k_ref[pl.ds(start, blk), :], augk_ref[pl.ds(start, blk), :]], axis=1)
        for h in range(2):
            s_ref[slot, h] = lax.dot_general(kb, q_heads[h], (((1,), (1,)), ((), ())),
                                             preferred_element_type=F32)

    def softmax_pv(j, slot, carry, masked):
        start = pl.multiple_of(j * blk, blk)
        out = []
        for h in range(2):
            m_prev, l_prev, acc_prev = carry[h]
            vtb = vt_ref[h * HEAD_DIM:(h + 1) * HEAD_DIM, pl.ds(start, blk)]
            s = s_ref[slot, h]
            if masked:
                s = jnp.where(causal, s, -jnp.inf)
            m_new = jnp.maximum(m_prev, jnp.max(s, axis=0, keepdims=True))
            alpha = jnp.exp2(m_prev - m_new)
            p = jnp.exp2(s - m_new)
            l_new = alpha * l_prev + jnp.sum(p, axis=0, keepdims=True)
            acc_new = alpha * acc_prev + jnp.dot(vtb, p.astype(BF16), preferred_element_type=F32)
            out.append((m_new, l_new, acc_new))
        return tuple(out)

    init = tuple((jnp.full((1, blk), -jnp.inf, F32), jnp.zeros((1, blk), F32), jnp.zeros((HEAD_DIM, blk), F32))
                 for _ in range(2))
    jlo = jlo_ref[p_idx * pl.num_programs(1) + i]
    total = i - jlo + 1
    scores_into(jlo, 0)

    def two_blocks(t, carry):
        j = jlo + 2 * t
        scores_into(j + 1, 1)
        carry = softmax_pv(j, 0, carry, False)
        scores_into(j + 2, 0)
        return softmax_pv(j + 1, 1, carry, False)

    res = lax.fori_loop(0, (total - 1) // 2, two_blocks, init)

    def last_one(carry):
        return softmax_pv(i, 0, carry, True)

    def last_two(carry):
        scores_into(i, 1)
        carry = softmax_pv(i - 1, 0, carry, False)
        return softmax_pv(i, 1, carry, True)

    res = lax.cond(total % 2 == 1, last_one, last_two, res)
    o_t = jnp.concatenate([res[0][2] / res[0][1], res[1][2] / res[1][1]], axis=0)
    o_ref[...] = (o_t.T * og_ref[...].astype(F32)).astype(o_ref.dtype)


def _fox_attn(jlo, q, augq, k, augk, v_t, og, blk):
    T, D = q.shape
    n_pairs = D // PAIR
    qspec = pl.BlockSpec((blk, PAIR), lambda p, i, jlo: (i, p))
    kspec = pl.BlockSpec((T, PAIR), lambda p, i, jlo: (0, p))
    grid_spec = pltpu.PrefetchScalarGridSpec(
        num_scalar_prefetch=1,
        grid=(n_pairs, T // blk),
        in_specs=[qspec, qspec, kspec, kspec,
                  pl.BlockSpec((PAIR, T), lambda p, i, jlo: (p, 0)),
                  qspec],
        out_specs=qspec,
        scratch_shapes=[pltpu.VMEM((2, 2, blk, blk), F32)],
    )
    return pl.pallas_call(
        functools.partial(_fox_attn_kernel, blk=blk),
        grid_spec=grid_spec,
        out_shape=jax.ShapeDtypeStruct((T, D), BF16),
        compiler_params=_cparams("parallel", "arbitrary"),
        name="fox_attn",
    )(jlo, q, augq, k, augk, v_t, og)


def _first_live_block(c2, q_gain, k_gain, blk):
    T, H = c2.shape
    bound = 8.0 * LOG2E * 1.02 * jnp.max(jnp.abs(q_gain)) * jnp.max(jnp.abs(k_gain))
    c_first = c2[0::blk]
    c_last = c2[blk - 1::blk]
    dead = c_last[None, :, :] > c_first[:, None, :] + (2.0 * bound + 104.0 * LOG2E + 1.0)
    jlo = jnp.sum(dead, axis=1).astype(jnp.int32)
    jlo = jnp.min(jlo.reshape(-1, H // 2, 2), axis=-1)
    return jlo.T.reshape(-1)


def _router_kernel(h_ref, x_ref, wo_ref, ng_ref, wr_ref, h1_ref, u_ref, info_ref, cnt_ref, csr_ref, carry_ref):
    i = pl.program_id(0)
    tm = h_ref.shape[0]
    E = N_EXPERTS
    parts = ROUTER_PARTS if tm % (8 * ROUTER_PARTS) == 0 else 1
    pm = tm // parts
    rows = [slice(j * pm, (j + 1) * pm) for j in range(parts)]

    @pl.when(i == 0)
    def _():
        carry_ref[...] = jnp.zeros_like(carry_ref)

    csr_ref[0] = carry_ref[...]
    h1 = [h_ref[r, :] + jnp.dot(x_ref[r, :], wo_ref[...], preferred_element_type=F32) for r in rows]
    for r, z in zip(rows, h1):
        h1_ref[r, :] = z
    u = [_rms(z, ng_ref[...]) for z in h1]
    for r, z in zip(rows, u):
        u_ref[r, :] = z.astype(BF16)
    w_hi, w_lo = _split_hi_lo(wr_ref[...])
    logits = []
    for z in u:
        u_hi, u_lo = _split_hi_lo(z)
        logits.append(jnp.dot(u_hi, w_hi, preferred_element_type=F32)
                      + jnp.dot(u_hi, w_lo, preferred_element_type=F32)
                      + jnp.dot(u_lo, w_hi, preferred_element_type=F32))
    lane = lax.broadcasted_iota(jnp.int32, (pm, E), 1)
    row = lax.broadcasted_iota(jnp.int32, (pm, pm), 0)
    col = lax.broadcasted_iota(jnp.int32, (pm, pm), 1)
    tri_strict = jnp.where(col < row, 1.0, 0.0).astype(BF16)
    picks = []
    for lg in logits:
        v1 = jnp.max(lg, axis=-1, keepdims=True)
        e1 = jnp.min(jnp.where(lg == v1, lane, E), axis=-1, keepdims=True)
        rest = jnp.where(lane == e1, -jnp.inf, lg)
        v2 = jnp.max(rest, axis=-1, keepdims=True)
        e2 = jnp.min(jnp.where(rest == v2, lane, E), axis=-1, keepdims=True)
        g1 = 1.0 / (1.0 + jnp.exp(v2 - v1))
        oh1 = lane == e1
        oh2 = lane == e2
        onehot = jnp.where(oh1 | oh2, 1.0, 0.0).astype(BF16)
        picks.append((e1, e2, g1, oh1, oh2, onehot))
    within = [jnp.dot(tri_strict, p[5], preferred_element_type=F32) for p in picks]
    counted = carry_ref[...]
    for r, (e1, e2, g1, oh1, oh2, onehot), inside in zip(rows, picks, within):
        before = inside + counted
        rank1 = jnp.sum(jnp.where(oh1, before, 0.0), axis=-1, keepdims=True)
        rank2 = jnp.sum(jnp.where(oh2, before, 0.0), axis=-1, keepdims=True)
        info = jnp.where(lane == 0, e1.astype(F32), 0.0)
        info = jnp.where(lane == 1, e2.astype(F32), info)
        info = jnp.where(lane == 2, g1, info)
        info = jnp.where(lane == 3, 1.0 - g1, info)
        info = jnp.where(lane == 4, rank1, info)
        info = jnp.where(lane == 5, rank2, info)
        info_ref[r, :] = info
        counted = counted + jnp.sum(onehot.astype(F32), axis=0, keepdims=True)
    carry_ref[...] = counted
    cnt_ref[...] = counted


def _router(h, x, w_o, ng, w_router):
    T, D = h.shape
    E = N_EXPERTS
    tm = min(MOE_CHUNK, T)
    row = pl.BlockSpec((tm, D), lambda i: (i, 0))
    return pl.pallas_call(
        _router_kernel,
        grid=(T // tm,),
        in_specs=[row, row, _full((D, D)), _full((1, D)), _full((D, E))],
        out_specs=[row, row, pl.BlockSpec((tm, E), lambda i: (i, 0)), _full((1, E)),
                   pl.BlockSpec((1, 1, E), lambda i: (i, 0, 0))],
        out_shape=[jax.ShapeDtypeStruct((T, D), F32), jax.ShapeDtypeStruct((T, D), BF16),
                   jax.ShapeDtypeStruct((T, E), F32), jax.ShapeDtypeStruct((1, E), F32),
                   jax.ShapeDtypeStruct((T // tm, 1, E), F32)],
        scratch_shapes=[pltpu.VMEM((1, E), F32)],
        compiler_params=_cparams("arbitrary"),
        name="router",
    )(h, x, w_o, ng, w_router)


def _combine_kernel(pos_ref, h_ref, gate_ref, fg_ref, y_ref, o_ref, ya_ref, yb_ref, sem):
    b = pl.program_id(0)
    nb = pl.num_programs(0)
    rows = h_ref.shape[0]

    def copies(blk, r):
        slot = blk % 2
        base = 2 * (blk * rows + r)
        return (pltpu.make_async_copy(y_ref.at[pl.ds(pos_ref[base], 1)], ya_ref.at[slot, pl.ds(r, 1)],
                                      sem.at[slot]),
                pltpu.make_async_copy(y_ref.at[pl.ds(pos_ref[base + 1], 1)], yb_ref.at[slot, pl.ds(r, 1)],
                                      sem.at[slot]))

    def issue(blk):
        def body(r, _):
            ca, cb = copies(blk, r)
            ca.start()
            cb.start()
            return 0
        lax.fori_loop(0, rows, body, 0, unroll=8)

    @pl.when(b == 0)
    def _():
        issue(0)

    @pl.when(b + 1 < nb)
    def _():
        issue(b + 1)

    def wait(r, _):
        ca, cb = copies(b, r)
        ca.wait()
        cb.wait()
        return 0

    lax.fori_loop(0, rows, wait, 0, unroll=8)
    slot = b % 2
    gate = gate_ref[...]
    moe = gate[:, 0:1] * ya_ref[slot] + gate[:, 1:2] * yb_ref[slot]
    o_ref[...] = _rms(h_ref[...] + moe, fg_ref[...])


def _combine(pos, h, gate, final_g, y, rows=512):
    T, D = h.shape
    row = pl.BlockSpec((rows, D), lambda b, pos: (b, 0))
    grid_spec = pltpu.PrefetchScalarGridSpec(
        num_scalar_prefetch=1,
        grid=(T // rows,),
        in_specs=[row, pl.BlockSpec((rows, 2), lambda b, pos: (b, 0)),
                  pl.BlockSpec((1, D), lambda b, pos: (0, 0)), pl.BlockSpec(memory_space=pl.ANY)],
        out_specs=row,
        scratch_shapes=[pltpu.VMEM((2, rows, D), F32), pltpu.VMEM((2, rows, D), F32),
                        pltpu.SemaphoreType.DMA((2,))],
    )
    return pl.pallas_call(
        _combine_kernel,
        grid_spec=grid_spec,
        out_shape=jax.ShapeDtypeStruct((T, D), F32),
        compiler_params=_cparams("arbitrary"),
        name="moe_combine",
    )(pos, h, gate, final_g, y)


def _rwkv_layer(h, ng, mu, w_rkv, w0, w1, w2, a0, a1, a2, g1, g2, k_k, k_a, r_k, gn_g, gn_b, w_o):
    D = h.shape[1]
    vec = lambda z: z.reshape(1, D)
    b16 = lambda z: z.astype(BF16)
    r, k, v, lw, a, g = _rwkv_pre(h, vec(ng), mu, b16(w_rkv), vec(w0), b16(w1), b16(w2), vec(a0),
                                  b16(a1), b16(a2), b16(g1), b16(g2))
    o = _rwkv_scan(r, k, v, lw, a, g, vec(k_k), vec(k_a), vec(r_k), vec(gn_g), vec(gn_b))
    return o, b16(w_o)


def _fox_layer(h, ng, w_in, b_f, q_gain, k_gain, w_o):
    T, D = h.shape
    H = D // HEAD_DIM
    order = jnp.argsort(b_f)
    w_main = w_in[:, :4 * D].reshape(D, 4, H, HEAD_DIM)[:, :, order, :].reshape(D, 4 * D)
    w_out = w_o.reshape(H, HEAD_DIM, D)[order].reshape(D, D)
    q, k, v, og, c2, augq, augk = _fox_pre(
        h, ng.reshape(1, D), w_main.astype(BF16), w_in[:, 4 * D:][:, order].astype(BF16),
        b_f[order].reshape(1, H), jnp.tile(q_gain, H).reshape(1, D), jnp.tile(k_gain, H).reshape(1, D))
    blk = min(ATTN_BLOCK, T)
    jlo = _first_live_block(c2, q_gain, k_gain, blk)
    o = _fox_attn(jlo, q, augq, k, augk, v.T, og, blk)
    return o, w_out.astype(BF16)


def _moe_layer(h, x, w_o, ng, final_g, w_router, w_gu, w_down):
    T, D = h.shape
    E, RB = N_EXPERTS, MOE_ROWS
    h, u, info, cnt, csr = _router(h, x, w_o, ng.reshape(1, D), w_router)
    e = info[:, 0:2].astype(jnp.int32)
    gate = info[:, 2:4]
    rank = info[:, 4:6].astype(jnp.int32)
    counts = cnt[0].astype(jnp.int32)
    padded = (counts + RB - 1) // RB * RB
    ends = jnp.cumsum(padded)
    starts = ends - padded
    n_blocks = (2 * T) // RB + E
    dest = starts[e] + rank
    blocks = jnp.arange(n_blocks, dtype=jnp.int32)
    block_e = jnp.minimum(jnp.sum(blocks[:, None] * RB >= ends[None, :], axis=1), E - 1).astype(jnp.int32)

    def first_rank(size):
        groups = jnp.arange(n_blocks * (RB // size), dtype=jnp.int32)
        group_e = jnp.repeat(block_e, RB // size)
        rank0 = groups * size - starts[group_e]
        return group_e, rank0, rank0 < counts[group_e]

    _, _, half_live = first_rank(MOE_HALF)
    group_e, rank0, group_live = first_rank(MOE_SUB)
    csr_e = csr[:, 0, :].astype(jnp.int32)[:, group_e]
    n_chunks = csr_e.shape[0]
    c_lo = jnp.clip(jnp.sum(csr_e <= rank0[None, :], axis=0) - 1, 0, n_chunks - 1)
    c_hi = jnp.clip(jnp.sum(csr_e < rank0[None, :] + MOE_SUB, axis=0) - 1, 0, n_chunks - 1)
    c_lo = jnp.where(group_live, c_lo, 1).astype(jnp.int32)
    c_hi = jnp.where(group_live, c_hi, 0).astype(jnp.int32)
    y = _moe_ffn(block_e, half_live.astype(jnp.int32), c_lo, c_hi, dest.T, u, w_gu, w_down, n_blocks)
    return _combine(dest.reshape(-1), h, gate, final_g.reshape(1, D), y)


def kernel(x, norm_g, final_g, rwkv_mu, rwkv_w_rkv, rwkv_w0, rwkv_w1, rwkv_w2, rwkv_a0, rwkv_a1,
           rwkv_a2, rwkv_g1, rwkv_g2, rwkv_k_k, rwkv_k_a, rwkv_r_k, rwkv_gn_g, rwkv_gn_b, rwkv_w_o,
           fox_w_in, fox_b_f, fox_q_gain, fox_k_gain, fox_w_o, ffn_w_gu, ffn_w_down,
           moe_w_router, moe_w_gu, moe_w_down):
    B, T, D = x.shape
    h = x.reshape(B * T, D)
    o, w_o = _rwkv_layer(h, norm_g[0, 0], rwkv_mu[0], rwkv_w_rkv[0], rwkv_w0[0], rwkv_w1[0], rwkv_w2[0],
                    rwkv_a0[0], rwkv_a1[0], rwkv_a2[0], rwkv_g1[0], rwkv_g2[0], rwkv_k_k[0],
                    rwkv_k_a[0], rwkv_r_k[0], rwkv_gn_g[0], rwkv_gn_b[0], rwkv_w_o[0])
    h = _ffn(h, o, w_o, norm_g[0, 1].reshape(1, D), ffn_w_gu[0].astype(BF16), ffn_w_down[0].astype(BF16))
    o, w_o = _fox_layer(h, norm_g[1, 0], fox_w_in[0], fox_b_f[0], fox_q_gain[0], fox_k_gain[0], fox_w_o[0])
    out = _moe_layer(h, o, w_o, norm_g[1, 1], final_g, moe_w_router[0], moe_w_gu[0], moe_w_down[0])
    return out.reshape(B, T, D)
```

```python
import functools

import jax
import jax.numpy as jnp
from jax import lax
from jax.experimental import pallas as pl
from jax.experimental.pallas import tpu as pltpu

F32 = jnp.float32
BF16 = jnp.bfloat16

HEAD_DIM = 64
PAIR = 2 * HEAD_DIM
GN_EPS = 64e-5
RMS_EPS = 1e-6
N_EXPERTS = 8
SCAN_CHUNK = 64
SCAN_CHUNKS_PER_STEP = 4
MOE_ROWS = 1024
MOE_HALF = 512
MOE_SUB = 128
MOE_CHUNK = 512
ROUTER_PARTS = 2
ATTN_BLOCK = 512
LOG2E = 1.4426950408889634
VMEM_LIMIT = 56 * 1024 * 1024
VMEM_LIMIT_MOE = 60 * 1024 * 1024


def _cparams(*sem, vmem_limit=VMEM_LIMIT):
    return pltpu.CompilerParams(dimension_semantics=sem, vmem_limit_bytes=vmem_limit)


def _dot(a, b):
    return jnp.dot(a.astype(BF16), b.astype(BF16), preferred_element_type=F32)


def _dot_nt(a, b):
    return lax.dot_general(a.astype(BF16), b.astype(BF16), (((1,), (1,)), ((), ())),
                           preferred_element_type=F32)


def _dot_tn(a, b):
    return lax.dot_general(a.astype(BF16), b.astype(BF16), (((0,), (0,)), ((), ())),
                           preferred_element_type=F32)


def _split_hi_lo(x):
    hi = x.astype(BF16)
    lo = (x - hi.astype(F32)).astype(BF16)
    return hi, lo


def _rms(x, g):
    ms = jnp.mean(x * x, axis=-1, keepdims=True)
    return x * lax.rsqrt(ms + RMS_EPS) * g


def _sigmoid(x):
    return 1.0 / (1.0 + jnp.exp(-x))


def _full(shape):
    return pl.BlockSpec(shape, lambda *_: (0,) * len(shape))


def _rwkv_pre_kernel(h_ref, hp_ref, ng_ref, mu_ref, wrkv_ref, w0_ref, w1_ref, w2_ref, a0_ref,
                     a1_ref, a2_ref, g1_ref, g2_ref, r_ref, k_ref, v_ref, lw_ref, a_ref, g_ref):
    i = pl.program_id(0)
    ng = ng_ref[...]
    u = _rms(h_ref[...], ng)
    up = _rms(hp_ref[7:8, :], ng)
    up = jnp.where(i == 0, 0.0, up)
    row = lax.broadcasted_iota(jnp.int32, u.shape, 0)
    u_prev = jnp.where(row == 0, up, pltpu.roll(u, 1, axis=0))
    dx = u_prev - u
    mix = lambda n: u + dx * mu_ref[n:n + 1, :]
    w_mid = jnp.tanh(_dot(mix(3), w1_ref[...]))
    a_mid = _dot(mix(4), a1_ref[...])
    g_mid = _sigmoid(_dot(mix(5), g1_ref[...]))
    r_ref[...] = _dot(mix(0), wrkv_ref[0])
    k_ref[...] = _dot(mix(1), wrkv_ref[1])
    v_ref[...] = _dot(mix(2), wrkv_ref[2])
    z = w0_ref[...] + _dot(w_mid, w2_ref[...])
    softplus_neg = jnp.maximum(-z, 0.0) + jnp.log(1.0 + jnp.exp(-jnp.abs(z)))
    lw_ref[...] = -jnp.exp(-softplus_neg - 0.5)
    a_ref[...] = _sigmoid(a0_ref[...] + _dot(a_mid, a2_ref[...]))
    g_ref[...] = _dot(g_mid, g2_ref[...])


def _rwkv_pre(h, ng, mu, wrkv, w0, w1, w2, a0, a1, a2, g1, g2, tm=512):
    T, D = h.shape
    row = pl.BlockSpec((tm, D), lambda i: (i, 0))
    prev = pl.BlockSpec((8, D), lambda i: (jnp.maximum(i * (tm // 8) - 1, 0), 0))
    ins = [ng, mu, wrkv, w0, w1, w2, a0, a1, a2, g1, g2]
    return pl.pallas_call(
        _rwkv_pre_kernel,
        grid=(T // tm,),
        in_specs=[row, prev] + [_full(a.shape) for a in ins],
        out_specs=[row] * 6,
        out_shape=[jax.ShapeDtypeStruct((T, D), F32)] * 6,
        compiler_params=_cparams("parallel"),
        name="rwkv_pre",
    )(h, h, *ins)


def _scan_kernel(r_ref, k_ref, v_ref, lw_ref, a_ref, g_ref, kk_ref, ka_ref, rk_ref, gng_ref,
                 gnb_ref, o_ref, s_ref):
    L = SCAN_CHUNK
    n_pairs = r_ref.shape[1] // PAIR

    @pl.when(pl.program_id(0) == 0)
    def _():
        s_ref[...] = jnp.zeros_like(s_ref)

    row = lax.broadcasted_iota(jnp.int32, (L, L), 0)
    col = lax.broadcasted_iota(jnp.int32, (L, L), 1)
    tri_incl = jnp.where(col <= row, 1.0, 0.0).astype(BF16)
    eye = jnp.where(col == row, 1.0, 0.0).astype(F32)
    lane = lax.broadcasted_iota(jnp.int32, (1, PAIR), 1)
    head0 = lane < HEAD_DIM
    arow = lax.broadcasted_iota(jnp.int32, (4 * L, PAIR), 0)
    acol = lax.broadcasted_iota(jnp.int32, (4 * L, PAIR), 1)
    t_idx = arow % L + jnp.where(arow < 2 * L, 0, 1)
    j_idx = acol % L
    a_mask = j_idx < t_idx
    brow = lax.broadcasted_iota(jnp.int32, (PAIR, PAIR), 0)
    bcol = lax.broadcasted_iota(jnp.int32, (PAIR, PAIR), 1)
    blockdiag = (brow < HEAD_DIM) == (bcol < HEAD_DIM)

    def head_sum(x):
        s0 = jnp.sum(jnp.where(head0, x, 0.0), axis=-1, keepdims=True)
        s1 = jnp.sum(jnp.where(head0, 0.0, x), axis=-1, keepdims=True)
        return jnp.where(head0, s0, s1)

    pairs = range(n_pairs)
    heads = [(p, h) for p in pairs for h in range(2)]
    sls = [slice(p * PAIR, (p + 1) * PAIR) for p in pairs]
    n_chunks = r_ref.shape[0] // L
    chunks = range(n_chunks)
    rows = [slice(c * L, (c + 1) * L) for c in chunks]
    cp = [(c, p) for c in chunks for p in pairs]
    cph = [(c, p, h) for c in chunks for p, h in heads]

    lws = {(c, p): lw_ref[rows[c], sls[p]] for c, p in cp}
    splits = {k: _split_hi_lo(lw) for k, lw in lws.items()}
    css = {k: jnp.dot(tri_incl, hi, preferred_element_type=F32) + jnp.dot(tri_incl, lo, preferred_element_type=F32)
           for k, (hi, lo) in splits.items()}
    prep = {}
    for c, p in cp:
        rw, sl, cs, lw = rows[c], sls[p], css[c, p], lws[c, p]
        w_inv = jnp.exp(-cs)
        w_last = jnp.exp(cs[L - 1:L, :])
        r = r_ref[rw, sl]
        k = k_ref[rw, sl]
        a = a_ref[rw, sl]
        kk = k * kk_ref[:, sl]
        kk = kk / jnp.maximum(jnp.sqrt(head_sum(kk * kk)), 1e-12)
        kmod = k * (1.0 + (a - 1.0) * ka_ref[:, sl])
        at = -kk * jnp.exp(cs - lw)
        bt = kk * a * w_inv
        kt = kmod * w_inv
        rt = r * jnp.exp(cs)
        v = v_ref[rw, sl]
        prep[c, p] = dict(atrt=jnp.concatenate([at, rt], axis=0).astype(BF16),
                          x_stack=jnp.concatenate([jnp.where(head0, at, 0.0), jnp.where(head0, 0.0, at),
                                                   jnp.where(head0, rt, 0.0), jnp.where(head0, 0.0, rt)],
                                                  axis=0).astype(BF16),
                          bk=jnp.concatenate([bt, kt], axis=0).astype(BF16),
                          bkhat=jnp.concatenate([bt * w_last, kt * w_last], axis=0).astype(BF16),
                          w_last=w_last, v=v, v_bf=v.astype(BF16), rk=r * kmod)
    aas = {k: jnp.where(a_mask, _dot_nt(prep[k]["x_stack"], prep[k]["bk"]), 0.0) for k in cp}
    akv = {}
    pw = {}
    t_inv = {}
    for c, p, h in cph:
        a_rows = aas[c, p][h * L:(h + 1) * L, :]
        vv = jnp.concatenate([prep[c, p]["v_bf"], prep[c, p]["v_bf"]], axis=0)
        akv[c, p, h] = _dot(jnp.where(head0, 0.0, a_rows), vv)
        pw[c, p, h] = a_rows[:, :L]
        t_inv[c, p, h] = eye + pw[c, p, h]
    for _ in range(5):
        for k in cph:
            pw[k] = _dot(pw[k], pw[k])
        for k in cph:
            t_inv[k] = t_inv[k] + _dot(t_inv[k], pw[k])

    states = [s_ref[p] for p in pairs]
    for c in chunks:
        inter = [_dot_nt(prep[c, p]["atrt"], states[p]) for p in pairs]
        u_h = {(p, h): _dot(t_inv[c, p, h], inter[p][:L, :] + akv[c, p, h]) for p, h in heads}
        y_h = {}
        for p, h in heads:
            ar = aas[c, p][(2 + h) * L:(3 + h) * L, :]
            y_h[p, h] = _dot(ar, jnp.concatenate([u_h[p, h].astype(BF16), prep[c, p]["v_bf"]], axis=0))
        new_states = []
        for p in pairs:
            pr = prep[c, p]
            u = jnp.where(head0, u_h[p, 0], u_h[p, 1])
            uv = jnp.concatenate([u.astype(BF16), pr["v_bf"]], axis=0)
            new_states.append(states[p] * pr["w_last"] + jnp.where(blockdiag, _dot_tn(uv, pr["bkhat"]), 0.0))
        states = new_states
        for p in pairs:
            sl = sls[p]
            pr = prep[c, p]
            y = inter[p][L:, :] + jnp.where(head0, y_h[p, 0], y_h[p, 1])
            mean = head_sum(y) * (1.0 / HEAD_DIM)
            d = y - mean
            var = head_sum(d * d) * (1.0 / HEAD_DIM)
            yn = d * lax.rsqrt(var + GN_EPS) * gng_ref[:, sl] + gnb_ref[:, sl]
            bonus = head_sum(pr["rk"] * rk_ref[:, sl]) * pr["v"]
            o_ref[rows[c], sl] = ((yn + bonus) * g_ref[rows[c], sl]).astype(o_ref.dtype)
    for p in pairs:
        s_ref[p] = states[p]


def _rwkv_scan(r, k, v, lw, a, g, k_k, k_a, r_k, gn_g, gn_b):
    T, D = r.shape
    rows = SCAN_CHUNK * SCAN_CHUNKS_PER_STEP
    row = pl.BlockSpec((rows, D), lambda c: (c, 0))
    par = _full((1, D))
    return pl.pallas_call(
        _scan_kernel,
        grid=(T // rows,),
        in_specs=[row] * 6 + [par] * 5,
        out_specs=row,
        out_shape=jax.ShapeDtypeStruct((T, D), BF16),
        scratch_shapes=[pltpu.VMEM((D // PAIR, PAIR, PAIR), F32)],
        compiler_params=_cparams("arbitrary"),
        name="rwkv_scan",
    )(r, k, v, lw, a, g, k_k, k_a, r_k, gn_g, gn_b)


def _swiglu_step(u_ref, wg_ref, wu_ref, wd_ref, acc_ref):
    u = u_ref[...]
    gate = jnp.dot(u, wg_ref[...].astype(BF16), preferred_element_type=F32)
    up = jnp.dot(u, wu_ref[...].astype(BF16), preferred_element_type=F32)
    hid = (gate * _sigmoid(gate) * up).astype(BF16)
    acc_ref[...] += jnp.dot(hid, wd_ref[...].astype(BF16), preferred_element_type=F32)


def _ffn_kernel(h_ref, x_ref, wo_ref, ng_ref, wg_ref, wu_ref, wd_ref, o_ref, u_ref):
    @pl.when(pl.program_id(1) == 0)
    def _():
        h1 = h_ref[...] + jnp.dot(x_ref[...], wo_ref[...], preferred_element_type=F32)
        o_ref[...] = h1
        u_ref[...] = _rms(h1, ng_ref[...]).astype(BF16)

    _swiglu_step(u_ref, wg_ref, wu_ref, wd_ref, o_ref)


def _ffn(h, x, w_o, ng, w_gu, w_down, tm=512, tf=1792):
    T, D = h.shape
    F = w_down.shape[0]
    nf = F // tf
    row = pl.BlockSpec((tm, D), lambda i, f: (i, 0))
    return pl.pallas_call(
        _ffn_kernel,
        grid=(T // tm, nf),
        in_specs=[row, row, _full((D, D)), _full((1, D)),
                  pl.BlockSpec((D, tf), lambda i, f: (0, f)),
                  pl.BlockSpec((D, tf), lambda i, f: (0, f + nf)),
                  pl.BlockSpec((tf, D), lambda i, f: (f, 0))],
        out_specs=row,
        out_shape=jax.ShapeDtypeStruct((T, D), F32),
        scratch_shapes=[pltpu.VMEM((tm, D), BF16)],
        compiler_params=_cparams("parallel", "arbitrary"),
        name="ffn",
    )(h, x, w_o, ng, w_gu, w_gu, w_down)


def _moe_ffn_kernel(be_ref, live_ref, clo_ref, chi_ref, dest_ref, tok_hbm, wg_ref, wu_ref, wd_ref, o_ref,
                    tok_ref, x_ref, sem):
    b = pl.program_id(0)
    f = pl.program_id(1)
    rows = o_ref.shape[0]
    ch = MOE_CHUNK
    halves = rows // MOE_HALF

    @pl.when((b == 0) & (f == 0))
    def _():
        cp = pltpu.make_async_copy(tok_hbm, tok_ref, sem)
        cp.start()
        cp.wait()

    @pl.when(f == 0)
    def _():
        o_ref[...] = jnp.zeros_like(o_ref)
        for s in range(rows // MOE_SUB):
            sub = slice(s * MOE_SUB, (s + 1) * MOE_SUB)
            row_id = b * rows + s * MOE_SUB + lax.broadcasted_iota(jnp.int32, (MOE_SUB, 1), 0)
            w = b * (rows // MOE_SUB) + s

            def chunk(c, _, sub=sub, row_id=row_id):
                start = pl.multiple_of(c * ch, ch)
                d = dest_ref[:, pl.ds(start, ch)]
                onehot = (jnp.where(row_id == d[0:1, :], 1.0, 0.0)
                          + jnp.where(row_id == d[1:2, :], 1.0, 0.0)).astype(BF16)
                o_ref[sub, :] += jnp.dot(onehot, tok_ref[pl.ds(start, ch), :], preferred_element_type=F32)
                return 0

            lax.fori_loop(clo_ref[w], chi_ref[w] + 1, chunk, 0)
        x_ref[...] = o_ref[...].astype(BF16)
        o_ref[...] = jnp.zeros_like(o_ref)

    for half in range(halves):
        @pl.when(live_ref[b * halves + half] > 0)
        def _(half=half):
            part = pl.ds(half * MOE_HALF, MOE_HALF)
            _swiglu_step(x_ref.at[part], wg_ref.at[0], wu_ref.at[0], wd_ref.at[0], o_ref.at[part])


def _moe_ffn(block_e, half_live, c_lo, c_hi, dest_t, tok, w_gu, w_down, n_blocks, tf=512):
    T, D = tok.shape
    F = w_down.shape[1]
    nf = F // tf
    tm = MOE_ROWS

    def fstep(b, f, live):
        return jnp.where(live[b * (tm // MOE_HALF)] > 0, f, nf - 1)

    grid_spec = pltpu.PrefetchScalarGridSpec(
        num_scalar_prefetch=4,
        grid=(n_blocks, nf),
        in_specs=[pl.BlockSpec((2, T), lambda b, f, *_: (0, 0)),
                  pl.BlockSpec(memory_space=pl.ANY),
                  pl.BlockSpec((1, D, tf), lambda b, f, be, live, *_: (be[b], 0, fstep(b, f, live))),
                  pl.BlockSpec((1, D, tf), lambda b, f, be, live, *_: (be[b], 0, fstep(b, f, live) + nf)),
                  pl.BlockSpec((1, tf, D), lambda b, f, be, live, *_: (be[b], fstep(b, f, live), 0))],
        out_specs=pl.BlockSpec((tm, D), lambda b, f, *_: (b, 0)),
        scratch_shapes=[pltpu.VMEM((T, D), BF16), pltpu.VMEM((tm, D), BF16), pltpu.SemaphoreType.DMA(())],
    )
    return pl.pallas_call(
        _moe_ffn_kernel,
        grid_spec=grid_spec,
        out_shape=jax.ShapeDtypeStruct((n_blocks * tm, D), F32),
        compiler_params=_cparams("arbitrary", "arbitrary", vmem_limit=VMEM_LIMIT_MOE),
        name="moe_ffn",
    )(block_e, half_live, c_lo, c_hi, dest_t, tok, w_gu, w_gu, w_down)


def _fox_pre_kernel(h_ref, ng_ref, w_ref, wf_ref, bf_ref, qg_ref, kg_ref, sel_ref, selt_ref, pq_ref, pk_ref,
                    oneq_ref, onek_ref, q_ref, k_ref, v_ref, og_ref, c2_ref, augq_ref, augk_ref, carry_ref):
    i = pl.program_id(0)
    tm, D = h_ref.shape
    H = c2_ref.shape[1]

    @pl.when(i == 0)
    def _():
        carry_ref[...] = jnp.zeros_like(carry_ref)

    u = _rms(h_ref[...], ng_ref[...]).astype(BF16)

    q = jnp.dot(u, w_ref[:, 0:D], preferred_element_type=F32)
    k = jnp.dot(u, w_ref[:, D:2 * D], preferred_element_type=F32)
    fl = jnp.dot(u, wf_ref[...], preferred_element_type=F32) + bf_ref[...]
    v_ref[...] = jnp.dot(u, w_ref[:, 2 * D:3 * D], preferred_element_type=F32).astype(BF16)
    og_ref[...] = _sigmoid(jnp.dot(u, w_ref[:, 3 * D:4 * D], preferred_element_type=F32)).astype(BF16)

    mean_sq = [_dot(x * x, sel_ref[...]) * (1.0 / HEAD_DIM) for x in (q, k)]

    log_f = jnp.minimum(fl, 0.0) - jnp.log(1.0 + jnp.exp(-jnp.abs(fl)))
    row = lax.broadcasted_iota(jnp.int32, (tm, tm), 0)
    col = lax.broadcasted_iota(jnp.int32, (tm, tm), 1)
    tri = jnp.where(col <= row, 1.0, 0.0).astype(BF16)
    hi, lo = _split_hi_lo(log_f)
    lo2 = (log_f - hi.astype(F32) - lo.astype(F32)).astype(BF16)
    c = (jnp.dot(tri, hi, preferred_element_type=F32) + jnp.dot(tri, lo, preferred_element_type=F32)
         + jnp.dot(tri, lo2, preferred_element_type=F32)) + carry_ref[...]
    carry_ref[...] = c[tm - 1:tm, :]

    def spread(ms):
        inv = lax.rsqrt(ms + RMS_EPS)
        inv_hi, inv_lo = _split_hi_lo(inv)
        first = lax.broadcasted_iota(jnp.int32, inv.shape, 1) < H
        return jnp.dot(jnp.where(first, inv_hi, inv_lo), selt_ref[...], preferred_element_type=F32)

    q_inv, k_inv = [spread(ms) for ms in mean_sq]
    q_ref[...] = (q * q_inv * qg_ref[...] * (HEAD_DIM ** -0.5 * LOG2E)).astype(BF16)
    k_ref[...] = (k * k_inv * kg_ref[...]).astype(BF16)

    c2 = c * LOG2E
    c2_ref[...] = c2[:, 0:H]
    c_hi, c_lo = _split_hi_lo(c2)
    c_lo2 = (c2 - c_hi.astype(F32) - c_lo.astype(F32)).astype(BF16)
    group = lax.broadcasted_iota(jnp.int32, c2.shape, 1) // H
    pieces = jnp.where(group == 0, c_hi, jnp.where(group == 1, c_lo, c_lo2))
    augq_ref[...] = (jnp.dot(pieces, pq_ref[...], preferred_element_type=F32) + oneq_ref[...]).astype(BF16)
    augk_ref[...] = (jnp.dot(pieces, pk_ref[...], preferred_element_type=F32) + onek_ref[...]).astype(BF16)


def _fox_pre(h, ng, w_main, w_f, b_f, q_gain, k_gain, tm=512):
    T, D = h.shape
    H = D // HEAD_DIM
    sel = (jnp.arange(D)[:, None] // HEAD_DIM == jnp.arange(2 * H)[None, :] % H).astype(BF16)
    piece = jnp.arange(3 * H)[:, None] // H
    head = jnp.arange(3 * H)[:, None] % H
    lane = jnp.arange(D)[None, :]
    at = lambda j: lane == head * HEAD_DIM + j
    p_q = at(piece).astype(BF16)
    p_k = -at(piece + 3).astype(BF16)
    off = jnp.arange(D) % HEAD_DIM
    one_q = ((off >= 3) & (off < 6)).astype(F32).reshape(1, D)
    one_k = (off < 3).astype(F32).reshape(1, D)
    row = pl.BlockSpec((tm, D), lambda i: (i, 0))
    ins = [ng, w_main, jnp.tile(w_f, (1, 3)), jnp.tile(b_f, (1, 3)), q_gain, k_gain, sel, sel.T, p_q, p_k,
           one_q, one_k]
    return pl.pallas_call(
        _fox_pre_kernel,
        grid=(T // tm,),
        in_specs=[row] + [_full(a.shape) for a in ins],
        out_specs=[row] * 4 + [pl.BlockSpec((tm, H), lambda i: (i, 0))] + [row] * 2,
        out_shape=([jax.ShapeDtypeStruct((T, D), BF16)] * 4 + [jax.ShapeDtypeStruct((T, H), F32)]
                   + [jax.ShapeDtypeStruct((T, D), BF16)] * 2),
        scratch_shapes=[pltpu.VMEM((1, 3 * H), F32)],
        compiler_params=_cparams("arbitrary"),
        name="fox_pre",
    )(h, *ins)


def _fox_attn_kernel(jlo_ref, q_ref, augq_ref, k_ref, augk_ref, vt_ref, og_ref, o_ref, s_ref, *, blk):
    p_idx = pl.program_id(0)
    i = pl.program_id(1)
    head0 = lax.broadcasted_iota(jnp.int32, (1, PAIR), 1) < HEAD_DIM
    q, augq = q_ref[...], augq_ref[...]
    zero = jnp.zeros_like(q)
    q_heads = [jnp.concatenate([jnp.where(m, q, zero), jnp.where(m, augq, zero)], axis=1)
               for m in (head0, jnp.logical_not(head0))]
    causal = (lax.broadcasted_iota(jnp.int32, (blk, blk), 0)
              <= lax.broadcasted_iota(jnp.int32, (blk, blk), 1))

    def scores_into(j, slot):
        start = pl.multiple_of(j * blk, blk)
        kb = jnp.concatenate([k_ref[pl.ds(start, blk), :], augk_ref[pl.ds(start, blk), :]], axis=1)
        for h in range(2):
            s_ref[slot, h] = lax.dot_general(kb, q_heads[h], (((1,), (1,)), ((), ())),
                                             preferred_element_type=F32)

    def softmax_pv(j, slot, carry, masked):
        start = pl.multiple_of(j * blk, blk)
        out = []
        for h in range(2):
            m_prev, l_prev, acc_prev = carry[h]
            vtb = vt_ref[h * HEAD_DIM:(h + 1) * HEAD_DIM, pl.ds(start, blk)]
            s = s_ref[slot, h]
            if masked:
                s = jnp.where(causal, s, -jnp.inf)
            m_new = jnp.maximum(m_prev, jnp.max(s, axis=0, keepdims=True))
            alpha = jnp.exp2(m_prev - m_new)
            p = jnp.exp2(s - m_new)
            l_new = alpha * l_prev + jnp.sum(p, axis=0, keepdims=True)
            acc_new = alpha * acc_prev + jnp.dot(vtb, p.astype(BF16), preferred_element_type=F32)
            out.append((m_new, l_new, acc_new))
        return tuple(out)

    init = tuple((jnp.full((1, blk), -jnp.inf, F32), jnp.zeros((1, blk), F32), jnp.zeros((HEAD_DIM, blk), F32))
                 for _ in range(2))
    jlo = jlo_ref[p_idx * pl.num_programs(1) + i]
    total = i - jlo + 1
    scores_into(jlo, 0)

    def two_blocks(t, carry):
        j = jlo + 2 * t
        scores_into(j + 1, 1)
        carry = softmax_pv(j, 0, carry, False)
        scores_into(j + 2, 0)
        return softmax_pv(j + 1, 1, carry, False)

    res = lax.fori_loop(0, (total - 1) // 2, two_blocks, init)

    def last_one(carry):
        return softmax_pv(i, 0, carry, True)

    def last_two(carry):
        scores_into(i, 1)
        carry = softmax_pv(i - 1, 0, carry, False)
        return softmax_pv(i, 1, carry, True)

    res = lax.cond(total % 2 == 1, last_one, last_two, res)
    o_t = jnp.concatenate([res[0][2] / res[0][1], res[1][2] / res[1][1]], axis=0)
    o_ref[...] = (o_t.T * og_ref[...].astype(F32)).astype(o_ref.dtype)


def _fox_attn(jlo, q, augq, k, augk, v_t, og, blk):
    T, D = q.shape
    n_pairs = D // PAIR
    qspec = pl.BlockSpec((blk, PAIR), lambda p, i, jlo: (i, p))
    kspec = pl.BlockSpec((T, PAIR), lambda p, i, jlo: (0, p))
    grid_spec = pltpu.PrefetchScalarGridSpec(
        num_scalar_prefetch=1,
        grid=(n_pairs, T // blk),
        in_specs=[qspec, qspec, kspec, kspec,
                  pl.BlockSpec((PAIR, T), lambda p, i, jlo: (p, 0)),
                  qspec],
        out_specs=qspec,
        scratch_shapes=[pltpu.VMEM((2, 2, blk, blk), F32)],
    )
    return pl.pallas_call(
        functools.partial(_fox_attn_kernel, blk=blk),
        grid_spec=grid_spec,
        out_shape=jax.ShapeDtypeStruct((T, D), BF16),
        compiler_params=_cparams("parallel", "arbitrary"),
        name="fox_attn",
    )(jlo, q, augq, k, augk, v_t, og)


def _first_live_block(c2, q_gain, k_gain, blk):
    T, H = c2.shape
    bound = 8.0 * LOG2E * 1.02 * jnp.max(jnp.abs(q_gain)) * jnp.max(jnp.abs(k_gain))
    c_first = c2[0::blk]
    c_last = c2[blk - 1::blk]
    dead = c_last[None, :, :] > c_first[:, None, :] + (2.0 * bound + 104.0 * LOG2E + 1.0)
    jlo = jnp.sum(dead, axis=1).astype(jnp.int32)
    jlo = jnp.min(jlo.reshape(-1, H // 2, 2), axis=-1)
    return jlo.T.reshape(-1)


def _router_kernel(h_ref, x_ref, wo_ref, ng_ref, wr_ref, h1_ref, u_ref, info_ref, cnt_ref, csr_ref, carry_ref):
    i = pl.program_id(0)
    tm = h_ref.shape[0]
    E = N_EXPERTS
    parts = ROUTER_PARTS if tm % (8 * ROUTER_PARTS) == 0 else 1
    pm = tm // parts
    rows = [slice(j * pm, (j + 1) * pm) for j in range(parts)]

    @pl.when(i == 0)
    def _():
        carry_ref[...] = jnp.zeros_like(carry_ref)

    csr_ref[0] = carry_ref[...]
    h1 = [h_ref[r, :] + jnp.dot(x_ref[r, :], wo_ref[...], preferred_element_type=F32) for r in rows]
    for r, z in zip(rows, h1):
        h1_ref[r, :] = z
    u = [_rms(z, ng_ref[...]) for z in h1]
    for r, z in zip(rows, u):
        u_ref[r, :] = z.astype(BF16)
    w_hi, w_lo = _split_hi_lo(wr_ref[...])
    logits = []
    for z in u:
        u_hi, u_lo = _split_hi_lo(z)
        logits.append(jnp.dot(u_hi, w_hi, preferred_element_type=F32)
                      + jnp.dot(u_hi, w_lo, preferred_element_type=F32)
                      + jnp.dot(u_lo, w_hi, preferred_element_type=F32))
    lane = lax.broadcasted_iota(jnp.int32, (pm, E), 1)
    row = lax.broadcasted_iota(jnp.int32, (pm, pm), 0)
    col = lax.broadcasted_iota(jnp.int32, (pm, pm), 1)
    tri_strict = jnp.where(col < row, 1.0, 0.0).astype(BF16)
    picks = []
    for lg in logits:
        v1 = jnp.max(lg, axis=-1, keepdims=True)
        e1 = jnp.min(jnp.where(lg == v1, lane, E), axis=-1, keepdims=True)
        rest = jnp.where(lane == e1, -jnp.inf, lg)
        v2 = jnp.max(rest, axis=-1, keepdims=True)
        e2 = jnp.min(jnp.where(rest == v2, lane, E), axis=-1, keepdims=True)
        g1 = 1.0 / (1.0 + jnp.exp(v2 - v1))
        oh1 = lane == e1
        oh2 = lane == e2
        onehot = jnp.where(oh1 | oh2, 1.0, 0.0).astype(BF16)
        picks.append((e1, e2, g1, oh1, oh2, onehot))
    within = [jnp.dot(tri_strict, p[5], preferred_element_type=F32) for p in picks]
    counted = carry_ref[...]
    for r, (e1, e2, g1, oh1, oh2, onehot), inside in zip(rows, picks, within):
        before = inside + counted
        rank1 = jnp.sum(jnp.where(oh1, before, 0.0), axis=-1, keepdims=True)
        rank2 = jnp.sum(jnp.where(oh2, before, 0.0), axis=-1, keepdims=True)
        info = jnp.where(lane == 0, e1.astype(F32), 0.0)
        info = jnp.where(lane == 1, e2.astype(F32), info)
        info = jnp.where(lane == 2, g1, info)
        info = jnp.where(lane == 3, 1.0 - g1, info)
        info = jnp.where(lane == 4, rank1, info)
        info = jnp.where(lane == 5, rank2, info)
        info_ref[r, :] = info
        counted = counted + jnp.sum(onehot.astype(F32), axis=0, keepdims=True)
    carry_ref[...] = counted
    cnt_ref[...] = counted


def _router(h, x, w_o, ng, w_router):
    T, D = h.shape
    E = N_EXPERTS
    tm = min(MOE_CHUNK, T)
    row = pl.BlockSpec((tm, D), lambda i: (i, 0))
    return pl.pallas_call(
        _router_kernel,
        grid=(T // tm,),
        in_specs=[row, row, _full((D, D)), _full((1, D)), _full((D, E))],
        out_specs=[row, row, pl.BlockSpec((tm, E), lambda i: (i, 0)), _full((1, E)),
                   pl.BlockSpec((1, 1, E), lambda i: (i, 0, 0))],
        out_shape=[jax.ShapeDtypeStruct((T, D), F32), jax.ShapeDtypeStruct((T, D), BF16),
                   jax.ShapeDtypeStruct((T, E), F32), jax.ShapeDtypeStruct((1, E), F32),
                   jax.ShapeDtypeStruct((T // tm, 1, E), F32)],
        scratch_shapes=[pltpu.VMEM((1, E), F32)],
        compiler_params=_cparams("arbitrary"),
        name="router",
    )(h, x, w_o, ng, w_router)


def _combine_kernel(pos_ref, h_ref, gate_ref, fg_ref, y_ref, o_ref, ya_ref, yb_ref, sem):
    b = pl.program_id(0)
    nb = pl.num_programs(0)
    rows = h_ref.shape[0]

    def copies(blk, r):
        slot = blk % 2
        base = 2 * (blk * rows + r)
        return (pltpu.make_async_copy(y_ref.at[pl.ds(pos_ref[base], 1)], ya_ref.at[slot, pl.ds(r, 1)],
                                      sem.at[slot]),
                pltpu.make_async_copy(y_ref.at[pl.ds(pos_ref[base + 1], 1)], yb_ref.at[slot, pl.ds(r, 1)],
                                      sem.at[slot]))

    def issue(blk):
        def body(r, _):
            ca, cb = copies(blk, r)
            ca.start()
            cb.start()
            return 0
        lax.fori_loop(0, rows, body, 0, unroll=8)

    @pl.when(b == 0)
    def _():
        issue(0)

    @pl.when(b + 1 < nb)
    def _():
        issue(b + 1)

    def wait(r, _):
        ca, cb = copies(b, r)
        ca.wait()
        cb.wait()
        return 0

    lax.fori_loop(0, rows, wait, 0, unroll=8)
    slot = b % 2
    gate = gate_ref[...]
    moe = gate[:, 0:1] * ya_ref[slot] + gate[:, 1:2] * yb_ref[slot]
    o_ref[...] = _rms(h_ref[...] + moe, fg_ref[...])


def _combine(pos, h, gate, final_g, y, rows=512):
    T, D = h.shape
    row = pl.BlockSpec((rows, D), lambda b, pos: (b, 0))
    grid_spec = pltpu.PrefetchScalarGridSpec(
        num_scalar_prefetch=1,
        grid=(T // rows,),
        in_specs=[row, pl.BlockSpec((rows, 2), lambda b, pos: (b, 0)),
                  pl.BlockSpec((1, D), lambda b, pos: (0, 0)), pl.BlockSpec(memory_space=pl.ANY)],
        out_specs=row,
        scratch_shapes=[pltpu.VMEM((2, rows, D), F32), pltpu.VMEM((2, rows, D), F32),
                        pltpu.SemaphoreType.DMA((2,))],
    )
    return pl.pallas_call(
        _combine_kernel,
        grid_spec=grid_spec,
        out_shape=jax.ShapeDtypeStruct((T, D), F32),
        compiler_params=_cparams("arbitrary"),
        name="moe_combine",
    )(pos, h, gate, final_g, y)


def _rwkv_layer(h, ng, mu, w_rkv, w0, w1, w2, a0, a1, a2, g1, g2, k_k, k_a, r_k, gn_g, gn_b, w_o):
    D = h.shape[1]
    vec = lambda z: z.reshape(1, D)
    b16 = lambda z: z.astype(BF16)
    r, k, v, lw, a, g = _rwkv_pre(h, vec(ng), mu, b16(w_rkv), vec(w0), b16(w1), b16(w2), vec(a0),
                                  b16(a1), b16(a2), b16(g1), b16(g2))
    o = _rwkv_scan(r, k, v, lw, a, g, vec(k_k), vec(k_a), vec(r_k), vec(gn_g), vec(gn_b))
    return o, b16(w_o)


def _fox_layer(h, ng, w_in, b_f, q_gain, k_gain, w_o):
    T, D = h.shape
    H = D // HEAD_DIM
    order = jnp.argsort(b_f)
    w_main = w_in[:, :4 * D].reshape(D, 4, H, HEAD_DIM)[:, :, order, :].reshape(D, 4 * D)
    w_out = w_o.reshape(H, HEAD_DIM, D)[order].reshape(D, D)
    q, k, v, og, c2, augq, augk = _fox_pre(
        h, ng.reshape(1, D), w_main.astype(BF16), w_in[:, 4 * D:][:, order].astype(BF16),
        b_f[order].reshape(1, H), jnp.tile(q_gain, H).reshape(1, D), jnp.tile(k_gain, H).reshape(1, D))
    blk = min(ATTN_BLOCK, T)
    jlo = _first_live_block(c2, q_gain, k_gain, blk)
    o = _fox_attn(jlo, q, augq, k, augk, v.T, og, blk)
    return o, w_out.astype(BF16)


def _moe_layer(h, x, w_o, ng, final_g, w_router, w_gu, w_down):
    T, D = h.shape
    E, RB = N_EXPERTS, MOE_ROWS
    h, u, info, cnt, csr = _router(h, x, w_o, ng.reshape(1, D), w_router)
    e = info[:, 0:2].astype(jnp.int32)
    gate = info[:, 2:4]
    rank = info[:, 4:6].astype(jnp.int32)
    counts = cnt[0].astype(jnp.int32)
    padded = (counts + RB - 1) // RB * RB
    ends = jnp.cumsum(padded)
    starts = ends - padded
    n_blocks = (2 * T) // RB + E
    dest = starts[e] + rank
    blocks = jnp.arange(n_blocks, dtype=jnp.int32)
    block_e = jnp.minimum(jnp.sum(blocks[:, None] * RB >= ends[None, :], axis=1), E - 1).astype(jnp.int32)

    def first_rank(size):
        groups = jnp.arange(n_blocks * (RB // size), dtype=jnp.int32)
        group_e = jnp.repeat(block_e, RB // size)
        rank0 = groups * size - starts[group_e]
        return group_e, rank0, rank0 < counts[group_e]

    _, _, half_live = first_rank(MOE_HALF)
    group_e, rank0, group_live = first_rank(MOE_SUB)
    csr_e = csr[:, 0, :].astype(jnp.int32)[:, group_e]
    n_chunks = csr_e.shape[0]
    c_lo = jnp.clip(jnp.sum(csr_e <= rank0[None, :], axis=0) - 1, 0, n_chunks - 1)
    c_hi = jnp.clip(jnp.sum(csr_e < rank0[None, :] + MOE_SUB, axis=0) - 1, 0, n_chunks - 1)
    c_lo = jnp.where(group_live, c_lo, 1).astype(jnp.int32)
    c_hi = jnp.where(group_live, c_hi, 0).astype(jnp.int32)
    y = _moe_ffn(block_e, half_live.astype(jnp.int32), c_lo, c_hi, dest.T, u, w_gu, w_down, n_blocks)
    return _combine(dest.reshape(-1), h, gate, final_g.reshape(1, D), y)


def kernel(x, norm_g, final_g, rwkv_mu, rwkv_w_rkv, rwkv_w0, rwkv_w1, rwkv_w2, rwkv_a0, rwkv_a1,
           rwkv_a2, rwkv_g1, rwkv_g2, rwkv_k_k, rwkv_k_a, rwkv_r_k, rwkv_gn_g, rwkv_gn_b, rwkv_w_o,
           fox_w_in, fox_b_f, fox_q_gain, fox_k_gain, fox_w_o, ffn_w_gu, ffn_w_down,
           moe_w_router, moe_w_gu, moe_w_down):
    B, T, D = x.shape
    h = x.reshape(B * T, D)
    o, w_o = _rwkv_layer(h, norm_g[0, 0], rwkv_mu[0], rwkv_w_rkv[0], rwkv_w0[0], rwkv_w1[0], rwkv_w2[0],
                    rwkv_a0[0], rwkv_a1[0], rwkv_a2[0], rwkv_g1[0], rwkv_g2[0], rwkv_k_k[0],
                    rwkv_k_a[0], rwkv_r_k[0], rwkv_gn_g[0], rwkv_gn_b[0], rwkv_w_o[0])
    h = _ffn(h, o, w_o, norm_g[0, 1].reshape(1, D), ffn_w_gu[0].astype(BF16), ffn_w_down[0].astype(BF16))
    o, w_o = _fox_layer(h, norm_g[1, 0], fox_w_in[0], fox_b_f[0], fox_q_gain[0], fox_k_gain[0], fox_w_o[0])
    out = _moe_layer(h, o, w_o, norm_g[1, 1], final_g, moe_w_router[0], moe_w_gu[0], moe_w_down[0])
    return out.reshape(B, T, D)
```

```python
import functools

import jax
import jax.numpy as jnp
from jax import lax
from jax.experimental import pallas as pl
from jax.experimental.pallas import tpu as pltpu

F32 = jnp.float32
BF16 = jnp.bfloat16

HEAD_DIM = 64
PAIR = 2 * HEAD_DIM
GN_EPS = 64e-5
RMS_EPS = 1e-6
N_EXPERTS = 8
SCAN_CHUNK = 64
SCAN_CHUNKS_PER_STEP = 4
MOE_ROWS = 1024
MOE_HALF = 512
MOE_SUB = 128
MOE_CHUNK = 512
ROUTER_PARTS = 2
ATTN_BLOCK = 512
LOG2E = 1.4426950408889634
VMEM_LIMIT = 56 * 1024 * 1024
VMEM_LIMIT_MOE = 60 * 1024 * 1024


def _cparams(*sem, vmem_limit=VMEM_LIMIT):
    return pltpu.CompilerParams(dimension_semantics=sem, vmem_limit_bytes=vmem_limit)


def _dot(a, b):
    return jnp.dot(a.astype(BF16), b.astype(BF16), preferred_element_type=F32)


def _dot_nt(a, b):
    return lax.dot_general(a.astype(BF16), b.astype(BF16), (((1,), (1,)), ((), ())),
                           preferred_element_type=F32)


def _dot_tn(a, b):
    return lax.dot_general(a.astype(BF16), b.astype(BF16), (((0,), (0,)), ((), ())),
                           preferred_element_type=F32)


def _split_hi_lo(x):
    hi = x.astype(BF16)
    lo = (x - hi.astype(F32)).astype(BF16)
    return hi, lo


def _rms(x, g):
    ms = jnp.mean(x * x, axis=-1, keepdims=True)
    return x * lax.rsqrt(ms + RMS_EPS) * g


def _sigmoid(x):
    return 1.0 / (1.0 + jnp.exp(-x))


def _full(shape):
    return pl.BlockSpec(shape, lambda *_: (0,) * len(shape))


def _rwkv_pre_kernel(h_ref, hp_ref, ng_ref, mu_ref, wrkv_ref, w0_ref, w1_ref, w2_ref, a0_ref,
                     a1_ref, a2_ref, g1_ref, g2_ref, r_ref, k_ref, v_ref, lw_ref, a_ref, g_ref):
    i = pl.program_id(0)
    ng = ng_ref[...]
    u = _rms(h_ref[...], ng)
    up = _rms(hp_ref[7:8, :], ng)
    up = jnp.where(i == 0, 0.0, up)
    row = lax.broadcasted_iota(jnp.int32, u.shape, 0)
    u_prev = jnp.where(row == 0, up, pltpu.roll(u, 1, axis=0))
    dx = u_prev - u
    mix = lambda n: u + dx * mu_ref[n:n + 1, :]
    w_mid = jnp.tanh(_dot(mix(3), w1_ref[...]))
    a_mid = _dot(mix(4), a1_ref[...])
    g_mid = _sigmoid(_dot(mix(5), g1_ref[...]))
    r_ref[...] = _dot(mix(0), wrkv_ref[0])
    k_ref[...] = _dot(mix(1), wrkv_ref[1])
    v_ref[...] = _dot(mix(2), wrkv_ref[2])
    z = w0_ref[...] + _dot(w_mid, w2_ref[...])
    softplus_neg = jnp.maximum(-z, 0.0) + jnp.log(1.0 + jnp.exp(-jnp.abs(z)))
    lw_ref[...] = -jnp.exp(-softplus_neg - 0.5)
    a_ref[...] = _sigmoid(a0_ref[...] + _dot(a_mid, a2_ref[...]))
    g_ref[...] = _dot(g_mid, g2_ref[...])


def _rwkv_pre(h, ng, mu, wrkv, w0, w1, w2, a0, a1, a2, g1, g2, tm=512):
    T, D = h.shape
    row = pl.BlockSpec((tm, D), lambda i: (i, 0))
    prev = pl.BlockSpec((8, D), lambda i: (jnp.maximum(i * (tm // 8) - 1, 0), 0))
    ins = [ng, mu, wrkv, w0, w1, w2, a0, a1, a2, g1, g2]
    return pl.pallas_call(
        _rwkv_pre_kernel,
        grid=(T // tm,),
        in_specs=[row, prev] + [_full(a.shape) for a in ins],
        out_specs=[row] * 6,
        out_shape=[jax.ShapeDtypeStruct((T, D), F32)] * 6,
        compiler_params=_cparams("parallel"),
        name="rwkv_pre",
    )(h, h, *ins)


def _scan_kernel(r_ref, k_ref, v_ref, lw_ref, a_ref, g_ref, kk_ref, ka_ref, rk_ref, gng_ref,
                 gnb_ref, o_ref, s_ref):
    L = SCAN_CHUNK
    n_pairs = r_ref.shape[1] // PAIR

    @pl.when(pl.program_id(0) == 0)
    def _():
        s_ref[...] = jnp.zeros_like(s_ref)

    row = lax.broadcasted_iota(jnp.int32, (L, L), 0)
    col = lax.broadcasted_iota(jnp.int32, (L, L), 1)
    tri_incl = jnp.where(col <= row, 1.0, 0.0).astype(BF16)
    eye = jnp.where(col == row, 1.0, 0.0).astype(F32)
    lane = lax.broadcasted_iota(jnp.int32, (1, PAIR), 1)
    head0 = lane < HEAD_DIM
    arow = lax.broadcasted_iota(jnp.int32, (4 * L, PAIR), 0)
    acol = lax.broadcasted_iota(jnp.int32, (4 * L, PAIR), 1)
    t_idx = arow % L + jnp.where(arow < 2 * L, 0, 1)
    j_idx = acol % L
    a_mask = j_idx < t_idx
    brow = lax.broadcasted_iota(jnp.int32, (PAIR, PAIR), 0)
    bcol = lax.broadcasted_iota(jnp.int32, (PAIR, PAIR), 1)
    blockdiag = (brow < HEAD_DIM) == (bcol < HEAD_DIM)

    def head_sum(x):
        s0 = jnp.sum(jnp.where(head0, x, 0.0), axis=-1, keepdims=True)
        s1 = jnp.sum(jnp.where(head0, 0.0, x), axis=-1, keepdims=True)
        return jnp.where(head0, s0, s1)

    pairs = range(n_pairs)
    heads = [(p, h) for p in pairs for h in range(2)]
    sls = [slice(p * PAIR, (p + 1) * PAIR) for p in pairs]
    n_chunks = r_ref.shape[0] // L
    chunks = range(n_chunks)
    rows = [slice(c * L, (c + 1) * L) for c in chunks]
    cp = [(c, p) for c in chunks for p in pairs]
    cph = [(c, p, h) for c in chunks for p, h in heads]

    lws = {(c, p): lw_ref[rows[c], sls[p]] for c, p in cp}
    css = {}
    for k, lw in lws.items():
        both = jnp.dot(tri_incl, jnp.concatenate(_split_hi_lo(lw), axis=1), preferred_element_type=F32)
        css[k] = both[:, :PAIR] + both[:, PAIR:]
    prep = {}
    for c, p in cp:
        rw, sl, cs, lw = rows[c], sls[p], css[c, p], lws[c, p]
        w_inv = jnp.exp(-cs)
        w_last = jnp.exp(cs[L - 1:L, :])
        r = r_ref[rw, sl]
        k = k_ref[rw, sl]
        a = a_ref[rw, sl]
        kk = k * kk_ref[:, sl]
        kk = kk / jnp.maximum(jnp.sqrt(head_sum(kk * kk)), 1e-12)
        kmod = k * (1.0 + (a - 1.0) * ka_ref[:, sl])
        at = -kk * jnp.exp(cs - lw)
        bt = kk * a * w_inv
        kt = kmod * w_inv
        rt = r * jnp.exp(cs)
        v = v_ref[rw, sl]
        prep[c, p] = dict(atrt=jnp.concatenate([at, rt], axis=0).astype(BF16),
                          x_stack=jnp.concatenate([jnp.where(head0, at, 0.0), jnp.where(head0, 0.0, at),
                                                   jnp.where(head0, rt, 0.0), jnp.where(head0, 0.0, rt)],
                                                  axis=0).astype(BF16),
                          bk=jnp.concatenate([bt, kt], axis=0).astype(BF16),
                          bkhat=jnp.concatenate([bt * w_last, kt * w_last], axis=0).astype(BF16),
                          w_last=w_last, v=v, v_bf=v.astype(BF16), rk=r * kmod)
    aas = {k: jnp.where(a_mask, _dot_nt(prep[k]["x_stack"], prep[k]["bk"]), 0.0) for k in cp}
    akv = {}
    pw = {}
    t_inv = {}
    for c, p in cp:
        vv = jnp.concatenate([prep[c, p]["v_bf"], prep[c, p]["v_bf"]], axis=0)
        both = _dot(jnp.where(head0, 0.0, aas[c, p][:2 * L, :]), vv)
        akv[c, p, 0], akv[c, p, 1] = both[:L], both[L:]
    for c, p, h in cph:
        a_rows = aas[c, p][h * L:(h + 1) * L, :]
        pw[c, p, h] = a_rows[:, :L]
        t_inv[c, p, h] = eye + pw[c, p, h]
    for _ in range(5):
        for k in cph:
            pw[k] = _dot(pw[k], pw[k])
        for k in cph:
            t_inv[k] = t_inv[k] + _dot(t_inv[k], pw[k])

    states = [s_ref[p] for p in pairs]
    for c in chunks:
        inter = [_dot_nt(prep[c, p]["atrt"], states[p]) for p in pairs]
        u_h = {(p, h): _dot(t_inv[c, p, h], inter[p][:L, :] + akv[c, p, h]) for p, h in heads}
        y_h = {}
        for p, h in heads:
            ar = aas[c, p][(2 + h) * L:(3 + h) * L, :]
            y_h[p, h] = _dot(ar, jnp.concatenate([u_h[p, h].astype(BF16), prep[c, p]["v_bf"]], axis=0))
        new_states = []
        for p in pairs:
            pr = prep[c, p]
            u = jnp.where(head0, u_h[p, 0], u_h[p, 1])
            uv = jnp.concatenate([u.astype(BF16), pr["v_bf"]], axis=0)
            new_states.append(states[p] * pr["w_last"] + jnp.where(blockdiag, _dot_tn(uv, pr["bkhat"]), 0.0))
        states = new_states
        for p in pairs:
            sl = sls[p]
            pr = prep[c, p]
            y = inter[p][L:, :] + jnp.where(head0, y_h[p, 0], y_h[p, 1])
            mean = head_sum(y) * (1.0 / HEAD_DIM)
            d = y - mean
            var = head_sum(d * d) * (1.0 / HEAD_DIM)
            yn = d * lax.rsqrt(var + GN_EPS) * gng_ref[:, sl] + gnb_ref[:, sl]
            bonus = head_sum(pr["rk"] * rk_ref[:, sl]) * pr["v"]
            o_ref[rows[c], sl] = ((yn + bonus) * g_ref[rows[c], sl]).astype(o_ref.dtype)
    for p in pairs:
        s_ref[p] = states[p]


def _rwkv_scan(r, k, v, lw, a, g, k_k, k_a, r_k, gn_g, gn_b):
    T, D = r.shape
    rows = SCAN_CHUNK * SCAN_CHUNKS_PER_STEP
    row = pl.BlockSpec((rows, D), lambda c: (c, 0))
    par = _full((1, D))
    return pl.pallas_call(
        _scan_kernel,
        grid=(T // rows,),
        in_specs=[row] * 6 + [par] * 5,
        out_specs=row,
        out_shape=jax.ShapeDtypeStruct((T, D), BF16),
        scratch_shapes=[pltpu.VMEM((D // PAIR, PAIR, PAIR), F32)],
        compiler_params=_cparams("arbitrary"),
        name="rwkv_scan",
    )(r, k, v, lw, a, g, k_k, k_a, r_k, gn_g, gn_b)


def _swiglu_step(u_ref, wg_ref, wu_ref, wd_ref, acc_ref):
    u = u_ref[...]
    gate = jnp.dot(u, wg_ref[...].astype(BF16), preferred_element_type=F32)
    up = jnp.dot(u, wu_ref[...].astype(BF16), preferred_element_type=F32)
    hid = (gate * _sigmoid(gate) * up).astype(BF16)
    acc_ref[...] += jnp.dot(hid, wd_ref[...].astype(BF16), preferred_element_type=F32)


def _ffn_kernel(h_ref, x_ref, wo_ref, ng_ref, wg_ref, wu_ref, wd_ref, o_ref, u_ref):
    @pl.when(pl.program_id(1) == 0)
    def _():
        h1 = h_ref[...] + jnp.dot(x_ref[...], wo_ref[...], preferred_element_type=F32)
        o_ref[...] = h1
        u_ref[...] = _rms(h1, ng_ref[...]).astype(BF16)

    _swiglu_step(u_ref, wg_ref, wu_ref, wd_ref, o_ref)


def _ffn(h, x, w_o, ng, w_gu, w_down, tm=512, tf=1792):
    T, D = h.shape
    F = w_down.shape[0]
    nf = F // tf
    row = pl.BlockSpec((tm, D), lambda i, f: (i, 0))
    return pl.pallas_call(
        _ffn_kernel,
        grid=(T // tm, nf),
        in_specs=[row, row, _full((D, D)), _full((1, D)),
                  pl.BlockSpec((D, tf), lambda i, f: (0, f)),
                  pl.BlockSpec((D, tf), lambda i, f: (0, f + nf)),
                  pl.BlockSpec((tf, D), lambda i, f: (f, 0))],
        out_specs=row,
        out_shape=jax.ShapeDtypeStruct((T, D), F32),
        scratch_shapes=[pltpu.VMEM((tm, D), BF16)],
        compiler_params=_cparams("parallel", "arbitrary"),
        name="ffn",
    )(h, x, w_o, ng, w_gu, w_gu, w_down)


def _moe_ffn_kernel(be_ref, live_ref, clo_ref, chi_ref, dest_ref, tok_hbm, wg_ref, wu_ref, wd_ref, o_ref,
                    tok_ref, x_ref, sem):
    b = pl.program_id(0)
    f = pl.program_id(1)
    rows = o_ref.shape[0]
    ch = MOE_CHUNK
    halves = rows // MOE_HALF

    @pl.when((b == 0) & (f == 0))
    def _():
        cp = pltpu.make_async_copy(tok_hbm, tok_ref, sem)
        cp.start()
        cp.wait()

    @pl.when(f == 0)
    def _():
        o_ref[...] = jnp.zeros_like(o_ref)
        for s in range(rows // MOE_SUB):
            sub = slice(s * MOE_SUB, (s + 1) * MOE_SUB)
            row_id = b * rows + s * MOE_SUB + lax.broadcasted_iota(jnp.int32, (MOE_SUB, 1), 0)
            w = b * (rows // MOE_SUB) + s

            def chunk(c, _, sub=sub, row_id=row_id):
                start = pl.multiple_of(c * ch, ch)
                d = dest_ref[:, pl.ds(start, ch)]
                onehot = (jnp.where(row_id == d[0:1, :], 1.0, 0.0)
                          + jnp.where(row_id == d[1:2, :], 1.0, 0.0)).astype(BF16)
                o_ref[sub, :] += jnp.dot(onehot, tok_ref[pl.ds(start, ch), :], preferred_element_type=F32)
                return 0

            lax.fori_loop(clo_ref[w], chi_ref[w] + 1, chunk, 0)
        x_ref[...] = o_ref[...].astype(BF16)
        o_ref[...] = jnp.zeros_like(o_ref)

    for half in range(halves):
        @pl.when(live_ref[b * halves + half] > 0)
        def _(half=half):
            part = pl.ds(half * MOE_HALF, MOE_HALF)
            _swiglu_step(x_ref.at[part], wg_ref.at[0], wu_ref.at[0], wd_ref.at[0], o_ref.at[part])


def _moe_ffn(block_e, half_live, c_lo, c_hi, dest_t, tok, w_gu, w_down, n_blocks, tf=512):
    T, D = tok.shape
    F = w_down.shape[1]
    nf = F // tf
    tm = MOE_ROWS

    def fstep(b, f, live):
        return jnp.where(live[b * (tm // MOE_HALF)] > 0, f, nf - 1)

    grid_spec = pltpu.PrefetchScalarGridSpec(
        num_scalar_prefetch=4,
        grid=(n_blocks, nf),
        in_specs=[pl.BlockSpec((2, T), lambda b, f, *_: (0, 0)),
                  pl.BlockSpec(memory_space=pl.ANY),
                  pl.BlockSpec((1, D, tf), lambda b, f, be, live, *_: (be[b], 0, fstep(b, f, live))),
                  pl.BlockSpec((1, D, tf), lambda b, f, be, live, *_: (be[b], 0, fstep(b, f, live) + nf)),
                  pl.BlockSpec((1, tf, D), lambda b, f, be, live, *_: (be[b], fstep(b, f, live), 0))],
        out_specs=pl.BlockSpec((tm, D), lambda b, f, *_: (b, 0)),
        scratch_shapes=[pltpu.VMEM((T, D), BF16), pltpu.VMEM((tm, D), BF16), pltpu.SemaphoreType.DMA(())],
    )
    return pl.pallas_call(
        _moe_ffn_kernel,
        grid_spec=grid_spec,
        out_shape=jax.ShapeDtypeStruct((n_blocks * tm, D), F32),
        compiler_params=_cparams("arbitrary", "arbitrary", vmem_limit=VMEM_LIMIT_MOE),
        name="moe_ffn",
    )(block_e, half_live, c_lo, c_hi, dest_t, tok, w_gu, w_gu, w_down)


def _fox_pre_kernel(h_ref, ng_ref, w_ref, wf_ref, bf_ref, qg_ref, kg_ref, sel_ref, selt_ref, pq_ref, pk_ref,
                    oneq_ref, onek_ref, q_ref, k_ref, v_ref, og_ref, c2_ref, augq_ref, augk_ref, carry_ref):
    i = pl.program_id(0)
    tm, D = h_ref.shape
    H = c2_ref.shape[1]

    @pl.when(i == 0)
    def _():
        carry_ref[...] = jnp.zeros_like(carry_ref)

    u = _rms(h_ref[...], ng_ref[...]).astype(BF16)

    q = jnp.dot(u, w_ref[:, 0:D], preferred_element_type=F32)
    k = jnp.dot(u, w_ref[:, D:2 * D], preferred_element_type=F32)
    fl = jnp.dot(u, wf_ref[...], preferred_element_type=F32) + bf_ref[...]
    v_ref[...] = jnp.dot(u, w_ref[:, 2 * D:3 * D], preferred_element_type=F32).astype(BF16)
    og_ref[...] = _sigmoid(jnp.dot(u, w_ref[:, 3 * D:4 * D], preferred_element_type=F32)).astype(BF16)

    mean_sq = [_dot(x * x, sel_ref[...]) * (1.0 / HEAD_DIM) for x in (q, k)]

    log_f = jnp.minimum(fl, 0.0) - jnp.log(1.0 + jnp.exp(-jnp.abs(fl)))
    row = lax.broadcasted_iota(jnp.int32, (tm, tm), 0)
    col = lax.broadcasted_iota(jnp.int32, (tm, tm), 1)
    tri = jnp.where(col <= row, 1.0, 0.0).astype(BF16)
    hi, lo = _split_hi_lo(log_f)
    lo2 = (log_f - hi.astype(F32) - lo.astype(F32)).astype(BF16)
    c = (jnp.dot(tri, hi, preferred_element_type=F32) + jnp.dot(tri, lo, preferred_element_type=F32)
         + jnp.dot(tri, lo2, preferred_element_type=F32)) + carry_ref[...]
    carry_ref[...] = c[tm - 1:tm, :]

    def spread(ms):
        inv = lax.rsqrt(ms + RMS_EPS)
        inv_hi, inv_lo = _split_hi_lo(inv)
        first = lax.broadcasted_iota(jnp.int32, inv.shape, 1) < H
        return jnp.dot(jnp.where(first, inv_hi, inv_lo), selt_ref[...], preferred_element_type=F32)

    q_inv, k_inv = [spread(ms) for ms in mean_sq]
    q_ref[...] = (q * q_inv * qg_ref[...] * (HEAD_DIM ** -0.5 * LOG2E)).astype(BF16)
    k_ref[...] = (k * k_inv * kg_ref[...]).astype(BF16)

    c2 = c * LOG2E
    c2_ref[...] = c2[:, 0:H]
    c_hi, c_lo = _split_hi_lo(c2)
    c_lo2 = (c2 - c_hi.astype(F32) - c_lo.astype(F32)).astype(BF16)
    group = lax.broadcasted_iota(jnp.int32, c2.shape, 1) // H
    pieces = jnp.where(group == 0, c_hi, jnp.where(group == 1, c_lo, c_lo2))
    augq_ref[...] = (jnp.dot(pieces, pq_ref[...], preferred_element_type=F32) + oneq_ref[...]).astype(BF16)
    augk_ref[...] = (jnp.dot(pieces, pk_ref[...], preferred_element_type=F32) + onek_ref[...]).astype(BF16)


def _fox_pre(h, ng, w_main, w_f, b_f, q_gain, k_gain, tm=512):
    T, D = h.shape
    H = D // HEAD_DIM
    sel = (jnp.arange(D)[:, None] // HEAD_DIM == jnp.arange(2 * H)[None, :] % H).astype(BF16)
    piece = jnp.arange(3 * H)[:, None] // H
    head = jnp.arange(3 * H)[:, None] % H
    lane = jnp.arange(D)[None, :]
    at = lambda j: lane == head * HEAD_DIM + j
    p_q = at(piece).astype(BF16)
    p_k = -at(piece + 3).astype(BF16)
    off = jnp.arange(D) % HEAD_DIM
    one_q = ((off >= 3) & (off < 6)).astype(F32).reshape(1, D)
    one_k = (off < 3).astype(F32).reshape(1, D)
    row = pl.BlockSpec((tm, D), lambda i: (i, 0))
    ins = [ng, w_main, jnp.tile(w_f, (1, 3)), jnp.tile(b_f, (1, 3)), q_gain, k_gain, sel, sel.T, p_q, p_k,
           one_q, one_k]
    return pl.pallas_call(
        _fox_pre_kernel,
        grid=(T // tm,),
        in_specs=[row] + [_full(a.shape) for a in ins],
        out_specs=[row] * 4 + [pl.BlockSpec((tm, H), lambda i: (i, 0))] + [row] * 2,
        out_shape=([jax.ShapeDtypeStruct((T, D), BF16)] * 4 + [jax.ShapeDtypeStruct((T, H), F32)]
                   + [jax.ShapeDtypeStruct((T, D), BF16)] * 2),
        scratch_shapes=[pltpu.VMEM((1, 3 * H), F32)],
        compiler_params=_cparams("arbitrary"),
        name="fox_pre",
    )(h, *ins)


def _fox_attn_kernel(jlo_ref, q_ref, augq_ref, k_ref, augk_ref, vt_ref, og_ref, o_ref, s_ref, *, blk):
    p_idx = pl.program_id(0)
    i = pl.program_id(1)
    head0 = lax.broadcasted_iota(jnp.int32, (1, PAIR), 1) < HEAD_DIM
    q, augq = q_ref[...], augq_ref[...]
    zero = jnp.zeros_like(q)
    q_heads = [jnp.concatenate([jnp.where(m, q, zero), jnp.where(m, augq, zero)], axis=1)
               for m in (head0, jnp.logical_not(head0))]
    causal = (lax.broadcasted_iota(jnp.int32, (blk, blk), 0)
              <= lax.broadcasted_iota(jnp.int32, (blk, blk), 1))

    def scores_into(j, slot):
        start = pl.multiple_of(j * blk, blk)
        kb = jnp.concatenate([k_ref[pl.ds(start, blk), :], augk_ref[pl.ds(start, blk), :]], axis=1)
        for h in range(2):
            s_ref[slot, h] = lax.dot_general(kb, q_heads[h], (((1,), (1,)), ((), ())),
                                             preferred_element_type=F32)

    def softmax_pv(j, slot, carry, masked):
        start = pl.multiple_of(j * blk, blk)
        out = []
        for h in range(2):
            m_prev, l_prev, acc_prev = carry[h]
            vtb = vt_ref[h * HEAD_DIM:(h + 1) * HEAD_DIM, pl.ds(start, blk)]
            s = s_ref[slot, h]
            if masked:
                s = jnp.where(causal, s, -jnp.inf)
            m_new = jnp.maximum(m_prev, jnp.max(s, axis=0, keepdims=True))
            alpha = jnp.exp2(m_prev - m_new)
            p = jnp.exp2(s - m_new)
            l_new = alpha * l_prev + jnp.sum(p, axis=0, keepdims=True)
            acc_new = alpha * acc_prev + jnp.dot(vtb, p.astype(BF16), preferred_element_type=F32)
            out.append((m_new, l_new, acc_new))
        return tuple(out)

    init = tuple((jnp.full((1, blk), -jnp.inf, F32), jnp.zeros((1, blk), F32), jnp.zeros((HEAD_DIM, blk), F32))
                 for _ in range(2))
    jlo = jlo_ref[p_idx * pl.num_programs(1) + i]
    total = i - jlo + 1
    scores_into(jlo, 0)

    def two_blocks(t, carry):
        j = jlo + 2 * t
        scores_into(j + 1, 1)
        carry = softmax_pv(j, 0, carry, False)
        scores_into(j + 2, 0)
        return softmax_pv(j + 1, 1, carry, False)

    res = lax.fori_loop(0, (total - 1) // 2, two_blocks, init)

    def last_one(carry):
        return softmax_pv(i, 0, carry, True)

    def last_two(carry):
        scores_into(i, 1)
        carry = softmax_pv(i - 1, 0, carry, False)
        return softmax_pv(i, 1, carry, True)

    res = lax.cond(total % 2 == 1, last_one, last_two, res)
    o_t = jnp.concatenate([res[0][2] / res[0][1], res[1][2] / res[1][1]], axis=0)
    o_ref[...] = (o_t.T * og_ref[...].astype(F32)).astype(o_ref.dtype)


def _fox_attn(jlo, q, augq, k, augk, v_t, og, blk):
    T, D = q.shape
    n_pairs = D // PAIR
    qspec = pl.BlockSpec((blk, PAIR), lambda p, i, jlo: (i, p))
    kspec = pl.BlockSpec((T, PAIR), lambda p, i, jlo: (0, p))
    grid_spec = pltpu.PrefetchScalarGridSpec(
        num_scalar_prefetch=1,
        grid=(n_pairs, T // blk),
        in_specs=[qspec, qspec, kspec, kspec,
                  pl.BlockSpec((PAIR, T), lambda p, i, jlo: (p, 0)),
                  qspec],
        out_specs=qspec,
        scratch_shapes=[pltpu.VMEM((2, 2, blk, blk), F32)],
    )
    return pl.pallas_call(
        functools.partial(_fox_attn_kernel, blk=blk),
        grid_spec=grid_spec,
        out_shape=jax.ShapeDtypeStruct((T, D), BF16),
        compiler_params=_cparams("parallel", "arbitrary"),
        name="fox_attn",
    )(jlo, q, augq, k, augk, v_t, og)


def _first_live_block(c2, q_gain, k_gain, blk):
    T, H = c2.shape
    bound = 8.0 * LOG2E * 1.02 * jnp.max(jnp.abs(q_gain)) * jnp.max(jnp.abs(k_gain))
    c_first = c2[0::blk]
    c_last = c2[blk - 1::blk]
    dead = c_last[None, :, :] > c_first[:, None, :] + (2.0 * bound + 104.0 * LOG2E + 1.0)
    jlo = jnp.sum(dead, axis=1).astype(jnp.int32)
    jlo = jnp.min(jlo.reshape(-1, H // 2, 2), axis=-1)
    return jlo.T.reshape(-1)


def _router_kernel(h_ref, x_ref, wo_ref, ng_ref, wr_ref, h1_ref, u_ref, info_ref, cnt_ref, csr_ref, carry_ref):
    i = pl.program_id(0)
    tm = h_ref.shape[0]
    E = N_EXPERTS
    parts = ROUTER_PARTS if tm % (8 * ROUTER_PARTS) == 0 else 1
    pm = tm // parts
    rows = [slice(j * pm, (j + 1) * pm) for j in range(parts)]

    @pl.when(i == 0)
    def _():
        carry_ref[...] = jnp.zeros_like(carry_ref)

    csr_ref[0] = carry_ref[...]
    h1 = [h_ref[r, :] + jnp.dot(x_ref[r, :], wo_ref[...], preferred_element_type=F32) for r in rows]
    for r, z in zip(rows, h1):
        h1_ref[r, :] = z
    u = [_rms(z, ng_ref[...]) for z in h1]
    for r, z in zip(rows, u):
        u_ref[r, :] = z.astype(BF16)
    w_hi, w_lo = _split_hi_lo(wr_ref[...])
    logits = []
    for z in u:
        u_hi, u_lo = _split_hi_lo(z)
        logits.append(jnp.dot(u_hi, w_hi, preferred_element_type=F32)
                      + jnp.dot(u_hi, w_lo, preferred_element_type=F32)
                      + jnp.dot(u_lo, w_hi, preferred_element_type=F32))
    lane = lax.broadcasted_iota(jnp.int32, (pm, E), 1)
    row = lax.broadcasted_iota(jnp.int32, (pm, pm), 0)
    col = lax.broadcasted_iota(jnp.int32, (pm, pm), 1)
    tri_strict = jnp.where(col < row, 1.0, 0.0).astype(BF16)
    picks = []
    for lg in logits:
        v1 = jnp.max(lg, axis=-1, keepdims=True)
        e1 = jnp.min(jnp.where(lg == v1, lane, E), axis=-1, keepdims=True)
        rest = jnp.where(lane == e1, -jnp.inf, lg)
        v2 = jnp.max(rest, axis=-1, keepdims=True)
        e2 = jnp.min(jnp.where(rest == v2, lane, E), axis=-1, keepdims=True)
        g1 = 1.0 / (1.0 + jnp.exp(v2 - v1))
        oh1 = lane == e1
        oh2 = lane == e2
        onehot = jnp.where(oh1 | oh2, 1.0, 0.0).astype(BF16)
        picks.append((e1, e2, g1, oh1, oh2, onehot))
    within = [jnp.dot(tri_strict, p[5], preferred_element_type=F32) for p in picks]
    counted = carry_ref[...]
    for r, (e1, e2, g1, oh1, oh2, onehot), inside in zip(rows, picks, within):
        before = inside + counted
        rank1 = jnp.sum(jnp.where(oh1, before, 0.0), axis=-1, keepdims=True)
        rank2 = jnp.sum(jnp.where(oh2, before, 0.0), axis=-1, keepdims=True)
        info = jnp.where(lane == 0, e1.astype(F32), 0.0)
        info = jnp.where(lane == 1, e2.astype(F32), info)
        info = jnp.where(lane == 2, g1, info)
        info = jnp.where(lane == 3, 1.0 - g1, info)
        info = jnp.where(lane == 4, rank1, info)
        info = jnp.where(lane == 5, rank2, info)
        info_ref[r, :] = info
        counted = counted + jnp.sum(onehot.astype(F32), axis=0, keepdims=True)
    carry_ref[...] = counted
    cnt_ref[...] = counted


def _router(h, x, w_o, ng, w_router):
    T, D = h.shape
    E = N_EXPERTS
    tm = min(MOE_CHUNK, T)
    row = pl.BlockSpec((tm, D), lambda i: (i, 0))
    return pl.pallas_call(
        _router_kernel,
        grid=(T // tm,),
        in_specs=[row, row, _full((D, D)), _full((1, D)), _full((D, E))],
        out_specs=[row, row, pl.BlockSpec((tm, E), lambda i: (i, 0)), _full((1, E)),
                   pl.BlockSpec((1, 1, E), lambda i: (i, 0, 0))],
        out_shape=[jax.ShapeDtypeStruct((T, D), F32), jax.ShapeDtypeStruct((T, D), BF16),
                   jax.ShapeDtypeStruct((T, E), F32), jax.ShapeDtypeStruct((1, E), F32),
                   jax.ShapeDtypeStruct((T // tm, 1, E), F32)],
        scratch_shapes=[pltpu.VMEM((1, E), F32)],
        compiler_params=_cparams("arbitrary"),
        name="router",
    )(h, x, w_o, ng, w_router)


def _combine_kernel(pos_ref, h_ref, gate_ref, fg_ref, y_ref, o_ref, ya_ref, yb_ref, sem):
    b = pl.program_id(0)
    nb = pl.num_programs(0)
    rows = h_ref.shape[0]

    def copies(blk, r):
        slot = blk % 2
        base = 2 * (blk * rows + r)
        return (pltpu.make_async_copy(y_ref.at[pl.ds(pos_ref[base], 1)], ya_ref.at[slot, pl.ds(r, 1)],
                                      sem.at[slot]),
                pltpu.make_async_copy(y_ref.at[pl.ds(pos_ref[base + 1], 1)], yb_ref.at[slot, pl.ds(r, 1)],
                                      sem.at[slot]))

    def issue(blk):
        def body(r, _):
            ca, cb = copies(blk, r)
            ca.start()
            cb.start()
            return 0
        lax.fori_loop(0, rows, body, 0, unroll=8)

    @pl.when(b == 0)
    def _():
        issue(0)

    @pl.when(b + 1 < nb)
    def _():
        issue(b + 1)

    def wait(r, _):
        ca, cb = copies(b, r)
        ca.wait()
        cb.wait()
        return 0

    lax.fori_loop(0, rows, wait, 0, unroll=8)
    slot = b % 2
    gate = gate_ref[...]
    moe = gate[:, 0:1] * ya_ref[slot] + gate[:, 1:2] * yb_ref[slot]
    o_ref[...] = _rms(h_ref[...] + moe, fg_ref[...])


def _combine(pos, h, gate, final_g, y, rows=512):
    T, D = h.shape
    row = pl.BlockSpec((rows, D), lambda b, pos: (b, 0))
    grid_spec = pltpu.PrefetchScalarGridSpec(
        num_scalar_prefetch=1,
        grid=(T // rows,),
        in_specs=[row, pl.BlockSpec((rows, 2), lambda b, pos: (b, 0)),
                  pl.BlockSpec((1, D), lambda b, pos: (0, 0)), pl.BlockSpec(memory_space=pl.ANY)],
        out_specs=row,
        scratch_shapes=[pltpu.VMEM((2, rows, D), F32), pltpu.VMEM((2, rows, D), F32),
                        pltpu.SemaphoreType.DMA((2,))],
    )
    return pl.pallas_call(
        _combine_kernel,
        grid_spec=grid_spec,
        out_shape=jax.ShapeDtypeStruct((T, D), F32),
        compiler_params=_cparams("arbitrary"),
        name="moe_combine",
    )(pos, h, gate, final_g, y)


def _rwkv_layer(h, ng, mu, w_rkv, w0, w1, w2, a0, a1, a2, g1, g2, k_k, k_a, r_k, gn_g, gn_b, w_o):
    D = h.shape[1]
    vec = lambda z: z.reshape(1, D)
    b16 = lambda z: z.astype(BF16)
    r, k, v, lw, a, g = _rwkv_pre(h, vec(ng), mu, b16(w_rkv), vec(w0), b16(w1), b16(w2), vec(a0),
                                  b16(a1), b16(a2), b16(g1), b16(g2))
    o = _rwkv_scan(r, k, v, lw, a, g, vec(k_k), vec(k_a), vec(r_k), vec(gn_g), vec(gn_b))
    return o, b16(w_o)


def _fox_layer(h, ng, w_in, b_f, q_gain, k_gain, w_o):
    T, D = h.shape
    H = D // HEAD_DIM
    order = jnp.argsort(b_f)
    w_main = w_in[:, :4 * D].reshape(D, 4, H, HEAD_DIM)[:, :, order, :].reshape(D, 4 * D)
    w_out = w_o.reshape(H, HEAD_DIM, D)[order].reshape(D, D)
    q, k, v, og, c2, augq, augk = _fox_pre(
        h, ng.reshape(1, D), w_main.astype(BF16), w_in[:, 4 * D:][:, order].astype(BF16),
        b_f[order].reshape(1, H), jnp.tile(q_gain, H).reshape(1, D), jnp.tile(k_gain, H).reshape(1, D))
    blk = min(ATTN_BLOCK, T)
    jlo = _first_live_block(c2, q_gain, k_gain, blk)
    o = _fox_attn(jlo, q, augq, k, augk, v.T, og, blk)
    return o, w_out.astype(BF16)


def _moe_layer(h, x, w_o, ng, final_g, w_router, w_gu, w_down):
    T, D = h.shape
    E, RB = N_EXPERTS, MOE_ROWS
    h, u, info, cnt, csr = _router(h, x, w_o, ng.reshape(1, D), w_router)
    e = info[:, 0:2].astype(jnp.int32)
    gate = info[:, 2:4]
    rank = info[:, 4:6].astype(jnp.int32)
    counts = cnt[0].astype(jnp.int32)
    padded = (counts + RB - 1) // RB * RB
    ends = jnp.cumsum(padded)
    starts = ends - padded
    n_blocks = (2 * T) // RB + E
    dest = starts[e] + rank
    blocks = jnp.arange(n_blocks, dtype=jnp.int32)
    block_e = jnp.minimum(jnp.sum(blocks[:, None] * RB >= ends[None, :], axis=1), E - 1).astype(jnp.int32)

    def first_rank(size):
        groups = jnp.arange(n_blocks * (RB // size), dtype=jnp.int32)
        group_e = jnp.repeat(block_e, RB // size)
        rank0 = groups * size - starts[group_e]
        return group_e, rank0, rank0 < counts[group_e]

    _, _, half_live = first_rank(MOE_HALF)
    group_e, rank0, group_live = first_rank(MOE_SUB)
    csr_e = csr[:, 0, :].astype(jnp.int32)[:, group_e]
    n_chunks = csr_e.shape[0]
    c_lo = jnp.clip(jnp.sum(csr_e <= rank0[None, :], axis=0) - 1, 0, n_chunks - 1)
    c_hi = jnp.clip(jnp.sum(csr_e < rank0[None, :] + MOE_SUB, axis=0) - 1, 0, n_chunks - 1)
    c_lo = jnp.where(group_live, c_lo, 1).astype(jnp.int32)
    c_hi = jnp.where(group_live, c_hi, 0).astype(jnp.int32)
    y = _moe_ffn(block_e, half_live.astype(jnp.int32), c_lo, c_hi, dest.T, u, w_gu, w_down, n_blocks)
    return _combine(dest.reshape(-1), h, gate, final_g.reshape(1, D), y)


def kernel(x, norm_g, final_g, rwkv_mu, rwkv_w_rkv, rwkv_w0, rwkv_w1, rwkv_w2, rwkv_a0, rwkv_a1,
           rwkv_a2, rwkv_g1, rwkv_g2, rwkv_k_k, rwkv_k_a, rwkv_r_k, rwkv_gn_g, rwkv_gn_b, rwkv_w_o,
           fox_w_in, fox_b_f, fox_q_gain, fox_k_gain, fox_w_o, ffn_w_gu, ffn_w_down,
           moe_w_router, moe_w_gu, moe_w_down):
    B, T, D = x.shape
    h = x.reshape(B * T, D)
    o, w_o = _rwkv_layer(h, norm_g[0, 0], rwkv_mu[0], rwkv_w_rkv[0], rwkv_w0[0], rwkv_w1[0], rwkv_w2[0],
                    rwkv_a0[0], rwkv_a1[0], rwkv_a2[0], rwkv_g1[0], rwkv_g2[0], rwkv_k_k[0],
                    rwkv_k_a[0], rwkv_r_k[0], rwkv_gn_g[0], rwkv_gn_b[0], rwkv_w_o[0])
    h = _ffn(h, o, w_o, norm_g[0, 1].reshape(1, D), ffn_w_gu[0].astype(BF16), ffn_w_down[0].astype(BF16))
    o, w_o = _fox_layer(h, norm_g[1, 0], fox_w_in[0], fox_b_f[0], fox_q_gain[0], fox_k_gain[0], fox_w_o[0])
    out = _moe_layer(h, o, w_o, norm_g[1, 1], final_g, moe_w_router[0], moe_w_gu[0], moe_w_down[0])
    return out.reshape(B, T, D)
```

```python
import functools

import jax
import jax.numpy as jnp
from jax import lax
from jax.experimental import pallas as pl
from jax.experimental.pallas import tpu as pltpu

F32 = jnp.float32
BF16 = jnp.bfloat16

HEAD_DIM = 64
PAIR = 2 * HEAD_DIM
GN_EPS = 64e-5
RMS_EPS = 1e-6
N_EXPERTS = 8
SCAN_CHUNK = 64
SCAN_CHUNKS_PER_STEP = 4
MOE_ROWS = 1024
MOE_HALF = 512
MOE_SUB = 128
MOE_CHUNK = 512
ROUTER_PARTS = 2
ATTN_BLOCK = 512
LOG2E = 1.4426950408889634
VMEM_LIMIT = 56 * 1024 * 1024
VMEM_LIMIT_MOE = 63 * 1024 * 1024


def _cparams(*sem, vmem_limit=VMEM_LIMIT):
    return pltpu.CompilerParams(dimension_semantics=sem, vmem_limit_bytes=vmem_limit)


def _dot(a, b):
    return jnp.dot(a.astype(BF16), b.astype(BF16), preferred_element_type=F32)


def _dot_nt(a, b):
    return lax.dot_general(a.astype(BF16), b.astype(BF16), (((1,), (1,)), ((), ())),
                           preferred_element_type=F32)


def _dot_tn(a, b):
    return lax.dot_general(a.astype(BF16), b.astype(BF16), (((0,), (0,)), ((), ())),
                           preferred_element_type=F32)


def _split_hi_lo(x):
    hi = x.astype(BF16)
    lo = (x - hi.astype(F32)).astype(BF16)
    return hi, lo


def _rms(x, g):
    ms = jnp.mean(x * x, axis=-1, keepdims=True)
    return x * lax.rsqrt(ms + RMS_EPS) * g


def _sigmoid(x):
    return 1.0 / (1.0 + jnp.exp(-x))


def _full(shape):
    return pl.BlockSpec(shape, lambda *_: (0,) * len(shape))


def _rwkv_pre_kernel(h_ref, hp_ref, ng_ref, mu_ref, wrkv_ref, w0_ref, w1_ref, w2_ref, a0_ref,
                     a1_ref, a2_ref, g1_ref, g2_ref, r_ref, k_ref, v_ref, lw_ref, a_ref, g_ref):
    i = pl.program_id(0)
    ng = ng_ref[...]
    u = _rms(h_ref[...], ng)
    up = _rms(hp_ref[7:8, :], ng)
    up = jnp.where(i == 0, 0.0, up)
    row = lax.broadcasted_iota(jnp.int32, u.shape, 0)
    u_prev = jnp.where(row == 0, up, pltpu.roll(u, 1, axis=0))
    dx = u_prev - u
    mix = lambda n: u + dx * mu_ref[n:n + 1, :]
    w_mid = jnp.tanh(_dot(mix(3), w1_ref[...]))
    a_mid = _dot(mix(4), a1_ref[...])
    g_mid = _sigmoid(_dot(mix(5), g1_ref[...]))
    r_ref[...] = _dot(mix(0), wrkv_ref[0])
    k_ref[...] = _dot(mix(1), wrkv_ref[1])
    v_ref[...] = _dot(mix(2), wrkv_ref[2])
    z = w0_ref[...] + _dot(w_mid, w2_ref[...])
    softplus_neg = jnp.maximum(-z, 0.0) + jnp.log(1.0 + jnp.exp(-jnp.abs(z)))
    lw_ref[...] = -jnp.exp(-softplus_neg - 0.5)
    a_ref[...] = _sigmoid(a0_ref[...] + _dot(a_mid, a2_ref[...]))
    g_ref[...] = _dot(g_mid, g2_ref[...])


def _rwkv_pre(h, ng, mu, wrkv, w0, w1, w2, a0, a1, a2, g1, g2, tm=512):
    T, D = h.shape
    row = pl.BlockSpec((tm, D), lambda i: (i, 0))
    prev = pl.BlockSpec((8, D), lambda i: (jnp.maximum(i * (tm // 8) - 1, 0), 0))
    ins = [ng, mu, wrkv, w0, w1, w2, a0, a1, a2, g1, g2]
    return pl.pallas_call(
        _rwkv_pre_kernel,
        grid=(T // tm,),
        in_specs=[row, prev] + [_full(a.shape) for a in ins],
        out_specs=[row] * 6,
        out_shape=[jax.ShapeDtypeStruct((T, D), F32)] * 6,
        compiler_params=_cparams("parallel"),
        name="rwkv_pre",
    )(h, h, *ins)


def _scan_kernel(r_ref, k_ref, v_ref, lw_ref, a_ref, g_ref, kk_ref, ka_ref, rk_ref, gng_ref,
                 gnb_ref, o_ref, s_ref):
    L = SCAN_CHUNK
    n_pairs = r_ref.shape[1] // PAIR

    @pl.when(pl.program_id(0) == 0)
    def _():
        s_ref[...] = jnp.zeros_like(s_ref)

    row = lax.broadcasted_iota(jnp.int32, (L, L), 0)
    col = lax.broadcasted_iota(jnp.int32, (L, L), 1)
    tri_incl = jnp.where(col <= row, 1.0, 0.0).astype(BF16)
    eye = jnp.where(col == row, 1.0, 0.0).astype(F32)
    lane = lax.broadcasted_iota(jnp.int32, (1, PAIR), 1)
    head0 = lane < HEAD_DIM
    arow = lax.broadcasted_iota(jnp.int32, (4 * L, PAIR), 0)
    acol = lax.broadcasted_iota(jnp.int32, (4 * L, PAIR), 1)
    t_idx = arow % L + jnp.where(arow < 2 * L, 0, 1)
    j_idx = acol % L
    a_mask = j_idx < t_idx
    brow = lax.broadcasted_iota(jnp.int32, (PAIR, PAIR), 0)
    bcol = lax.broadcasted_iota(jnp.int32, (PAIR, PAIR), 1)
    blockdiag = (brow < HEAD_DIM) == (bcol < HEAD_DIM)

    def head_sum(x):
        s0 = jnp.sum(jnp.where(head0, x, 0.0), axis=-1, keepdims=True)
        s1 = jnp.sum(jnp.where(head0, 0.0, x), axis=-1, keepdims=True)
        return jnp.where(head0, s0, s1)

    pairs = range(n_pairs)
    heads = [(p, h) for p in pairs for h in range(2)]
    sls = [slice(p * PAIR, (p + 1) * PAIR) for p in pairs]
    n_chunks = r_ref.shape[0] // L
    chunks = range(n_chunks)
    rows = [slice(c * L, (c + 1) * L) for c in chunks]
    cp = [(c, p) for c in chunks for p in pairs]
    cph = [(c, p, h) for c in chunks for p, h in heads]

    lws = {(c, p): lw_ref[rows[c], sls[p]] for c, p in cp}
    css = {}
    for k, lw in lws.items():
        both = jnp.dot(tri_incl, jnp.concatenate(_split_hi_lo(lw), axis=1), preferred_element_type=F32)
        css[k] = both[:, :PAIR] + both[:, PAIR:]
    prep = {}
    for c, p in cp:
        rw, sl, cs, lw = rows[c], sls[p], css[c, p], lws[c, p]
        w_inv = jnp.exp(-cs)
        w_last = jnp.exp(cs[L - 1:L, :])
        r = r_ref[rw, sl]
        k = k_ref[rw, sl]
        a = a_ref[rw, sl]
        kk = k * kk_ref[:, sl]
        kk = kk / jnp.maximum(jnp.sqrt(head_sum(kk * kk)), 1e-12)
        kmod = k * (1.0 + (a - 1.0) * ka_ref[:, sl])
        at = -kk * jnp.exp(cs - lw)
        bt = kk * a * w_inv
        kt = kmod * w_inv
        rt = r * jnp.exp(cs)
        v = v_ref[rw, sl]
        prep[c, p] = dict(atrt=jnp.concatenate([at, rt], axis=0).astype(BF16),
                          x_stack=jnp.concatenate([jnp.where(head0, at, 0.0), jnp.where(head0, 0.0, at),
                                                   jnp.where(head0, rt, 0.0), jnp.where(head0, 0.0, rt)],
                                                  axis=0).astype(BF16),
                          bk=jnp.concatenate([bt, kt], axis=0).astype(BF16),
                          bkhat=jnp.concatenate([bt * w_last, kt * w_last], axis=0).astype(BF16),
                          w_last=w_last, v=v, v_bf=v.astype(BF16), rk=r * kmod)
    aas = {k: jnp.where(a_mask, _dot_nt(prep[k]["x_stack"], prep[k]["bk"]), 0.0) for k in cp}
    akv = {}
    pw = {}
    t_inv = {}
    for c, p in cp:
        vv = jnp.concatenate([prep[c, p]["v_bf"], prep[c, p]["v_bf"]], axis=0)
        both = _dot(jnp.where(head0, 0.0, aas[c, p][:2 * L, :]), vv)
        akv[c, p, 0], akv[c, p, 1] = both[:L], both[L:]
    for c, p, h in cph:
        a_rows = aas[c, p][h * L:(h + 1) * L, :]
        pw[c, p, h] = a_rows[:, :L]
        t_inv[c, p, h] = eye + pw[c, p, h]
    for _ in range(5):
        for k in cph:
            pw[k] = _dot(pw[k], pw[k])
        for k in cph:
            t_inv[k] = t_inv[k] + _dot(t_inv[k], pw[k])

    states = [s_ref[p] for p in pairs]
    for c in chunks:
        inter = [_dot_nt(prep[c, p]["atrt"], states[p]) for p in pairs]
        u_h = {(p, h): _dot(t_inv[c, p, h], inter[p][:L, :] + akv[c, p, h]) for p, h in heads}
        y_h = {}
        for p, h in heads:
            ar = aas[c, p][(2 + h) * L:(3 + h) * L, :]
            y_h[p, h] = _dot(ar, jnp.concatenate([u_h[p, h].astype(BF16), prep[c, p]["v_bf"]], axis=0))
        new_states = []
        for p in pairs:
            pr = prep[c, p]
            u = jnp.where(head0, u_h[p, 0], u_h[p, 1])
            uv = jnp.concatenate([u.astype(BF16), pr["v_bf"]], axis=0)
            new_states.append(states[p] * pr["w_last"] + jnp.where(blockdiag, _dot_tn(uv, pr["bkhat"]), 0.0))
        states = new_states
        for p in pairs:
            sl = sls[p]
            pr = prep[c, p]
            y = inter[p][L:, :] + jnp.where(head0, y_h[p, 0], y_h[p, 1])
            mean = head_sum(y) * (1.0 / HEAD_DIM)
            d = y - mean
            var = head_sum(d * d) * (1.0 / HEAD_DIM)
            yn = d * lax.rsqrt(var + GN_EPS) * gng_ref[:, sl] + gnb_ref[:, sl]
            bonus = head_sum(pr["rk"] * rk_ref[:, sl]) * pr["v"]
            o_ref[rows[c], sl] = ((yn + bonus) * g_ref[rows[c], sl]).astype(o_ref.dtype)
    for p in pairs:
        s_ref[p] = states[p]


def _rwkv_scan(r, k, v, lw, a, g, k_k, k_a, r_k, gn_g, gn_b):
    T, D = r.shape
    rows = SCAN_CHUNK * SCAN_CHUNKS_PER_STEP
    row = pl.BlockSpec((rows, D), lambda c: (c, 0))
    par = _full((1, D))
    return pl.pallas_call(
        _scan_kernel,
        grid=(T // rows,),
        in_specs=[row] * 6 + [par] * 5,
        out_specs=row,
        out_shape=jax.ShapeDtypeStruct((T, D), BF16),
        scratch_shapes=[pltpu.VMEM((D // PAIR, PAIR, PAIR), F32)],
        compiler_params=_cparams("arbitrary"),
        name="rwkv_scan",
    )(r, k, v, lw, a, g, k_k, k_a, r_k, gn_g, gn_b)


def _swiglu_step(u_ref, wg_ref, wu_ref, wd_ref, acc_ref):
    u = u_ref[...]
    gate = jnp.dot(u, wg_ref[...].astype(BF16), preferred_element_type=F32)
    up = jnp.dot(u, wu_ref[...].astype(BF16), preferred_element_type=F32)
    hid = (gate * _sigmoid(gate) * up).astype(BF16)
    acc_ref[...] += jnp.dot(hid, wd_ref[...].astype(BF16), preferred_element_type=F32)


def _ffn_kernel(h_ref, x_ref, wo_ref, ng_ref, wg_ref, wu_ref, wd_ref, o_ref, u_ref):
    @pl.when(pl.program_id(1) == 0)
    def _():
        h1 = h_ref[...] + jnp.dot(x_ref[...], wo_ref[...], preferred_element_type=F32)
        o_ref[...] = h1
        u_ref[...] = _rms(h1, ng_ref[...]).astype(BF16)

    _swiglu_step(u_ref, wg_ref, wu_ref, wd_ref, o_ref)


def _ffn(h, x, w_o, ng, w_gu, w_down, tm=512, tf=1792):
    T, D = h.shape
    F = w_down.shape[0]
    nf = F // tf
    row = pl.BlockSpec((tm, D), lambda i, f: (i, 0))
    return pl.pallas_call(
        _ffn_kernel,
        grid=(T // tm, nf),
        in_specs=[row, row, _full((D, D)), _full((1, D)),
                  pl.BlockSpec((D, tf), lambda i, f: (0, f)),
                  pl.BlockSpec((D, tf), lambda i, f: (0, f + nf)),
                  pl.BlockSpec((tf, D), lambda i, f: (f, 0))],
        out_specs=row,
        out_shape=jax.ShapeDtypeStruct((T, D), F32),
        scratch_shapes=[pltpu.VMEM((tm, D), BF16)],
        compiler_params=_cparams("parallel", "arbitrary"),
        name="ffn",
    )(h, x, w_o, ng, w_gu, w_gu, w_down)


def _moe_ffn_kernel(be_ref, live_ref, clo_ref, chi_ref, dest_ref, tok_hbm, wg_ref, wu_ref, wd_ref, o_ref,
                    tok_ref, x_ref, sem):
    b = pl.program_id(0)
    f = pl.program_id(1)
    rows = o_ref.shape[0]
    ch = MOE_CHUNK
    halves = rows // MOE_HALF

    @pl.when((b == 0) & (f == 0))
    def _():
        cp = pltpu.make_async_copy(tok_hbm, tok_ref, sem)
        cp.start()
        cp.wait()

    @pl.when(f == 0)
    def _():
        o_ref[...] = jnp.zeros_like(o_ref)
        for s in range(rows // MOE_SUB):
            sub = slice(s * MOE_SUB, (s + 1) * MOE_SUB)
            row_id = b * rows + s * MOE_SUB + lax.broadcasted_iota(jnp.int32, (MOE_SUB, 1), 0)
            w = b * (rows // MOE_SUB) + s

            def chunk(c, _, sub=sub, row_id=row_id):
                start = pl.multiple_of(c * ch, ch)
                d = dest_ref[:, pl.ds(start, ch)]
                onehot = (jnp.where(row_id == d[0:1, :], 1.0, 0.0)
                          + jnp.where(row_id == d[1:2, :], 1.0, 0.0)).astype(BF16)
                o_ref[sub, :] += jnp.dot(onehot, tok_ref[pl.ds(start, ch), :], preferred_element_type=F32)
                return 0

            lax.fori_loop(clo_ref[w], chi_ref[w] + 1, chunk, 0)
        x_ref[...] = o_ref[...].astype(BF16)
        o_ref[...] = jnp.zeros_like(o_ref)

    full = live_ref[b * halves + halves - 1] > 0

    @pl.when(full)
    def _():
        _swiglu_step(x_ref, wg_ref.at[0], wu_ref.at[0], wd_ref.at[0], o_ref)

    @pl.when((live_ref[b * halves] > 0) & jnp.logical_not(full))
    def _():
        part = pl.ds(0, MOE_HALF)
        _swiglu_step(x_ref.at[part], wg_ref.at[0], wu_ref.at[0], wd_ref.at[0], o_ref.at[part])


def _moe_ffn(block_e, half_live, c_lo, c_hi, dest_t, tok, w_gu, w_down, n_blocks, tf=512):
    T, D = tok.shape
    F = w_down.shape[1]
    nf = F // tf
    tm = MOE_ROWS

    def fstep(b, f, live):
        return jnp.where(live[b * (tm // MOE_HALF)] > 0, f, nf - 1)

    grid_spec = pltpu.PrefetchScalarGridSpec(
        num_scalar_prefetch=4,
        grid=(n_blocks, nf),
        in_specs=[pl.BlockSpec((2, T), lambda b, f, *_: (0, 0)),
                  pl.BlockSpec(memory_space=pl.ANY),
                  pl.BlockSpec((1, D, tf), lambda b, f, be, live, *_: (be[b], 0, fstep(b, f, live))),
                  pl.BlockSpec((1, D, tf), lambda b, f, be, live, *_: (be[b], 0, fstep(b, f, live) + nf)),
                  pl.BlockSpec((1, tf, D), lambda b, f, be, live, *_: (be[b], fstep(b, f, live), 0))],
        out_specs=pl.BlockSpec((tm, D), lambda b, f, *_: (b, 0)),
        scratch_shapes=[pltpu.VMEM((T, D), BF16), pltpu.VMEM((tm, D), BF16), pltpu.SemaphoreType.DMA(())],
    )
    return pl.pallas_call(
        _moe_ffn_kernel,
        grid_spec=grid_spec,
        out_shape=jax.ShapeDtypeStruct((n_blocks * tm, D), F32),
        compiler_params=_cparams("arbitrary", "arbitrary", vmem_limit=VMEM_LIMIT_MOE),
        name="moe_ffn",
    )(block_e, half_live, c_lo, c_hi, dest_t, tok, w_gu, w_gu, w_down)


def _fox_pre_kernel(h_ref, ng_ref, w_ref, wf_ref, bf_ref, qg_ref, kg_ref, sel_ref, selt_ref, pq_ref, pk_ref,
                    oneq_ref, onek_ref, q_ref, k_ref, v_ref, og_ref, c2_ref, augq_ref, augk_ref, carry_ref):
    i = pl.program_id(0)
    tm, D = h_ref.shape
    H = c2_ref.shape[1]

    @pl.when(i == 0)
    def _():
        carry_ref[...] = jnp.zeros_like(carry_ref)

    u = _rms(h_ref[...], ng_ref[...]).astype(BF16)

    q = jnp.dot(u, w_ref[:, 0:D], preferred_element_type=F32)
    k = jnp.dot(u, w_ref[:, D:2 * D], preferred_element_type=F32)
    fl = jnp.dot(u, wf_ref[...], preferred_element_type=F32) + bf_ref[...]
    v_ref[...] = jnp.dot(u, w_ref[:, 2 * D:3 * D], preferred_element_type=F32).astype(BF16)
    og_ref[...] = _sigmoid(jnp.dot(u, w_ref[:, 3 * D:4 * D], preferred_element_type=F32)).astype(BF16)

    mean_sq = [_dot(x * x, sel_ref[...]) * (1.0 / HEAD_DIM) for x in (q, k)]

    log_f = jnp.minimum(fl, 0.0) - jnp.log(1.0 + jnp.exp(-jnp.abs(fl)))
    row = lax.broadcasted_iota(jnp.int32, (tm, tm), 0)
    col = lax.broadcasted_iota(jnp.int32, (tm, tm), 1)
    tri = jnp.where(col <= row, 1.0, 0.0).astype(BF16)
    hi, lo = _split_hi_lo(log_f)
    lo2 = (log_f - hi.astype(F32) - lo.astype(F32)).astype(BF16)
    c = (jnp.dot(tri, hi, preferred_element_type=F32) + jnp.dot(tri, lo, preferred_element_type=F32)
         + jnp.dot(tri, lo2, preferred_element_type=F32)) + carry_ref[...]
    carry_ref[...] = c[tm - 1:tm, :]

    def spread(ms):
        inv = lax.rsqrt(ms + RMS_EPS)
        inv_hi, inv_lo = _split_hi_lo(inv)
        first = lax.broadcasted_iota(jnp.int32, inv.shape, 1) < H
        return jnp.dot(jnp.where(first, inv_hi, inv_lo), selt_ref[...], preferred_element_type=F32)

    q_inv, k_inv = [spread(ms) for ms in mean_sq]
    q_ref[...] = (q * q_inv * qg_ref[...] * (HEAD_DIM ** -0.5 * LOG2E)).astype(BF16)
    k_ref[...] = (k * k_inv * kg_ref[...]).astype(BF16)

    c2 = c * LOG2E
    c2_ref[...] = c2[:, 0:H]
    c_hi, c_lo = _split_hi_lo(c2)
    c_lo2 = (c2 - c_hi.astype(F32) - c_lo.astype(F32)).astype(BF16)
    group = lax.broadcasted_iota(jnp.int32, c2.shape, 1) // H
    pieces = jnp.where(group == 0, c_hi, jnp.where(group == 1, c_lo, c_lo2))
    augq_ref[...] = (jnp.dot(pieces, pq_ref[...], preferred_element_type=F32) + oneq_ref[...]).astype(BF16)
    augk_ref[...] = (jnp.dot(pieces, pk_ref[...], preferred_element_type=F32) + onek_ref[...]).astype(BF16)


def _fox_pre(h, ng, w_main, w_f, b_f, q_gain, k_gain, tm=512):
    T, D = h.shape
    H = D // HEAD_DIM
    sel = (jnp.arange(D)[:, None] // HEAD_DIM == jnp.arange(2 * H)[None, :] % H).astype(BF16)
    piece = jnp.arange(3 * H)[:, None] // H
    head = jnp.arange(3 * H)[:, None] % H
    lane = jnp.arange(D)[None, :]
    at = lambda j: lane == head * HEAD_DIM + j
    p_q = at(piece).astype(BF16)
    p_k = -at(piece + 3).astype(BF16)
    off = jnp.arange(D) % HEAD_DIM
    one_q = ((off >= 3) & (off < 6)).astype(F32).reshape(1, D)
    one_k = (off < 3).astype(F32).reshape(1, D)
    row = pl.BlockSpec((tm, D), lambda i: (i, 0))
    ins = [ng, w_main, jnp.tile(w_f, (1, 3)), jnp.tile(b_f, (1, 3)), q_gain, k_gain, sel, sel.T, p_q, p_k,
           one_q, one_k]
    return pl.pallas_call(
        _fox_pre_kernel,
        grid=(T // tm,),
        in_specs=[row] + [_full(a.shape) for a in ins],
        out_specs=[row] * 4 + [pl.BlockSpec((tm, H), lambda i: (i, 0))] + [row] * 2,
        out_shape=([jax.ShapeDtypeStruct((T, D), BF16)] * 4 + [jax.ShapeDtypeStruct((T, H), F32)]
                   + [jax.ShapeDtypeStruct((T, D), BF16)] * 2),
        scratch_shapes=[pltpu.VMEM((1, 3 * H), F32)],
        compiler_params=_cparams("arbitrary"),
        name="fox_pre",
    )(h, *ins)


def _fox_attn_kernel(jlo_ref, q_ref, augq_ref, k_ref, augk_ref, vt_ref, og_ref, o_ref, s_ref, *, blk):
    p_idx = pl.program_id(0)
    i = pl.program_id(1)
    head0 = lax.broadcasted_iota(jnp.int32, (1, PAIR), 1) < HEAD_DIM
    q, augq = q_ref[...], augq_ref[...]
    zero = jnp.zeros_like(q)
    q_heads = [jnp.concatenate([jnp.where(m, q, zero), jnp.where(m, augq, zero)], axis=1)
               for m in (head0, jnp.logical_not(head0))]
    causal = (lax.broadcasted_iota(jnp.int32, (blk, blk), 0)
              <= lax.broadcasted_iota(jnp.int32, (blk, blk), 1))

    def scores_into(j, slot):
        start = pl.multiple_of(j * blk, blk)
        kb = jnp.concatenate([k_ref[pl.ds(start, blk), :], augk_ref[pl.ds(start, blk), :]], axis=1)
        for h in range(2):
            s_ref[slot, h] = lax.dot_general(kb, q_heads[h], (((1,), (1,)), ((), ())),
                                             preferred_element_type=F32)

    def softmax_pv(j, slot, carry, masked):
        start = pl.multiple_of(j * blk, blk)
        out = []
        for h in range(2):
            m_prev, l_prev, acc_prev = carry[h]
            vtb = vt_ref[h * HEAD_DIM:(h + 1) * HEAD_DIM, pl.ds(start, blk)]
            s = s_ref[slot, h]
            if masked:
                s = jnp.where(causal, s, -jnp.inf)
            m_new = jnp.maximum(m_prev, jnp.max(s, axis=0, keepdims=True))
            alpha = jnp.exp2(m_prev - m_new)
            p = jnp.exp2(s - m_new)
            l_new = alpha * l_prev + jnp.sum(p, axis=0, keepdims=True)
            acc_new = alpha * acc_prev + jnp.dot(vtb, p.astype(BF16), preferred_element_type=F32)
            out.append((m_new, l_new, acc_new))
        return tuple(out)

    init = tuple((jnp.full((1, blk), -jnp.inf, F32), jnp.zeros((1, blk), F32), jnp.zeros((HEAD_DIM, blk), F32))
                 for _ in range(2))
    jlo = jlo_ref[p_idx * pl.num_programs(1) + i]
    total = i - jlo + 1
    scores_into(jlo, 0)

    def two_blocks(t, carry):
        j = jlo + 2 * t
        scores_into(j + 1, 1)
        carry = softmax_pv(j, 0, carry, False)
        scores_into(j + 2, 0)
        return softmax_pv(j + 1, 1, carry, False)

    res = lax.fori_loop(0, (total - 1) // 2, two_blocks, init)

    def last_one(carry):
        return softmax_pv(i, 0, carry, True)

    def last_two(carry):
        scores_into(i, 1)
        carry = softmax_pv(i - 1, 0, carry, False)
        return softmax_pv(i, 1, carry, True)

    res = lax.cond(total % 2 == 1, last_one, last_two, res)
    o_t = jnp.concatenate([res[0][2] / res[0][1], res[1][2] / res[1][1]], axis=0)
    o_ref[...] = (o_t.T * og_ref[...].astype(F32)).astype(o_ref.dtype)


def _fox_attn(jlo, q, augq, k, augk, v_t, og, blk):
    T, D = q.shape
    n_pairs = D // PAIR
    qspec = pl.BlockSpec((blk, PAIR), lambda p, i, jlo: (i, p))
    kspec = pl.BlockSpec((T, PAIR), lambda p, i, jlo: (0, p))
    grid_spec = pltpu.PrefetchScalarGridSpec(
        num_scalar_prefetch=1,
        grid=(n_pairs, T // blk),
        in_specs=[qspec, qspec, kspec, kspec,
                  pl.BlockSpec((PAIR, T), lambda p, i, jlo: (p, 0)),
                  qspec],
        out_specs=qspec,
        scratch_shapes=[pltpu.VMEM((2, 2, blk, blk), F32)],
    )
    return pl.pallas_call(
        functools.partial(_fox_attn_kernel, blk=blk),
        grid_spec=grid_spec,
        out_shape=jax.ShapeDtypeStruct((T, D), BF16),
        compiler_params=_cparams("parallel", "arbitrary"),
        name="fox_attn",
    )(jlo, q, augq, k, augk, v_t, og)


def _first_live_block(c2, q_gain, k_gain, blk):
    T, H = c2.shape
    bound = 8.0 * LOG2E * 1.02 * jnp.max(jnp.abs(q_gain)) * jnp.max(jnp.abs(k_gain))
    c_first = c2[0::blk]
    c_last = c2[blk - 1::blk]
    dead = c_last[None, :, :] > c_first[:, None, :] + (2.0 * bound + 104.0 * LOG2E + 1.0)
    jlo = jnp.sum(dead, axis=1).astype(jnp.int32)
    jlo = jnp.min(jlo.reshape(-1, H // 2, 2), axis=-1)
    return jlo.T.reshape(-1)


def _router_kernel(h_ref, x_ref, wo_ref, ng_ref, wr_ref, h1_ref, u_ref, info_ref, cnt_ref, csr_ref, carry_ref):
    i = pl.program_id(0)
    tm = h_ref.shape[0]
    E = N_EXPERTS
    parts = ROUTER_PARTS if tm % (8 * ROUTER_PARTS) == 0 else 1
    pm = tm // parts
    rows = [slice(j * pm, (j + 1) * pm) for j in range(parts)]

    @pl.when(i == 0)
    def _():
        carry_ref[...] = jnp.zeros_like(carry_ref)

    csr_ref[0] = carry_ref[...]
    h1 = [h_ref[r, :] + jnp.dot(x_ref[r, :], wo_ref[...], preferred_element_type=F32) for r in rows]
    for r, z in zip(rows, h1):
        h1_ref[r, :] = z
    u = [_rms(z, ng_ref[...]) for z in h1]
    for r, z in zip(rows, u):
        u_ref[r, :] = z.astype(BF16)
    w_hi, w_lo = _split_hi_lo(wr_ref[...])
    logits = []
    for z in u:
        u_hi, u_lo = _split_hi_lo(z)
        logits.append(jnp.dot(u_hi, w_hi, preferred_element_type=F32)
                      + jnp.dot(u_hi, w_lo, preferred_element_type=F32)
                      + jnp.dot(u_lo, w_hi, preferred_element_type=F32))
    lane = lax.broadcasted_iota(jnp.int32, (pm, E), 1)
    row = lax.broadcasted_iota(jnp.int32, (pm, pm), 0)
    col = lax.broadcasted_iota(jnp.int32, (pm, pm), 1)
    tri_strict = jnp.where(col < row, 1.0, 0.0).astype(BF16)
    picks = []
    for lg in logits:
        v1 = jnp.max(lg, axis=-1, keepdims=True)
        e1 = jnp.min(jnp.where(lg == v1, lane, E), axis=-1, keepdims=True)
        rest = jnp.where(lane == e1, -jnp.inf, lg)
        v2 = jnp.max(rest, axis=-1, keepdims=True)
        e2 = jnp.min(jnp.where(rest == v2, lane, E), axis=-1, keepdims=True)
        g1 = 1.0 / (1.0 + jnp.exp(v2 - v1))
        oh1 = lane == e1
        oh2 = lane == e2
        onehot = jnp.where(oh1 | oh2, 1.0, 0.0).astype(BF16)
        picks.append((e1, e2, g1, oh1, oh2, onehot))
    within = [jnp.dot(tri_strict, p[5], preferred_element_type=F32) for p in picks]
    counted = carry_ref[...]
    for r, (e1, e2, g1, oh1, oh2, onehot), inside in zip(rows, picks, within):
        before = inside + counted
        rank1 = jnp.sum(jnp.where(oh1, before, 0.0), axis=-1, keepdims=True)
        rank2 = jnp.sum(jnp.where(oh2, before, 0.0), axis=-1, keepdims=True)
        info = jnp.where(lane == 0, e1.astype(F32), 0.0)
        info = jnp.where(lane == 1, e2.astype(F32), info)
        info = jnp.where(lane == 2, g1, info)
        info = jnp.where(lane == 3, 1.0 - g1, info)
        info = jnp.where(lane == 4, rank1, info)
        info = jnp.where(lane == 5, rank2, info)
        info_ref[r, :] = info
        counted = counted + jnp.sum(onehot.astype(F32), axis=0, keepdims=True)
    carry_ref[...] = counted
    cnt_ref[...] = counted


def _router(h, x, w_o, ng, w_router):
    T, D = h.shape
    E = N_EXPERTS
    tm = min(MOE_CHUNK, T)
    row = pl.BlockSpec((tm, D), lambda i: (i, 0))
    return pl.pallas_call(
        _router_kernel,
        grid=(T // tm,),
        in_specs=[row, row, _full((D, D)), _full((1, D)), _full((D, E))],
        out_specs=[row, row, pl.BlockSpec((tm, E), lambda i: (i, 0)), _full((1, E)),
                   pl.BlockSpec((1, 1, E), lambda i: (i, 0, 0))],
        out_shape=[jax.ShapeDtypeStruct((T, D), F32), jax.ShapeDtypeStruct((T, D), BF16),
                   jax.ShapeDtypeStruct((T, E), F32), jax.ShapeDtypeStruct((1, E), F32),
                   jax.ShapeDtypeStruct((T // tm, 1, E), F32)],
        scratch_shapes=[pltpu.VMEM((1, E), F32)],
        compiler_params=_cparams("arbitrary"),
        name="router",
    )(h, x, w_o, ng, w_router)


def _combine_kernel(pos_ref, h_ref, gate_ref, fg_ref, y_ref, o_ref, ya_ref, yb_ref, sem):
    b = pl.program_id(0)
    nb = pl.num_programs(0)
    rows = h_ref.shape[0]

    def copies(blk, r):
        slot = blk % 2
        base = 2 * (blk * rows + r)
        return (pltpu.make_async_copy(y_ref.at[pl.ds(pos_ref[base], 1)], ya_ref.at[slot, pl.ds(r, 1)],
                                      sem.at[slot]),
                pltpu.make_async_copy(y_ref.at[pl.ds(pos_ref[base + 1], 1)], yb_ref.at[slot, pl.ds(r, 1)],
                                      sem.at[slot]))

    def issue(blk):
        def body(r, _):
            ca, cb = copies(blk, r)
            ca.start()
            cb.start()
            return 0
        lax.fori_loop(0, rows, body, 0, unroll=8)

    @pl.when(b == 0)
    def _():
        issue(0)

    @pl.when(b + 1 < nb)
    def _():
        issue(b + 1)

    def wait(r, _):
        ca, cb = copies(b, r)
        ca.wait()
        cb.wait()
        return 0

    lax.fori_loop(0, rows, wait, 0, unroll=8)
    slot = b % 2
    gate = gate_ref[...]
    moe = gate[:, 0:1] * ya_ref[slot] + gate[:, 1:2] * yb_ref[slot]
    o_ref[...] = _rms(h_ref[...] + moe, fg_ref[...])


def _combine(pos, h, gate, final_g, y, rows=512):
    T, D = h.shape
    row = pl.BlockSpec((rows, D), lambda b, pos: (b, 0))
    grid_spec = pltpu.PrefetchScalarGridSpec(
        num_scalar_prefetch=1,
        grid=(T // rows,),
        in_specs=[row, pl.BlockSpec((rows, 2), lambda b, pos: (b, 0)),
                  pl.BlockSpec((1, D), lambda b, pos: (0, 0)), pl.BlockSpec(memory_space=pl.ANY)],
        out_specs=row,
        scratch_shapes=[pltpu.VMEM((2, rows, D), F32), pltpu.VMEM((2, rows, D), F32),
                        pltpu.SemaphoreType.DMA((2,))],
    )
    return pl.pallas_call(
        _combine_kernel,
        grid_spec=grid_spec,
        out_shape=jax.ShapeDtypeStruct((T, D), F32),
        compiler_params=_cparams("arbitrary"),
        name="moe_combine",
    )(pos, h, gate, final_g, y)


def _rwkv_layer(h, ng, mu, w_rkv, w0, w1, w2, a0, a1, a2, g1, g2, k_k, k_a, r_k, gn_g, gn_b, w_o):
    D = h.shape[1]
    vec = lambda z: z.reshape(1, D)
    b16 = lambda z: z.astype(BF16)
    r, k, v, lw, a, g = _rwkv_pre(h, vec(ng), mu, b16(w_rkv), vec(w0), b16(w1), b16(w2), vec(a0),
                                  b16(a1), b16(a2), b16(g1), b16(g2))
    o = _rwkv_scan(r, k, v, lw, a, g, vec(k_k), vec(k_a), vec(r_k), vec(gn_g), vec(gn_b))
    return o, b16(w_o)


def _fox_layer(h, ng, w_in, b_f, q_gain, k_gain, w_o):
    T, D = h.shape
    H = D // HEAD_DIM
    order = jnp.argsort(b_f)
    w_main = w_in[:, :4 * D].reshape(D, 4, H, HEAD_DIM)[:, :, order, :].reshape(D, 4 * D)
    w_out = w_o.reshape(H, HEAD_DIM, D)[order].reshape(D, D)
    q, k, v, og, c2, augq, augk = _fox_pre(
        h, ng.reshape(1, D), w_main.astype(BF16), w_in[:, 4 * D:][:, order].astype(BF16),
        b_f[order].reshape(1, H), jnp.tile(q_gain, H).reshape(1, D), jnp.tile(k_gain, H).reshape(1, D))
    blk = min(ATTN_BLOCK, T)
    jlo = _first_live_block(c2, q_gain, k_gain, blk)
    o = _fox_attn(jlo, q, augq, k, augk, v.T, og, blk)
    return o, w_out.astype(BF16)


def _moe_layer(h, x, w_o, ng, final_g, w_router, w_gu, w_down):
    T, D = h.shape
    E, RB = N_EXPERTS, MOE_ROWS
    h, u, info, cnt, csr = _router(h, x, w_o, ng.reshape(1, D), w_router)
    e = info[:, 0:2].astype(jnp.int32)
    gate = info[:, 2:4]
    rank = info[:, 4:6].astype(jnp.int32)
    counts = cnt[0].astype(jnp.int32)
    padded = (counts + RB - 1) // RB * RB
    ends = jnp.cumsum(padded)
    starts = ends - padded
    n_blocks = (2 * T) // RB + E
    dest = starts[e] + rank
    blocks = jnp.arange(n_blocks, dtype=jnp.int32)
    block_e = jnp.minimum(jnp.sum(blocks[:, None] * RB >= ends[None, :], axis=1), E - 1).astype(jnp.int32)

    def first_rank(size):
        groups = jnp.arange(n_blocks * (RB // size), dtype=jnp.int32)
        group_e = jnp.repeat(block_e, RB // size)
        rank0 = groups * size - starts[group_e]
        return group_e, rank0, rank0 < counts[group_e]

    _, _, half_live = first_rank(MOE_HALF)
    group_e, rank0, group_live = first_rank(MOE_SUB)
    csr_e = csr[:, 0, :].astype(jnp.int32)[:, group_e]
    n_chunks = csr_e.shape[0]
    c_lo = jnp.clip(jnp.sum(csr_e <= rank0[None, :], axis=0) - 1, 0, n_chunks - 1)
    c_hi = jnp.clip(jnp.sum(csr_e < rank0[None, :] + MOE_SUB, axis=0) - 1, 0, n_chunks - 1)
    c_lo = jnp.where(group_live, c_lo, 1).astype(jnp.int32)
    c_hi = jnp.where(group_live, c_hi, 0).astype(jnp.int32)
    y = _moe_ffn(block_e, half_live.astype(jnp.int32), c_lo, c_hi, dest.T, u, w_gu, w_down, n_blocks)
    return _combine(dest.reshape(-1), h, gate, final_g.reshape(1, D), y)


def kernel(x, norm_g, final_g, rwkv_mu, rwkv_w_rkv, rwkv_w0, rwkv_w1, rwkv_w2, rwkv_a0, rwkv_a1,
           rwkv_a2, rwkv_g1, rwkv_g2, rwkv_k_k, rwkv_k_a, rwkv_r_k, rwkv_gn_g, rwkv_gn_b, rwkv_w_o,
           fox_w_in, fox_b_f, fox_q_gain, fox_k_gain, fox_w_o, ffn_w_gu, ffn_w_down,
           moe_w_router, moe_w_gu, moe_w_down):
    B, T, D = x.shape
    h = x.reshape(B * T, D)
    o, w_o = _rwkv_layer(h, norm_g[0, 0], rwkv_mu[0], rwkv_w_rkv[0], rwkv_w0[0], rwkv_w1[0], rwkv_w2[0],
                    rwkv_a0[0], rwkv_a1[0], rwkv_a2[0], rwkv_g1[0], rwkv_g2[0], rwkv_k_k[0],
                    rwkv_k_a[0], rwkv_r_k[0], rwkv_gn_g[0], rwkv_gn_b[0], rwkv_w_o[0])
    h = _ffn(h, o, w_o, norm_g[0, 1].reshape(1, D), ffn_w_gu[0].astype(BF16), ffn_w_down[0].astype(BF16))
    o, w_o = _fox_layer(h, norm_g[1, 0], fox_w_in[0], fox_b_f[0], fox_q_gain[0], fox_k_gain[0], fox_w_o[0])
    out = _moe_layer(h, o, w_o, norm_g[1, 1], final_g, moe_w_router[0], moe_w_gu[0], moe_w_down[0])
    return out.reshape(B, T, D)
```

```python
import functools

import jax
import jax.numpy as jnp
from jax import lax
from jax.experimental import pallas as pl
from jax.experimental.pallas import tpu as pltpu

F32 = jnp.float32
BF16 = jnp.bfloat16

HEAD_DIM = 64
PAIR = 2 * HEAD_DIM
GN_EPS = 64e-5
RMS_EPS = 1e-6
N_EXPERTS = 8
SCAN_CHUNK = 64
SCAN_CHUNKS_PER_STEP = 4
MOE_ROWS = 1024
MOE_HALF = 512
MOE_SUB = 128
MOE_CHUNK = 512
ROUTER_PARTS = 2
ATTN_BLOCK = 512
LOG2E = 1.4426950408889634
VMEM_LIMIT = 56 * 1024 * 1024
VMEM_LIMIT_MOE = 63 * 1024 * 1024


def _cparams(*sem, vmem_limit=VMEM_LIMIT):
    return pltpu.CompilerParams(dimension_semantics=sem, vmem_limit_bytes=vmem_limit)


def _dot(a, b):
    return jnp.dot(a.astype(BF16), b.astype(BF16), preferred_element_type=F32)


def _dot_nt(a, b):
    return lax.dot_general(a.astype(BF16), b.astype(BF16), (((1,), (1,)), ((), ())),
                           preferred_element_type=F32)


def _dot_tn(a, b):
    return lax.dot_general(a.astype(BF16), b.astype(BF16), (((0,), (0,)), ((), ())),
                           preferred_element_type=F32)


def _split_hi_lo(x):
    hi = x.astype(BF16)
    lo = (x - hi.astype(F32)).astype(BF16)
    return hi, lo


def _rms(x, g):
    ms = jnp.mean(x * x, axis=-1, keepdims=True)
    return x * lax.rsqrt(ms + RMS_EPS) * g


def _sigmoid(x):
    return 1.0 / (1.0 + jnp.exp(-x))


def _full(shape):
    return pl.BlockSpec(shape, lambda *_: (0,) * len(shape))


def _rwkv_pre_kernel(h_ref, hp_ref, ng_ref, mu_ref, wrkv_ref, w0_ref, w1_ref, w2_ref, a0_ref,
                     a1_ref, a2_ref, g1_ref, g2_ref, r_ref, k_ref, v_ref, lw_ref, a_ref, g_ref):
    i = pl.program_id(0)
    ng = ng_ref[...]
    u = _rms(h_ref[...], ng)
    up = _rms(hp_ref[7:8, :], ng)
    up = jnp.where(i == 0, 0.0, up)
    row = lax.broadcasted_iota(jnp.int32, u.shape, 0)
    u_prev = jnp.where(row == 0, up, pltpu.roll(u, 1, axis=0))
    dx = u_prev - u
    mix = lambda n: u + dx * mu_ref[n:n + 1, :]
    w_mid = jnp.tanh(_dot(mix(3), w1_ref[...]))
    a_mid = _dot(mix(4), a1_ref[...])
    g_mid = _sigmoid(_dot(mix(5), g1_ref[...]))
    r_ref[...] = _dot(mix(0), wrkv_ref[0])
    k_ref[...] = _dot(mix(1), wrkv_ref[1])
    v_ref[...] = _dot(mix(2), wrkv_ref[2])
    z = w0_ref[...] + _dot(w_mid, w2_ref[...])
    softplus_neg = jnp.maximum(-z, 0.0) + jnp.log(1.0 + jnp.exp(-jnp.abs(z)))
    lw_ref[...] = -jnp.exp(-softplus_neg - 0.5)
    a_ref[...] = _sigmoid(a0_ref[...] + _dot(a_mid, a2_ref[...]))
    g_ref[...] = _dot(g_mid, g2_ref[...])


def _rwkv_pre(h, ng, mu, wrkv, w0, w1, w2, a0, a1, a2, g1, g2, tm=512):
    T, D = h.shape
    row = pl.BlockSpec((tm, D), lambda i: (i, 0))
    prev = pl.BlockSpec((8, D), lambda i: (jnp.maximum(i * (tm // 8) - 1, 0), 0))
    ins = [ng, mu, wrkv, w0, w1, w2, a0, a1, a2, g1, g2]
    return pl.pallas_call(
        _rwkv_pre_kernel,
        grid=(T // tm,),
        in_specs=[row, prev] + [_full(a.shape) for a in ins],
        out_specs=[row] * 6,
        out_shape=[jax.ShapeDtypeStruct((T, D), F32)] * 6,
        compiler_params=_cparams("parallel"),
        name="rwkv_pre",
    )(h, h, *ins)


def _scan_kernel(r_ref, k_ref, v_ref, lw_ref, a_ref, g_ref, kk_ref, ka_ref, rk_ref, gng_ref,
                 gnb_ref, o_ref, s_ref):
    L = SCAN_CHUNK
    n_pairs = r_ref.shape[1] // PAIR

    @pl.when(pl.program_id(0) == 0)
    def _():
        s_ref[...] = jnp.zeros_like(s_ref)

    row = lax.broadcasted_iota(jnp.int32, (L, L), 0)
    col = lax.broadcasted_iota(jnp.int32, (L, L), 1)
    tri_incl = jnp.where(col <= row, 1.0, 0.0).astype(BF16)
    eye = jnp.where(col == row, 1.0, 0.0).astype(F32)
    lane = lax.broadcasted_iota(jnp.int32, (1, PAIR), 1)
    head0 = lane < HEAD_DIM
    arow = lax.broadcasted_iota(jnp.int32, (4 * L, PAIR), 0)
    acol = lax.broadcasted_iota(jnp.int32, (4 * L, PAIR), 1)
    t_idx = arow % L + jnp.where(arow < 2 * L, 0, 1)
    j_idx = acol % L
    a_mask = j_idx < t_idx
    brow = lax.broadcasted_iota(jnp.int32, (PAIR, PAIR), 0)
    bcol = lax.broadcasted_iota(jnp.int32, (PAIR, PAIR), 1)
    blockdiag = (brow < HEAD_DIM) == (bcol < HEAD_DIM)

    def head_sum(x):
        s0 = jnp.sum(jnp.where(head0, x, 0.0), axis=-1, keepdims=True)
        s1 = jnp.sum(jnp.where(head0, 0.0, x), axis=-1, keepdims=True)
        return jnp.where(head0, s0, s1)

    pairs = range(n_pairs)
    heads = [(p, h) for p in pairs for h in range(2)]
    sls = [slice(p * PAIR, (p + 1) * PAIR) for p in pairs]
    n_chunks = r_ref.shape[0] // L
    chunks = range(n_chunks)
    rows = [slice(c * L, (c + 1) * L) for c in chunks]
    cp = [(c, p) for c in chunks for p in pairs]
    cph = [(c, p, h) for c in chunks for p, h in heads]

    lws = {(c, p): lw_ref[rows[c], sls[p]] for c, p in cp}
    css = {}
    for k, lw in lws.items():
        both = jnp.dot(tri_incl, jnp.concatenate(_split_hi_lo(lw), axis=1), preferred_element_type=F32)
        css[k] = both[:, :PAIR] + both[:, PAIR:]
    prep = {}
    for c, p in cp:
        rw, sl, cs, lw = rows[c], sls[p], css[c, p], lws[c, p]
        w_inv = jnp.exp(-cs)
        w_last = jnp.exp(cs[L - 1:L, :])
        r = r_ref[rw, sl]
        k = k_ref[rw, sl]
        a = a_ref[rw, sl]
        kk = k * kk_ref[:, sl]
        kk = kk / jnp.maximum(jnp.sqrt(head_sum(kk * kk)), 1e-12)
        kmod = k * (1.0 + (a - 1.0) * ka_ref[:, sl])
        at = -kk * jnp.exp(cs - lw)
        bt = kk * a * w_inv
        kt = kmod * w_inv
        rt = r * jnp.exp(cs)
        v = v_ref[rw, sl]
        prep[c, p] = dict(atrt=jnp.concatenate([at, rt], axis=0).astype(BF16),
                          x_stack=jnp.concatenate([jnp.where(head0, at, 0.0), jnp.where(head0, 0.0, at),
                                                   jnp.where(head0, rt, 0.0), jnp.where(head0, 0.0, rt)],
                                                  axis=0).astype(BF16),
                          bk=jnp.concatenate([bt, kt], axis=0).astype(BF16),
                          bkhat=jnp.concatenate([bt * w_last, kt * w_last], axis=0).astype(BF16),
                          w_last=w_last, v=v, v_bf=v.astype(BF16), rk=r * kmod)
    aas = {k: jnp.where(a_mask, _dot_nt(prep[k]["x_stack"], prep[k]["bk"]), 0.0) for k in cp}
    akv = {}
    pw = {}
    t_inv = {}
    for c, p in cp:
        vv = jnp.concatenate([prep[c, p]["v_bf"], prep[c, p]["v_bf"]], axis=0)
        both = _dot(jnp.where(head0, 0.0, aas[c, p][:2 * L, :]), vv)
        akv[c, p, 0], akv[c, p, 1] = both[:L], both[L:]
    for c, p, h in cph:
        a_rows = aas[c, p][h * L:(h + 1) * L, :]
        pw[c, p, h] = a_rows[:, :L]
        t_inv[c, p, h] = eye + pw[c, p, h]
    for _ in range(5):
        for k in cph:
            pw[k] = _dot(pw[k], pw[k])
        for k in cph:
            t_inv[k] = t_inv[k] + _dot(t_inv[k], pw[k])

    states = [s_ref[p] for p in pairs]
    for c in chunks:
        inter = [_dot_nt(prep[c, p]["atrt"], states[p]) for p in pairs]
        u_h = {(p, h): _dot(t_inv[c, p, h], inter[p][:L, :] + akv[c, p, h]) for p, h in heads}
        y_h = {}
        for p, h in heads:
            ar = aas[c, p][(2 + h) * L:(3 + h) * L, :]
            y_h[p, h] = _dot(ar, jnp.concatenate([u_h[p, h].astype(BF16), prep[c, p]["v_bf"]], axis=0))
        new_states = []
        for p in pairs:
            pr = prep[c, p]
            u = jnp.where(head0, u_h[p, 0], u_h[p, 1])
            uv = jnp.concatenate([u.astype(BF16), pr["v_bf"]], axis=0)
            new_states.append(states[p] * pr["w_last"] + jnp.where(blockdiag, _dot_tn(uv, pr["bkhat"]), 0.0))
        states = new_states
        for p in pairs:
            sl = sls[p]
            pr = prep[c, p]
            y = inter[p][L:, :] + jnp.where(head0, y_h[p, 0], y_h[p, 1])
            mean = head_sum(y) * (1.0 / HEAD_DIM)
            d = y - mean
            var = head_sum(d * d) * (1.0 / HEAD_DIM)
            yn = d * lax.rsqrt(var + GN_EPS) * gng_ref[:, sl] + gnb_ref[:, sl]
            bonus = head_sum(pr["rk"] * rk_ref[:, sl]) * pr["v"]
            o_ref[rows[c], sl] = ((yn + bonus) * g_ref[rows[c], sl]).astype(o_ref.dtype)
    for p in pairs:
        s_ref[p] = states[p]


def _rwkv_scan(r, k, v, lw, a, g, k_k, k_a, r_k, gn_g, gn_b):
    T, D = r.shape
    rows = SCAN_CHUNK * SCAN_CHUNKS_PER_STEP
    row = pl.BlockSpec((rows, D), lambda c: (c, 0))
    par = _full((1, D))
    return pl.pallas_call(
        _scan_kernel,
        grid=(T // rows,),
        in_specs=[row] * 6 + [par] * 5,
        out_specs=row,
        out_shape=jax.ShapeDtypeStruct((T, D), BF16),
        scratch_shapes=[pltpu.VMEM((D // PAIR, PAIR, PAIR), F32)],
        compiler_params=_cparams("arbitrary"),
        name="rwkv_scan",
    )(r, k, v, lw, a, g, k_k, k_a, r_k, gn_g, gn_b)


def _swiglu_step(u_ref, wg_ref, wu_ref, wd_ref, acc_ref):
    u = u_ref[...]
    gate = jnp.dot(u, wg_ref[...].astype(BF16), preferred_element_type=F32)
    up = jnp.dot(u, wu_ref[...].astype(BF16), preferred_element_type=F32)
    hid = (gate * _sigmoid(gate) * up).astype(BF16)
    acc_ref[...] += jnp.dot(hid, wd_ref[...].astype(BF16), preferred_element_type=F32)


def _ffn_kernel(h_ref, x_ref, wo_ref, ng_ref, wg_ref, wu_ref, wd_ref, o_ref, u_ref):
    @pl.when(pl.program_id(1) == 0)
    def _():
        h1 = h_ref[...] + jnp.dot(x_ref[...], wo_ref[...], preferred_element_type=F32)
        o_ref[...] = h1
        u_ref[...] = _rms(h1, ng_ref[...]).astype(BF16)

    _swiglu_step(u_ref, wg_ref, wu_ref, wd_ref, o_ref)


def _ffn(h, x, w_o, ng, w_gu, w_down, tm=512, tf=1792):
    T, D = h.shape
    F = w_down.shape[0]
    nf = F // tf
    row = pl.BlockSpec((tm, D), lambda i, f: (i, 0))
    return pl.pallas_call(
        _ffn_kernel,
        grid=(T // tm, nf),
        in_specs=[row, row, _full((D, D)), _full((1, D)),
                  pl.BlockSpec((D, tf), lambda i, f: (0, f)),
                  pl.BlockSpec((D, tf), lambda i, f: (0, f + nf)),
                  pl.BlockSpec((tf, D), lambda i, f: (f, 0))],
        out_specs=row,
        out_shape=jax.ShapeDtypeStruct((T, D), F32),
        scratch_shapes=[pltpu.VMEM((tm, D), BF16)],
        compiler_params=_cparams("parallel", "arbitrary"),
        name="ffn",
    )(h, x, w_o, ng, w_gu, w_gu, w_down)


def _moe_ffn_kernel(be_ref, live_ref, clo_ref, chi_ref, dest_ref, tok_hbm, wg_ref, wu_ref, wd_ref, o_ref,
                    tok_ref, x_ref, sem):
    b = pl.program_id(0)
    f = pl.program_id(1)
    rows = o_ref.shape[0]
    ch = MOE_CHUNK
    halves = rows // MOE_HALF

    @pl.when((b == 0) & (f == 0))
    def _():
        cp = pltpu.make_async_copy(tok_hbm, tok_ref, sem)
        cp.start()
        cp.wait()

    @pl.when(f == 0)
    def _():
        o_ref[...] = jnp.zeros_like(o_ref)
        for s in range(rows // MOE_SUB):
            sub = slice(s * MOE_SUB, (s + 1) * MOE_SUB)
            row_id = b * rows + s * MOE_SUB + lax.broadcasted_iota(jnp.int32, (MOE_SUB, 1), 0)
            w = b * (rows // MOE_SUB) + s

            def chunk(c, _, sub=sub, row_id=row_id):
                start = pl.multiple_of(c * ch, ch)
                d = dest_ref[:, pl.ds(start, ch)]
                onehot = (jnp.where(row_id == d[0:1, :], 1.0, 0.0)
                          + jnp.where(row_id == d[1:2, :], 1.0, 0.0)).astype(BF16)
                o_ref[sub, :] += jnp.dot(onehot, tok_ref[pl.ds(start, ch), :], preferred_element_type=F32)
                return 0

            lax.fori_loop(clo_ref[w], chi_ref[w] + 1, chunk, 0)
        x_ref[...] = o_ref[...].astype(BF16)
        o_ref[...] = jnp.zeros_like(o_ref)

    full = live_ref[b * halves + halves - 1] > 0

    @pl.when(full)
    def _():
        _swiglu_step(x_ref, wg_ref.at[0], wu_ref.at[0], wd_ref.at[0], o_ref)

    @pl.when((live_ref[b * halves] > 0) & jnp.logical_not(full))
    def _():
        part = pl.ds(0, MOE_HALF)
        _swiglu_step(x_ref.at[part], wg_ref.at[0], wu_ref.at[0], wd_ref.at[0], o_ref.at[part])


def _moe_ffn(block_e, half_live, c_lo, c_hi, dest_t, tok, w_gu, w_down, n_blocks, tf=512):
    T, D = tok.shape
    F = w_down.shape[1]
    nf = F // tf
    tm = MOE_ROWS

    def fstep(b, f, live):
        return jnp.where(live[b * (tm // MOE_HALF)] > 0, f, nf - 1)

    grid_spec = pltpu.PrefetchScalarGridSpec(
        num_scalar_prefetch=4,
        grid=(n_blocks, nf),
        in_specs=[pl.BlockSpec((2, T), lambda b, f, *_: (0, 0)),
                  pl.BlockSpec(memory_space=pl.ANY),
                  pl.BlockSpec((1, D, tf), lambda b, f, be, live, *_: (be[b], 0, fstep(b, f, live))),
                  pl.BlockSpec((1, D, tf), lambda b, f, be, live, *_: (be[b], 0, fstep(b, f, live) + nf)),
                  pl.BlockSpec((1, tf, D), lambda b, f, be, live, *_: (be[b], fstep(b, f, live), 0))],
        out_specs=pl.BlockSpec((tm, D), lambda b, f, *_: (b, 0)),
        scratch_shapes=[pltpu.VMEM((T, D), BF16), pltpu.VMEM((tm, D), BF16), pltpu.SemaphoreType.DMA(())],
    )
    return pl.pallas_call(
        _moe_ffn_kernel,
        grid_spec=grid_spec,
        out_shape=jax.ShapeDtypeStruct((n_blocks * tm, D), F32),
        compiler_params=_cparams("arbitrary", "arbitrary", vmem_limit=VMEM_LIMIT_MOE),
        name="moe_ffn",
    )(block_e, half_live, c_lo, c_hi, dest_t, tok, w_gu, w_gu, w_down)


def _fox_pre_kernel(h_ref, ng_ref, w_ref, wf_ref, bf_ref, qg_ref, kg_ref, sel_ref, selt_ref, pq_ref, pk_ref,
                    oneq_ref, onek_ref, q_ref, k_ref, v_ref, og_ref, c2_ref, augq_ref, augk_ref, carry_ref):
    i = pl.program_id(0)
    tm, D = h_ref.shape
    H = c2_ref.shape[1]

    @pl.when(i == 0)
    def _():
        carry_ref[...] = jnp.zeros_like(carry_ref)

    u = _rms(h_ref[...], ng_ref[...]).astype(BF16)

    q = jnp.dot(u, w_ref[:, 0:D], preferred_element_type=F32)
    k = jnp.dot(u, w_ref[:, D:2 * D], preferred_element_type=F32)
    fl = jnp.dot(u, wf_ref[...], preferred_element_type=F32) + bf_ref[...]
    v_ref[...] = jnp.dot(u, w_ref[:, 2 * D:3 * D], preferred_element_type=F32).astype(BF16)
    og_ref[...] = _sigmoid(jnp.dot(u, w_ref[:, 3 * D:4 * D], preferred_element_type=F32)).astype(BF16)

    mean_sq = [_dot(x * x, sel_ref[...]) * (1.0 / HEAD_DIM) for x in (q, k)]

    log_f = jnp.minimum(fl, 0.0) - jnp.log(1.0 + jnp.exp(-jnp.abs(fl)))
    row = lax.broadcasted_iota(jnp.int32, (tm, tm), 0)
    col = lax.broadcasted_iota(jnp.int32, (tm, tm), 1)
    tri = jnp.where(col <= row, 1.0, 0.0).astype(BF16)
    hi, lo = _split_hi_lo(log_f)
    lo2 = (log_f - hi.astype(F32) - lo.astype(F32)).astype(BF16)
    c = (jnp.dot(tri, hi, preferred_element_type=F32) + jnp.dot(tri, lo, preferred_element_type=F32)
         + jnp.dot(tri, lo2, preferred_element_type=F32)) + carry_ref[...]
    carry_ref[...] = c[tm - 1:tm, :]

    def spread(ms):
        inv = lax.rsqrt(ms + RMS_EPS)
        inv_hi, inv_lo = _split_hi_lo(inv)
        first = lax.broadcasted_iota(jnp.int32, inv.shape, 1) < H
        return jnp.dot(jnp.where(first, inv_hi, inv_lo), selt_ref[...], preferred_element_type=F32)

    q_inv, k_inv = [spread(ms) for ms in mean_sq]
    q_ref[...] = (q * q_inv * qg_ref[...] * (HEAD_DIM ** -0.5 * LOG2E)).astype(BF16)
    k_ref[...] = (k * k_inv * kg_ref[...]).astype(BF16)

    c2 = c * LOG2E
    c2_ref[...] = c2[:, 0:H]
    c_hi, c_lo = _split_hi_lo(c2)
    c_lo2 = (c2 - c_hi.astype(F32) - c_lo.astype(F32)).astype(BF16)
    group = lax.broadcasted_iota(jnp.int32, c2.shape, 1) // H
    pieces = jnp.where(group == 0, c_hi, jnp.where(group == 1, c_lo, c_lo2))
    augq_ref[...] = (jnp.dot(pieces, pq_ref[...], preferred_element_type=F32) + oneq_ref[...]).astype(BF16)
    augk_ref[...] = (jnp.dot(pieces, pk_ref[...], preferred_element_type=F32) + onek_ref[...]).astype(BF16)


def _fox_pre(h, ng, w_main, w_f, b_f, q_gain, k_gain, tm=512):
    T, D = h.shape
    H = D // HEAD_DIM
    sel = (jnp.arange(D)[:, None] // HEAD_DIM == jnp.arange(2 * H)[None, :] % H).astype(BF16)
    piece = jnp.arange(3 * H)[:, None] // H
    head = jnp.arange(3 * H)[:, None] % H
    lane = jnp.arange(D)[None, :]
    at = lambda j: lane == head * HEAD_DIM + j
    p_q = at(piece).astype(BF16)
    p_k = -at(piece + 3).astype(BF16)
    off = jnp.arange(D) % HEAD_DIM
    one_q = ((off >= 3) & (off < 6)).astype(F32).reshape(1, D)
    one_k = (off < 3).astype(F32).reshape(1, D)
    row = pl.BlockSpec((tm, D), lambda i: (i, 0))
    ins = [ng, w_main, jnp.tile(w_f, (1, 3)), jnp.tile(b_f, (1, 3)), q_gain, k_gain, sel, sel.T, p_q, p_k,
           one_q, one_k]
    return pl.pallas_call(
        _fox_pre_kernel,
        grid=(T // tm,),
        in_specs=[row] + [_full(a.shape) for a in ins],
        out_specs=[row] * 4 + [pl.BlockSpec((tm, H), lambda i: (i, 0))] + [row] * 2,
        out_shape=([jax.ShapeDtypeStruct((T, D), BF16)] * 4 + [jax.ShapeDtypeStruct((T, H), F32)]
                   + [jax.ShapeDtypeStruct((T, D), BF16)] * 2),
        scratch_shapes=[pltpu.VMEM((1, 3 * H), F32)],
        compiler_params=_cparams("arbitrary"),
        name="fox_pre",
    )(h, *ins)


def _fox_attn_kernel(jlo_ref, q_ref, augq_ref, k_ref, augk_ref, vt_ref, og_ref, o_ref, s_ref, *, blk):
    p_idx = pl.program_id(0)
    i = pl.program_id(1)
    head0 = lax.broadcasted_iota(jnp.int32, (1, PAIR), 1) < HEAD_DIM
    q, augq = q_ref[...], augq_ref[...]
    zero = jnp.zeros_like(q)
    q_heads = [jnp.concatenate([jnp.where(m, q, zero), jnp.where(m, augq, zero)], axis=1)
               for m in (head0, jnp.logical_not(head0))]
    causal = (lax.broadcasted_iota(jnp.int32, (blk, blk), 0)
              <= lax.broadcasted_iota(jnp.int32, (blk, blk), 1))

    def scores_into(j, slot):
        start = pl.multiple_of(j * blk, blk)
        kb = jnp.concatenate([k_ref[pl.ds(start, blk), :], augk_ref[pl.ds(start, blk), :]], axis=1)
        for h in range(2):
            s_ref[slot, h] = lax.dot_general(kb, q_heads[h], (((1,), (1,)), ((), ())),
                                             preferred_element_type=F32)

    def softmax_pv(j, slot, carry, masked):
        start = pl.multiple_of(j * blk, blk)
        out = []
        for h in range(2):
            m_prev, l_prev, acc_prev = carry[h]
            vtb = vt_ref[h * HEAD_DIM:(h + 1) * HEAD_DIM, pl.ds(start, blk)]
            s = s_ref[slot, h]
            if masked:
                s = jnp.where(causal, s, -jnp.inf)
            m_new = jnp.maximum(m_prev, jnp.max(s, axis=0, keepdims=True))
            alpha = jnp.exp2(m_prev - m_new)
            p = jnp.exp2(s - m_new)
            l_new = alpha * l_prev + jnp.sum(p, axis=0, keepdims=True)
            acc_new = alpha * acc_prev + jnp.dot(vtb, p.astype(BF16), preferred_element_type=F32)
            out.append((m_new, l_new, acc_new))
        return tuple(out)

    init = tuple((jnp.full((1, blk), -jnp.inf, F32), jnp.zeros((1, blk), F32), jnp.zeros((HEAD_DIM, blk), F32))
                 for _ in range(2))
    jlo = jlo_ref[p_idx * pl.num_programs(1) + i]
    total = i - jlo + 1
    scores_into(jlo, 0)

    def two_blocks(t, carry):
        j = jlo + 2 * t
        scores_into(j + 1, 1)
        carry = softmax_pv(j, 0, carry, False)
        scores_into(j + 2, 0)
        return softmax_pv(j + 1, 1, carry, False)

    res = lax.fori_loop(0, (total - 1) // 2, two_blocks, init)

    def last_one(carry):
        return softmax_pv(i, 0, carry, True)

    def last_two(carry):
        scores_into(i, 1)
        carry = softmax_pv(i - 1, 0, carry, False)
        return softmax_pv(i, 1, carry, True)

    res = lax.cond(total % 2 == 1, last_one, last_two, res)
    o_t = jnp.concatenate([res[0][2] / res[0][1], res[1][2] / res[1][1]], axis=0)
    o_ref[...] = (o_t.T * og_ref[...].astype(F32)).astype(o_ref.dtype)


def _fox_attn(jlo, q, augq, k, augk, v_t, og, blk):
    T, D = q.shape
    n_pairs = D // PAIR
    qspec = pl.BlockSpec((blk, PAIR), lambda p, i, jlo: (i, p))
    kspec = pl.BlockSpec((T, PAIR), lambda p, i, jlo: (0, p))
    grid_spec = pltpu.PrefetchScalarGridSpec(
        num_scalar_prefetch=1,
        grid=(n_pairs, T // blk),
        in_specs=[qspec, qspec, kspec, kspec,
                  pl.BlockSpec((PAIR, T), lambda p, i, jlo: (p, 0)),
                  qspec],
        out_specs=qspec,
        scratch_shapes=[pltpu.VMEM((2, 2, blk, blk), F32)],
    )
    return pl.pallas_call(
        functools.partial(_fox_attn_kernel, blk=blk),
        grid_spec=grid_spec,
        out_shape=jax.ShapeDtypeStruct((T, D), BF16),
        compiler_params=_cparams("parallel", "arbitrary"),
        name="fox_attn",
    )(jlo, q, augq, k, augk, v_t, og)


def _first_live_block(c2, q_gain, k_gain, blk):
    T, H = c2.shape
    bound = 8.0 * LOG2E * 1.02 * jnp.max(jnp.abs(q_gain)) * jnp.max(jnp.abs(k_gain))
    c_first = c2[0::blk]
    c_last = c2[blk - 1::blk]
    dead = c_last[None, :, :] > c_first[:, None, :] + (2.0 * bound + 104.0 * LOG2E + 1.0)
    jlo = jnp.sum(dead, axis=1).astype(jnp.int32)
    jlo = jnp.min(jlo.reshape(-1, H // 2, 2), axis=-1)
    return jlo.T.reshape(-1)


def _router_kernel(h_ref, x_ref, wo_ref, ng_ref, wr_ref, h1_ref, u_ref, info_ref, cnt_ref, csr_ref, carry_ref):
    i = pl.program_id(0)
    tm = h_ref.shape[0]
    E = N_EXPERTS
    parts = ROUTER_PARTS if tm % (8 * ROUTER_PARTS) == 0 else 1
    pm = tm // parts
    rows = [slice(j * pm, (j + 1) * pm) for j in range(parts)]

    @pl.when(i == 0)
    def _():
        carry_ref[...] = jnp.zeros_like(carry_ref)

    csr_ref[0] = carry_ref[...]
    h1 = [h_ref[r, :] + jnp.dot(x_ref[r, :], wo_ref[...], preferred_element_type=F32) for r in rows]
    for r, z in zip(rows, h1):
        h1_ref[r, :] = z
    u = [_rms(z, ng_ref[...]) for z in h1]
    for r, z in zip(rows, u):
        u_ref[r, :] = z.astype(BF16)
    w_hi, w_lo = _split_hi_lo(wr_ref[...])
    logits = []
    for z in u:
        u_hi, u_lo = _split_hi_lo(z)
        logits.append(jnp.dot(u_hi, w_hi, preferred_element_type=F32)
                      + jnp.dot(u_hi, w_lo, preferred_element_type=F32)
                      + jnp.dot(u_lo, w_hi, preferred_element_type=F32))
    lane = lax.broadcasted_iota(jnp.int32, (pm, E), 1)
    row = lax.broadcasted_iota(jnp.int32, (pm, pm), 0)
    col = lax.broadcasted_iota(jnp.int32, (pm, pm), 1)
    tri_strict = jnp.where(col < row, 1.0, 0.0).astype(BF16)
    picks = []
    for lg in logits:
        v1 = jnp.max(lg, axis=-1, keepdims=True)
        e1 = jnp.min(jnp.where(lg == v1, lane, E), axis=-1, keepdims=True)
        rest = jnp.where(lane == e1, -jnp.inf, lg)
        v2 = jnp.max(rest, axis=-1, keepdims=True)
        e2 = jnp.min(jnp.where(rest == v2, lane, E), axis=-1, keepdims=True)
        g1 = 1.0 / (1.0 + jnp.exp(v2 - v1))
        oh1 = lane == e1
        oh2 = lane == e2
        onehot = jnp.where(oh1 | oh2, 1.0, 0.0).astype(BF16)
        picks.append((e1, e2, g1, oh1, oh2, onehot))
    within = [jnp.dot(tri_strict, p[5], preferred_element_type=F32) for p in picks]
    counted = carry_ref[...]
    for r, (e1, e2, g1, oh1, oh2, onehot), inside in zip(rows, picks, within):
        before = inside + counted
        rank1 = jnp.sum(jnp.where(oh1, before, 0.0), axis=-1, keepdims=True)
        rank2 = jnp.sum(jnp.where(oh2, before, 0.0), axis=-1, keepdims=True)
        info = jnp.where(lane == 0, e1.astype(F32), 0.0)
        info = jnp.where(lane == 1, e2.astype(F32), info)
        info = jnp.where(lane == 2, g1, info)
        info = jnp.where(lane == 3, 1.0 - g1, info)
        info = jnp.where(lane == 4, rank1, info)
        info = jnp.where(lane == 5, rank2, info)
        info_ref[r, :] = info
        counted = counted + jnp.sum(onehot.astype(F32), axis=0, keepdims=True)
    carry_ref[...] = counted
    cnt_ref[...] = counted


def _router(h, x, w_o, ng, w_router):
    T, D = h.shape
    E = N_EXPERTS
    tm = min(MOE_CHUNK, T)
    row = pl.BlockSpec((tm, D), lambda i: (i, 0))
    return pl.pallas_call(
        _router_kernel,
        grid=(T // tm,),
        in_specs=[row, row, _full((D, D)), _full((1, D)), _full((D, E))],
        out_specs=[row, row, pl.BlockSpec((tm, E), lambda i: (i, 0)), _full((1, E)),
                   pl.BlockSpec((1, 1, E), lambda i: (i, 0, 0))],
        out_shape=[jax.ShapeDtypeStruct((T, D), F32), jax.ShapeDtypeStruct((T, D), BF16),
                   jax.ShapeDtypeStruct((T, E), F32), jax.ShapeDtypeStruct((1, E), F32),
                   jax.ShapeDtypeStruct((T // tm, 1, E), F32)],
        scratch_shapes=[pltpu.VMEM((1, E), F32)],
        compiler_params=_cparams("arbitrary"),
        name="router",
    )(h, x, w_o, ng, w_router)


def _combine_kernel(pos_ref, h_ref, gate_ref, fg_ref, y_ref, o_ref, ya_ref, yb_ref, sem):
    b = pl.program_id(0)
    nb = pl.num_programs(0)
    rows = h_ref.shape[0]

    def copies(blk, r):
        slot = blk % 2
        base = 2 * (blk * rows + r)
        return (pltpu.make_async_copy(y_ref.at[pl.ds(pos_ref[base], 1)], ya_ref.at[slot, pl.ds(r, 1)],
                                      sem.at[slot]),
                pltpu.make_async_copy(y_ref.at[pl.ds(pos_ref[base + 1], 1)], yb_ref.at[slot, pl.ds(r, 1)],
                                      sem.at[slot]))

    def issue(blk):
        def body(r, _):
            ca, cb = copies(blk, r)
            ca.start(priority=0)
            cb.start(priority=1)
            return 0
        lax.fori_loop(0, rows, body, 0, unroll=8)

    @pl.when(b == 0)
    def _():
        issue(0)

    @pl.when(b + 1 < nb)
    def _():
        issue(b + 1)

    def wait(r, _):
        ca, cb = copies(b, r)
        ca.wait()
        cb.wait()
        return 0

    lax.fori_loop(0, rows, wait, 0, unroll=8)
    slot = b % 2
    gate = gate_ref[...]
    moe = gate[:, 0:1] * ya_ref[slot] + gate[:, 1:2] * yb_ref[slot]
    o_ref[...] = _rms(h_ref[...] + moe, fg_ref[...])


def _combine(pos, h, gate, final_g, y, rows=512):
    T, D = h.shape
    row = pl.BlockSpec((rows, D), lambda b, pos: (b, 0))
    grid_spec = pltpu.PrefetchScalarGridSpec(
        num_scalar_prefetch=1,
        grid=(T // rows,),
        in_specs=[row, pl.BlockSpec((rows, 2), lambda b, pos: (b, 0)),
                  pl.BlockSpec((1, D), lambda b, pos: (0, 0)), pl.BlockSpec(memory_space=pl.ANY)],
        out_specs=row,
        scratch_shapes=[pltpu.VMEM((2, rows, D), F32), pltpu.VMEM((2, rows, D), F32),
                        pltpu.SemaphoreType.DMA((2,))],
    )
    return pl.pallas_call(
        _combine_kernel,
        grid_spec=grid_spec,
        out_shape=jax.ShapeDtypeStruct((T, D), F32),
        compiler_params=_cparams("arbitrary"),
        name="moe_combine",
    )(pos, h, gate, final_g, y)


def _rwkv_layer(h, ng, mu, w_rkv, w0, w1, w2, a0, a1, a2, g1, g2, k_k, k_a, r_k, gn_g, gn_b, w_o):
    D = h.shape[1]
    vec = lambda z: z.reshape(1, D)
    b16 = lambda z: z.astype(BF16)
    r, k, v, lw, a, g = _rwkv_pre(h, vec(ng), mu, b16(w_rkv), vec(w0), b16(w1), b16(w2), vec(a0),
                                  b16(a1), b16(a2), b16(g1), b16(g2))
    o = _rwkv_scan(r, k, v, lw, a, g, vec(k_k), vec(k_a), vec(r_k), vec(gn_g), vec(gn_b))
    return o, b16(w_o)


def _fox_layer(h, ng, w_in, b_f, q_gain, k_gain, w_o):
    T, D = h.shape
    H = D // HEAD_DIM
    order = jnp.argsort(b_f)
    w_main = w_in[:, :4 * D].reshape(D, 4, H, HEAD_DIM)[:, :, order, :].reshape(D, 4 * D)
    w_out = w_o.reshape(H, HEAD_DIM, D)[order].reshape(D, D)
    q, k, v, og, c2, augq, augk = _fox_pre(
        h, ng.reshape(1, D), w_main.astype(BF16), w_in[:, 4 * D:][:, order].astype(BF16),
        b_f[order].reshape(1, H), jnp.tile(q_gain, H).reshape(1, D), jnp.tile(k_gain, H).reshape(1, D))
    blk = min(ATTN_BLOCK, T)
    jlo = _first_live_block(c2, q_gain, k_gain, blk)
    o = _fox_attn(jlo, q, augq, k, augk, v.T, og, blk)
    return o, w_out.astype(BF16)


def _moe_layer(h, x, w_o, ng, final_g, w_router, w_gu, w_down):
    T, D = h.shape
    E, RB = N_EXPERTS, MOE_ROWS
    h, u, info, cnt, csr = _router(h, x, w_o, ng.reshape(1, D), w_router)
    e = info[:, 0:2].astype(jnp.int32)
    gate = info[:, 2:4]
    rank = info[:, 4:6].astype(jnp.int32)
    counts = cnt[0].astype(jnp.int32)
    padded = (counts + RB - 1) // RB * RB
    ends = jnp.cumsum(padded)
    starts = ends - padded
    n_blocks = (2 * T) // RB + E
    dest = starts[e] + rank
    blocks = jnp.arange(n_blocks, dtype=jnp.int32)
    block_e = jnp.minimum(jnp.sum(blocks[:, None] * RB >= ends[None, :], axis=1), E - 1).astype(jnp.int32)

    def first_rank(size):
        groups = jnp.arange(n_blocks * (RB // size), dtype=jnp.int32)
        group_e = jnp.repeat(block_e, RB // size)
        rank0 = groups * size - starts[group_e]
        return group_e, rank0, rank0 < counts[group_e]

    _, _, half_live = first_rank(MOE_HALF)
    group_e, rank0, group_live = first_rank(MOE_SUB)
    csr_e = csr[:, 0, :].astype(jnp.int32)[:, group_e]
    n_chunks = csr_e.shape[0]
    c_lo = jnp.clip(jnp.sum(csr_e <= rank0[None, :], axis=0) - 1, 0, n_chunks - 1)
    c_hi = jnp.clip(jnp.sum(csr_e < rank0[None, :] + MOE_SUB, axis=0) - 1, 0, n_chunks - 1)
    c_lo = jnp.where(group_live, c_lo, 1).astype(jnp.int32)
    c_hi = jnp.where(group_live, c_hi, 0).astype(jnp.int32)
    y = _moe_ffn(block_e, half_live.astype(jnp.int32), c_lo, c_hi, dest.T, u, w_gu, w_down, n_blocks)
    return _combine(dest.reshape(-1), h, gate, final_g.reshape(1, D), y)


def kernel(x, norm_g, final_g, rwkv_mu, rwkv_w_rkv, rwkv_w0, rwkv_w1, rwkv_w2, rwkv_a0, rwkv_a1,
           rwkv_a2, rwkv_g1, rwkv_g2, rwkv_k_k, rwkv_k_a, rwkv_r_k, rwkv_gn_g, rwkv_gn_b, rwkv_w_o,
           fox_w_in, fox_b_f, fox_q_gain, fox_k_gain, fox_w_o, ffn_w_gu, ffn_w_down,
           moe_w_router, moe_w_gu, moe_w_down):
    B, T, D = x.shape
    h = x.reshape(B * T, D)
    o, w_o = _rwkv_layer(h, norm_g[0, 0], rwkv_mu[0], rwkv_w_rkv[0], rwkv_w0[0], rwkv_w1[0], rwkv_w2[0],
                    rwkv_a0[0], rwkv_a1[0], rwkv_a2[0], rwkv_g1[0], rwkv_g2[0], rwkv_k_k[0],
                    rwkv_k_a[0], rwkv_r_k[0], rwkv_gn_g[0], rwkv_gn_b[0], rwkv_w_o[0])
    h = _ffn(h, o, w_o, norm_g[0, 1].reshape(1, D), ffn_w_gu[0].astype(BF16), ffn_w_down[0].astype(BF16))
    o, w_o = _fox_layer(h, norm_g[1, 0], fox_w_in[0], fox_b_f[0], fox_q_gain[0], fox_k_gain[0], fox_w_o[0])
    out = _moe_layer(h, o, w_o, norm_g[1, 1], final_g, moe_w_router[0], moe_w_gu[0], moe_w_down[0])
    return out.reshape(B, T, D)
```
